```python
import math
import functools
import jax
import jax.numpy as jnp
from jax import lax
import numpy as np

D_MODEL = 4096
BATCH = 8
SEQ = 4096
DEPTH = 2

CTX_LEN = 256
GRID_W = 64
EPS = 1e-6

DN_HEADS = 16
DN_HEAD_DIM = 128
DN_WIDTH = DN_HEADS * DN_HEAD_DIM
DN_CONV = 4
DN_CHUNK = 64

LRU_WIDTH = 2048
LRU_BLOCKS = 16
LRU_BLOCK_DIM = LRU_WIDTH // LRU_BLOCKS
LRU_CONV = 4
LRU_C = 8.0

SC_WIDTH = D_MODEL
SC_CONV = 3

OFF_LRU = 3 * DN_WIDTH
OFF_BETA = OFF_LRU + LRU_WIDTH
OFF_ALPHA = OFF_BETA + 2 * DN_HEADS
AB_STATE = OFF_ALPHA + 2 * DN_HEADS
OFF_LRU_GATE = AB_STATE + DN_WIDTH
AB_IN = OFF_LRU_GATE + LRU_WIDTH
AB_OUT = DN_WIDTH + LRU_WIDTH

kernel_name = 'hybrid_deltanet_rglru_shortconv_dit'

F32 = jnp.float32


def _rms_norm(x, w):
    xf = x.astype(F32)
    y = xf * lax.rsqrt(jnp.mean(xf * xf, axis=-1, keepdims=True) + EPS)
    return (y * w.astype(F32)).astype(x.dtype)


def _l2norm(x):
    return x * lax.rsqrt(jnp.sum(x * x, axis=-1, keepdims=True) + EPS)


def _rev(t, axis, on):
    return jnp.flip(t, axis=axis) if on else t


def _dw_conv(x, w, b=None):
    k = w.shape[0]
    left = k // 2
    y = lax.conv_general_dilated(
        x, w[:, None, :].astype(x.dtype), window_strides=(1,), padding=[(left, k - 1 - left)],
        dimension_numbers=('NWC', 'WIO', 'NWC'), feature_group_count=x.shape[-1])
    if b is not None:
        y = y + b.astype(x.dtype)
    return y


def _to_col_major(t, rows):
    bsz, length, ch = t.shape
    return t.reshape(bsz, rows, GRID_W, ch).swapaxes(1, 2).reshape(bsz, length, ch)


def _to_raster(t, rows):
    bsz, length, ch = t.shape
    return t.reshape(bsz, GRID_W, rows, ch).swapaxes(1, 2).reshape(bsz, length, ch)


def _short_conv_heads(p, conv_w):
    bsz, length, width = p.shape
    y = jax.nn.silu(_dw_conv(p, conv_w)).astype(F32)
    return y.reshape(bsz, length, width // DN_WIDTH, DN_HEADS, DN_HEAD_DIM).transpose(2, 0, 3, 1, 4)


def _decay_gates(p_beta, p_alpha, a_log, dt_bias):
    bsz, length, _ = p_beta.shape
    beta = jax.nn.sigmoid(p_beta.astype(F32)).reshape(bsz, length, 2, DN_HEADS)
    alpha = p_alpha.astype(F32).reshape(bsz, length, 2, DN_HEADS)
    g = -jnp.exp(a_log.astype(F32)) * jax.nn.softplus(alpha + dt_bias.astype(F32))
    return beta.transpose(2, 0, 3, 1), g.transpose(2, 0, 3, 1)


def _delta_chunks(k, v, beta, g):
    bsz, nh, length, dk = k.shape
    n = length // DN_CHUNK
    kc = k.reshape(bsz, nh, n, DN_CHUNK, dk)
    vc = v.reshape(bsz, nh, n, DN_CHUNK, v.shape[-1])
    bc = beta.reshape(bsz, nh, n, DN_CHUNK)
    g_cum = jnp.cumsum(g.reshape(bsz, nh, n, DN_CHUNK), axis=-1)
    idx = jnp.arange(DN_CHUNK)
    diff = g_cum[..., :, None] - g_cum[..., None, :]
    decay = jnp.exp(jnp.where(idx[:, None] >= idx[None, :], diff, -jnp.inf))
    kk = jnp.einsum('bhntd,bhnsd->bhnts', kc, kc)
    lower = jnp.where(idx[:, None] > idx[None, :], bc[..., :, None] * kk * decay, 0.0)
    t_mat = lower + jnp.eye(DN_CHUNK, dtype=lower.dtype)
    solve = functools.partial(lax.linalg.triangular_solve, left_side=True, lower=True, unit_diagonal=True)
    w = solve(t_mat, (bc * jnp.exp(g_cum))[..., None] * kc)
    u = solve(t_mat, bc[..., None] * vc)
    k_end = kc * jnp.exp(g_cum[..., -1:] - g_cum)[..., None]
    g_end = jnp.exp(g_cum[..., -1])
    return g_cum, decay, kc, w, u, k_end, g_end


def _delta_step(s, w_c, u_c, ke_c, ge_c):
    u_c = u_c - jnp.einsum('bhtd,bhdv->bhtv', w_c, s)
    s_new = ge_c[..., None, None] * s + jnp.einsum('bhtd,bhtv->bhdv', ke_c, u_c)
    return u_c, s_new


def _delta_final_state(k, v, beta, g, s0):
    _, _, _, w, u, k_end, g_end = _delta_chunks(k, v, beta, g)

    def step(s, xs):
        _, s = _delta_step(s, *xs)
        return s, None

    s, _ = lax.scan(step, s0, tuple(jnp.moveaxis(t, 2, 0) for t in (w, u, k_end, g_end)))
    return s


def _delta_outputs(q, k, v, beta, g, s0):
    g_cum, decay, kc, w, u, k_end, g_end = _delta_chunks(k, v, beta, g)
    bsz, nh, length, dk = q.shape
    qc = q.reshape(bsz, nh, -1, DN_CHUNK, dk)
    a_qk = jnp.einsum('bhntd,bhnsd->bhnts', qc, kc) * decay
    q_g = qc * jnp.exp(g_cum)[..., None]

    def step(s, xs):
        w_c, u_c, ke_c, ge_c, aqk_c, qg_c = xs
        u_c, s_new = _delta_step(s, w_c, u_c, ke_c, ge_c)
        o = jnp.einsum('bhtd,bhdv->bhtv', qg_c, s) + jnp.einsum('bhts,bhsv->bhtv', aqk_c, u_c)
        return s_new, o

    xs = tuple(jnp.moveaxis(t, 2, 0) for t in (w, u, k_end, g_end, a_qk, q_g))
    _, o = lax.scan(step, s0, xs)
    return jnp.moveaxis(o, 0, 2).reshape(bsz, nh, length, -1)


def _rglru_gates(xc, w_r, b_r, w_i, b_i, lam):
    blocks = xc.reshape(xc.shape[0], xc.shape[1], LRU_BLOCKS, LRU_BLOCK_DIM)
    r = jax.nn.sigmoid(jnp.einsum('blnd,nde->blne', blocks, w_r).reshape(xc.shape) + b_r)
    i = jax.nn.sigmoid(jnp.einsum('blnd,nde->blne', blocks, w_i).reshape(xc.shape) + b_i)
    log_a = -LRU_C * r * jax.nn.softplus(-lam)
    b = jnp.sqrt(-jnp.expm1(2.0 * log_a)) * (i * xc)
    return log_a, b


def _linear_scan(a, b, h0):
    b = b.at[:, 0].add(a[:, 0] * h0)

    def combine(prev, nxt):
        return prev[0] * nxt[0], nxt[0] * prev[1] + nxt[1]

    _, h = lax.associative_scan(combine, (a, b), axis=1)
    return h


def _linear_final_state(log_a, b):
    suffix = lax.cumsum(log_a, axis=1, reverse=True) - log_a
    return jnp.sum(jnp.exp(suffix) * b, axis=1)


def _ab_mixer(h, hc, w_in, qkv_conv, a_log, dt_bias, dn_norm, lru_conv_w, lru_conv_b,
              lru_w_r, lru_b_r, lru_w_i, lru_b_i, lru_lambda, w_out):
    bsz, length, _ = h.shape
    rows = length // GRID_W
    sh = DN_WIDTH
    proj = h @ w_in
    proj_c = hc @ w_in[:, sh:AB_STATE]

    qkv = _short_conv_heads(proj[..., :OFF_LRU], qkv_conv)
    q = _l2norm(qkv[0]) * DN_HEAD_DIM ** -0.5
    k = _l2norm(qkv[1])
    v = qkv[2]
    beta, g = _decay_gates(proj[..., OFF_BETA:OFF_ALPHA], proj[..., OFF_ALPHA:AB_STATE], a_log, dt_bias)
    kv_c = _short_conv_heads(proj_c[..., :OFF_LRU - sh], qkv_conv[:, sh:])
    k_c = _l2norm(kv_c[0])
    v_c = kv_c[1]
    beta_c, g_c = _decay_gates(proj_c[..., OFF_BETA - sh:OFF_ALPHA - sh], proj_c[..., OFF_ALPHA - sh:],
                               a_log, dt_bias)
    s0 = jnp.zeros((bsz, DN_HEADS, DN_HEAD_DIM, DN_HEAD_DIM), F32)
    o_dn = jnp.zeros_like(v)
    for d in range(2):
        s_ctx = _delta_final_state(_rev(k_c, 2, d), _rev(v_c, 2, d), _rev(beta_c[d], 2, d),
                                   _rev(g_c[d], 2, d), s0)
        o = _delta_outputs(_rev(q, 2, d), _rev(k, 2, d), _rev(v, 2, d), _rev(beta[d], 2, d),
                           _rev(g[d], 2, d), s_ctx)
        o_dn = o_dn + _rev(o, 2, d)
    o_dn = _rms_norm(o_dn, dn_norm).transpose(0, 2, 1, 3).reshape(bsz, length, DN_WIDTH)

    xc = _dw_conv(_to_col_major(proj[..., OFF_LRU:OFF_BETA], rows), lru_conv_w, lru_conv_b).astype(F32)
    xc_c = _dw_conv(proj_c[..., OFF_LRU - sh:OFF_BETA - sh], lru_conv_w, lru_conv_b).astype(F32)
    h_lru = jnp.zeros_like(xc)
    for d in range(2):
        la_c, b_c = _rglru_gates(_rev(xc_c, 1, d), lru_w_r[d], lru_b_r[d], lru_w_i[d], lru_b_i[d], lru_lambda[d])
        h0 = _linear_final_state(la_c, b_c)
        la, bb = _rglru_gates(_rev(xc, 1, d), lru_w_r[d], lru_b_r[d], lru_w_i[d], lru_b_i[d], lru_lambda[d])
        h_lru = h_lru + _rev(_linear_scan(jnp.exp(la), bb, h0), 1, d)
    h_lru = _to_raster(h_lru, rows)

    y = jnp.concatenate([o_dn * jax.nn.silu(proj[..., AB_STATE:OFF_LRU_GATE]),
                         h_lru * jax.nn.silu(proj[..., OFF_LRU_GATE:])], axis=-1)
    return y @ w_out


def _sc_mixer(h, w_in, conv_w, w_out):
    bsz, length, _ = h.shape
    rows = length // GRID_W
    b_g, c_g, x_in, gate = jnp.split(h @ w_in, 4, axis=-1)
    z = (c_g * x_in).reshape(bsz * rows, GRID_W, SC_WIDTH)
    z = _dw_conv(z, conv_w).reshape(bsz, length, SC_WIDTH)
    return (b_g * z * jax.nn.silu(gate)) @ w_out


def _fwd_setup_inputs(seed: int = 0) -> dict:
    key = jax.random.key(seed)
    ks = iter(jax.random.split(key, 32))
    ne, no = (DEPTH + 1) // 2, DEPTH // 2
    dm = D_MODEL

    def nrm(shape, scale):
        return jax.random.normal(next(ks), shape, F32) * scale

    a_pow = jax.random.uniform(next(ks), (ne, 2, LRU_WIDTH), F32, 0.9, 0.999)
    a_base = a_pow ** (1.0 / LRU_C)
    lru_lambda = jnp.log(a_base) - jnp.log1p(-a_base)
    a_mag = jax.random.uniform(next(ks), (ne, 2, DN_HEADS), F32, 1.0, 16.0)
    dt = jnp.exp(jax.random.uniform(next(ks), (ne, 2, DN_HEADS), F32, math.log(1e-3), math.log(1e-1)))
    dt_bias = dt + jnp.log(-jnp.expm1(-dt))
    return {
        'x': nrm((BATCH, SEQ, dm), 1.0),
        'c': nrm((BATCH, dm), 1.0),
        'ctx': nrm((BATCH, CTX_LEN, dm), 1.0),
        'c_ctx': nrm((dm,), 1.0),
        'mod_w': nrm((DEPTH, dm, 3 * dm), dm ** -0.5),
        'mod_b': nrm((DEPTH, 3 * dm), 0.02),
        'norm_w': 1.0 + nrm((DEPTH, dm), 0.02),
        'ab_w_in': nrm((ne, dm, AB_IN), dm ** -0.5),
        'ab_qkv_conv': nrm((ne, DN_CONV, 3 * DN_WIDTH), DN_CONV ** -0.5),
        'ab_a_log': jnp.log(a_mag),
        'ab_dt_bias': dt_bias,
        'ab_dn_norm': 1.0 + nrm((ne, DN_HEAD_DIM), 0.02),
        'ab_lru_conv_w': nrm((ne, LRU_CONV, LRU_WIDTH), LRU_CONV ** -0.5),
        'ab_lru_conv_b': nrm((ne, LRU_WIDTH), 0.02),
        'ab_lru_w_r': nrm((ne, 2, LRU_BLOCKS, LRU_BLOCK_DIM, LRU_BLOCK_DIM), LRU_BLOCK_DIM ** -0.5),
        'ab_lru_b_r': nrm((ne, 2, LRU_WIDTH), 0.02),
        'ab_lru_w_i': nrm((ne, 2, LRU_BLOCKS, LRU_BLOCK_DIM, LRU_BLOCK_DIM), LRU_BLOCK_DIM ** -0.5),
        'ab_lru_b_i': nrm((ne, 2, LRU_WIDTH), 0.02),
        'ab_lru_lambda': lru_lambda,
        'ab_w_out': nrm((ne, AB_OUT, dm), AB_OUT ** -0.5),
        'sc_w_in': nrm((no, dm, 4 * SC_WIDTH), dm ** -0.5),
        'sc_conv': nrm((no, SC_CONV, SC_WIDTH), SC_CONV ** -0.5),
        'sc_w_out': nrm((no, SC_WIDTH, dm), SC_WIDTH ** -0.5),
        'final_norm_w': 1.0 + nrm((dm,), 0.02),
    }


def _fwd_reference(x, c, ctx, c_ctx, mod_w, mod_b, norm_w, ab_w_in, ab_qkv_conv, ab_a_log, ab_dt_bias,
              ab_dn_norm, ab_lru_conv_w, ab_lru_conv_b, ab_lru_w_r, ab_lru_b_r, ab_lru_w_i, ab_lru_b_i,
              ab_lru_lambda, ab_w_out, sc_w_in, sc_conv, sc_w_out, final_norm_w):
    dm = D_MODEL
    silu_c = jax.nn.silu(c)
    silu_cc = jax.nn.silu(c_ctx)
    for layer in range(DEPTH):
        j = layer // 2
        shift, scale, gate = jnp.split(silu_c @ mod_w[layer] + mod_b[layer], 3, axis=-1)
        hn = _rms_norm(x, norm_w[layer]) * (1.0 + scale[:, None, :]) + shift[:, None, :]
        if layer % 2 == 0:
            shift_c, scale_c = jnp.split(silu_cc @ mod_w[layer][:, :2 * dm] + mod_b[layer][:2 * dm], 2)
            hc = _rms_norm(ctx, norm_w[layer]) * (1.0 + scale_c) + shift_c
            y = _ab_mixer(hn, hc, ab_w_in[j], ab_qkv_conv[j], ab_a_log[j], ab_dt_bias[j], ab_dn_norm[j],
                          ab_lru_conv_w[j], ab_lru_conv_b[j], ab_lru_w_r[j], ab_lru_b_r[j], ab_lru_w_i[j],
                          ab_lru_b_i[j], ab_lru_lambda[j], ab_w_out[j])
        else:
            y = _sc_mixer(hn, sc_w_in[j], sc_conv[j], sc_w_out[j])
        x = x + (gate[:, None, :] * y).astype(x.dtype)
    return _rms_norm(x, final_norm_w)


import jax as _jax
import jax.numpy as _jnp

TWIN_FORMAT = 'train_step'
FWD_PARAMS = ['x', 'c', 'ctx', 'c_ctx', 'mod_w', 'mod_b', 'norm_w', 'ab_w_in', 'ab_qkv_conv', 'ab_a_log', 'ab_dt_bias', 'ab_dn_norm', 'ab_lru_conv_w', 'ab_lru_conv_b', 'ab_lru_w_r', 'ab_lru_b_r', 'ab_lru_w_i', 'ab_lru_b_i', 'ab_lru_lambda', 'ab_w_out', 'sc_w_in', 'sc_conv', 'sc_w_out', 'final_norm_w']
TWIN_WEIGHTS = ['c_ctx', 'mod_w', 'mod_b', 'norm_w', 'ab_w_in', 'ab_qkv_conv', 'ab_a_log', 'ab_dt_bias', 'ab_dn_norm', 'ab_lru_conv_w', 'ab_lru_conv_b', 'ab_lru_w_r', 'ab_lru_b_r', 'ab_lru_w_i', 'ab_lru_b_i', 'ab_lru_lambda', 'ab_w_out', 'sc_w_in', 'sc_conv', 'sc_w_out', 'final_norm_w']
TWIN_DIFF_INPUT = 'x'
TWIN_INPUTS = ['x', 'c', 'ctx', 'c_ctx', 'mod_w', 'mod_b', 'norm_w', 'ab_w_in', 'ab_qkv_conv', 'ab_a_log', 'ab_dt_bias', 'ab_dn_norm', 'ab_lru_conv_w', 'ab_lru_conv_b', 'ab_lru_w_r', 'ab_lru_b_r', 'ab_lru_w_i', 'ab_lru_b_i', 'ab_lru_lambda', 'ab_w_out', 'sc_w_in', 'sc_conv', 'sc_w_out', 'final_norm_w', 'loss_target', 'm_c_ctx', 'm_mod_w', 'm_mod_b', 'm_norm_w', 'm_ab_w_in', 'm_ab_qkv_conv', 'm_ab_a_log', 'm_ab_dt_bias', 'm_ab_dn_norm', 'm_ab_lru_conv_w', 'm_ab_lru_conv_b', 'm_ab_lru_w_r', 'm_ab_lru_b_r', 'm_ab_lru_w_i', 'm_ab_lru_b_i', 'm_ab_lru_lambda', 'm_ab_w_out', 'm_sc_w_in', 'm_sc_conv', 'm_sc_w_out', 'm_final_norm_w', 'v_c_ctx', 'v_mod_w', 'v_mod_b', 'v_norm_w', 'v_ab_w_in', 'v_ab_qkv_conv', 'v_ab_a_log', 'v_ab_dt_bias', 'v_ab_dn_norm', 'v_ab_lru_conv_w', 'v_ab_lru_conv_b', 'v_ab_lru_w_r', 'v_ab_lru_b_r', 'v_ab_lru_w_i', 'v_ab_lru_b_i', 'v_ab_lru_lambda', 'v_ab_w_out', 'v_sc_w_in', 'v_sc_conv', 'v_sc_w_out', 'v_final_norm_w']
TWIN_OUTPUTS = ['loss', 'grad_x', 'grad_c_ctx', 'grad_mod_w', 'grad_mod_b', 'grad_norm_w', 'grad_ab_w_in', 'grad_ab_qkv_conv', 'grad_ab_a_log', 'grad_ab_dt_bias', 'grad_ab_dn_norm', 'grad_ab_lru_conv_w', 'grad_ab_lru_conv_b', 'grad_ab_lru_w_r', 'grad_ab_lru_b_r', 'grad_ab_lru_w_i', 'grad_ab_lru_b_i', 'grad_ab_lru_lambda', 'grad_ab_w_out', 'grad_sc_w_in', 'grad_sc_conv', 'grad_sc_w_out', 'grad_final_norm_w', 'delta_c_ctx', 'delta_mod_w', 'delta_mod_b', 'delta_norm_w', 'delta_ab_w_in', 'delta_ab_qkv_conv', 'delta_ab_a_log', 'delta_ab_dt_bias', 'delta_ab_dn_norm', 'delta_ab_lru_conv_w', 'delta_ab_lru_conv_b', 'delta_ab_lru_w_r', 'delta_ab_lru_b_r', 'delta_ab_lru_w_i', 'delta_ab_lru_b_i', 'delta_ab_lru_lambda', 'delta_ab_w_out', 'delta_sc_w_in', 'delta_sc_conv', 'delta_sc_w_out', 'delta_final_norm_w', 'new_m_c_ctx', 'new_m_mod_w', 'new_m_mod_b', 'new_m_norm_w', 'new_m_ab_w_in', 'new_m_ab_qkv_conv', 'new_m_ab_a_log', 'new_m_ab_dt_bias', 'new_m_ab_dn_norm', 'new_m_ab_lru_conv_w', 'new_m_ab_lru_conv_b', 'new_m_ab_lru_w_r', 'new_m_ab_lru_b_r', 'new_m_ab_lru_w_i', 'new_m_ab_lru_b_i', 'new_m_ab_lru_lambda', 'new_m_ab_w_out', 'new_m_sc_w_in', 'new_m_sc_conv', 'new_m_sc_w_out', 'new_m_final_norm_w', 'new_v_c_ctx', 'new_v_mod_w', 'new_v_mod_b', 'new_v_norm_w', 'new_v_ab_w_in', 'new_v_ab_qkv_conv', 'new_v_ab_a_log', 'new_v_ab_dt_bias', 'new_v_ab_dn_norm', 'new_v_ab_lru_conv_w', 'new_v_ab_lru_conv_b', 'new_v_ab_lru_w_r', 'new_v_ab_lru_b_r', 'new_v_ab_lru_w_i', 'new_v_ab_lru_b_i', 'new_v_ab_lru_lambda', 'new_v_ab_w_out', 'new_v_sc_w_in', 'new_v_sc_conv', 'new_v_sc_w_out', 'new_v_final_norm_w']
TWIN_LEAF_KINDS = {'loss': 'loss', 'grad_x': 'grad_x', 'grad_c_ctx': 'grad_w', 'grad_mod_w': 'grad_w', 'grad_mod_b': 'grad_w', 'grad_norm_w': 'grad_w', 'grad_ab_w_in': 'grad_w', 'grad_ab_qkv_conv': 'grad_w', 'grad_ab_a_log': 'grad_w', 'grad_ab_dt_bias': 'grad_w', 'grad_ab_dn_norm': 'grad_w', 'grad_ab_lru_conv_w': 'grad_w', 'grad_ab_lru_conv_b': 'grad_w', 'grad_ab_lru_w_r': 'grad_w', 'grad_ab_lru_b_r': 'grad_w', 'grad_ab_lru_w_i': 'grad_w', 'grad_ab_lru_b_i': 'grad_w', 'grad_ab_lru_lambda': 'grad_w', 'grad_ab_w_out': 'grad_w', 'grad_sc_w_in': 'grad_w', 'grad_sc_conv': 'grad_w', 'grad_sc_w_out': 'grad_w', 'grad_final_norm_w': 'grad_w', 'delta_c_ctx': 'delta_w', 'delta_mod_w': 'delta_w', 'delta_mod_b': 'delta_w', 'delta_norm_w': 'delta_w', 'delta_ab_w_in': 'delta_w', 'delta_ab_qkv_conv': 'delta_w', 'delta_ab_a_log': 'delta_w', 'delta_ab_dt_bias': 'delta_w', 'delta_ab_dn_norm': 'delta_w', 'delta_ab_lru_conv_w': 'delta_w', 'delta_ab_lru_conv_b': 'delta_w', 'delta_ab_lru_w_r': 'delta_w', 'delta_ab_lru_b_r': 'delta_w', 'delta_ab_lru_w_i': 'delta_w', 'delta_ab_lru_b_i': 'delta_w', 'delta_ab_lru_lambda': 'delta_w', 'delta_ab_w_out': 'delta_w', 'delta_sc_w_in': 'delta_w', 'delta_sc_conv': 'delta_w', 'delta_sc_w_out': 'delta_w', 'delta_final_norm_w': 'delta_w', 'new_m_c_ctx': 'new_m', 'new_m_mod_w': 'new_m', 'new_m_mod_b': 'new_m', 'new_m_norm_w': 'new_m', 'new_m_ab_w_in': 'new_m', 'new_m_ab_qkv_conv': 'new_m', 'new_m_ab_a_log': 'new_m', 'new_m_ab_dt_bias': 'new_m', 'new_m_ab_dn_norm': 'new_m', 'new_m_ab_lru_conv_w': 'new_m', 'new_m_ab_lru_conv_b': 'new_m', 'new_m_ab_lru_w_r': 'new_m', 'new_m_ab_lru_b_r': 'new_m', 'new_m_ab_lru_w_i': 'new_m', 'new_m_ab_lru_b_i': 'new_m', 'new_m_ab_lru_lambda': 'new_m', 'new_m_ab_w_out': 'new_m', 'new_m_sc_w_in': 'new_m', 'new_m_sc_conv': 'new_m', 'new_m_sc_w_out': 'new_m', 'new_m_final_norm_w': 'new_m', 'new_v_c_ctx': 'new_v', 'new_v_mod_w': 'new_v', 'new_v_mod_b': 'new_v', 'new_v_norm_w': 'new_v', 'new_v_ab_w_in': 'new_v', 'new_v_ab_qkv_conv': 'new_v', 'new_v_ab_a_log': 'new_v', 'new_v_ab_dt_bias': 'new_v', 'new_v_ab_dn_norm': 'new_v', 'new_v_ab_lru_conv_w': 'new_v', 'new_v_ab_lru_conv_b': 'new_v', 'new_v_ab_lru_w_r': 'new_v', 'new_v_ab_lru_b_r': 'new_v', 'new_v_ab_lru_w_i': 'new_v', 'new_v_ab_lru_b_i': 'new_v', 'new_v_ab_lru_lambda': 'new_v', 'new_v_ab_w_out': 'new_v', 'new_v_sc_w_in': 'new_v', 'new_v_sc_conv': 'new_v', 'new_v_sc_w_out': 'new_v', 'new_v_final_norm_w': 'new_v'}


def _forward(args):
    return _fwd_reference(*[args[k] for k in FWD_PARAMS])


def _output_shape():
    out = _jax.eval_shape(lambda: _forward(_fwd_setup_inputs(0)))
    return out.shape, out.dtype

N_MICROBATCH = 1
ADAM_LR = 0.001
ADAM_B1 = 0.9
ADAM_B2 = 0.999
ADAM_EPS = 1e-08
ADAM_WD = 0.01
ADAM_STEP = 10
PER_EXAMPLE_BATCH_AXIS = {'x': 0, 'c': 0, 'ctx': 0, 'loss_target': 0}
SHARED_INPUTS = []
_WEIGHT_DTYPES = {'c_ctx': _jnp.float32, 'mod_w': _jnp.float32, 'mod_b': _jnp.float32, 'norm_w': _jnp.float32, 'ab_w_in': _jnp.float32, 'ab_qkv_conv': _jnp.float32, 'ab_a_log': _jnp.float32, 'ab_dt_bias': _jnp.float32, 'ab_dn_norm': _jnp.float32, 'ab_lru_conv_w': _jnp.float32, 'ab_lru_conv_b': _jnp.float32, 'ab_lru_w_r': _jnp.float32, 'ab_lru_b_r': _jnp.float32, 'ab_lru_w_i': _jnp.float32, 'ab_lru_b_i': _jnp.float32, 'ab_lru_lambda': _jnp.float32, 'ab_w_out': _jnp.float32, 'sc_w_in': _jnp.float32, 'sc_conv': _jnp.float32, 'sc_w_out': _jnp.float32, 'final_norm_w': _jnp.float32}
MOMENT_SCALE = {'c_ctx': 2.128056e-02, 'mod_w': 4.702388e-02, 'mod_b': 8.084272e-02, 'norm_w': 4.102604e-02, 'ab_w_in': 4.403351e-02, 'ab_qkv_conv': 9.379001e-03, 'ab_a_log': 2.975703e-02, 'ab_dt_bias': 2.926313e-02, 'ab_dn_norm': 4.772559e-02, 'ab_lru_conv_w': 8.589727e-02, 'ab_lru_conv_b': 1.338726e-01, 'ab_lru_w_r': 4.999041e-03, 'ab_lru_b_r': 6.329987e-03, 'ab_lru_w_i': 1.140870e-02, 'ab_lru_b_i': 1.733481e-02, 'ab_lru_lambda': 1.755257e-02, 'ab_w_out': 5.907321e-02, 'sc_w_in': 2.092714e-02, 'sc_conv': 2.065784e-02, 'sc_w_out': 2.063308e-02, 'final_norm_w': 8.997592e+00}


def _to_microbatches(a, axis):
    t = _jnp.moveaxis(a, axis, 0)
    t = t.reshape((N_MICROBATCH, t.shape[0] // N_MICROBATCH) + t.shape[1:])
    return _jnp.moveaxis(t, 1, axis + 1)


def setup_inputs(seed: int = 0) -> dict:
    inp = _fwd_setup_inputs(seed)
    key = _jax.random.fold_in(_jax.random.key(seed), 7919)
    shape, _ = _output_shape()
    out = dict(inp)
    out["loss_target"] = _jax.random.normal(_jax.random.fold_in(key, 0), shape, _jnp.float32)
    for i, name in enumerate(TWIN_WEIGHTS):
        w = inp[name].astype(_jnp.float32)
        if MOMENT_SCALE is None:
            s = _jnp.sqrt(_jnp.mean(_jnp.square(w)) + 1e-30)
        else:
            s = MOMENT_SCALE[name]
        km, kv = _jax.random.split(_jax.random.fold_in(key, i + 1))
        out[name] = w
        out["m_" + name] = s * _jax.random.normal(km, w.shape, _jnp.float32)
        out["v_" + name] = (s * s) * _jax.random.uniform(kv, w.shape, _jnp.float32, 0.5, 1.5)
    if N_MICROBATCH > 1:
        for name, axis in PER_EXAMPLE_BATCH_AXIS.items():
            out[name] = _to_microbatches(out[name], axis)
    return {'x': out['x'], 'c': out['c'], 'ctx': out['ctx'], 'c_ctx': out['c_ctx'], 'mod_w': out['mod_w'], 'mod_b': out['mod_b'], 'norm_w': out['norm_w'], 'ab_w_in': out['ab_w_in'], 'ab_qkv_conv': out['ab_qkv_conv'], 'ab_a_log': out['ab_a_log'], 'ab_dt_bias': out['ab_dt_bias'], 'ab_dn_norm': out['ab_dn_norm'], 'ab_lru_conv_w': out['ab_lru_conv_w'], 'ab_lru_conv_b': out['ab_lru_conv_b'], 'ab_lru_w_r': out['ab_lru_w_r'], 'ab_lru_b_r': out['ab_lru_b_r'], 'ab_lru_w_i': out['ab_lru_w_i'], 'ab_lru_b_i': out['ab_lru_b_i'], 'ab_lru_lambda': out['ab_lru_lambda'], 'ab_w_out': out['ab_w_out'], 'sc_w_in': out['sc_w_in'], 'sc_conv': out['sc_conv'], 'sc_w_out': out['sc_w_out'], 'final_norm_w': out['final_norm_w'], 'loss_target': out['loss_target'], 'm_c_ctx': out['m_c_ctx'], 'm_mod_w': out['m_mod_w'], 'm_mod_b': out['m_mod_b'], 'm_norm_w': out['m_norm_w'], 'm_ab_w_in': out['m_ab_w_in'], 'm_ab_qkv_conv': out['m_ab_qkv_conv'], 'm_ab_a_log': out['m_ab_a_log'], 'm_ab_dt_bias': out['m_ab_dt_bias'], 'm_ab_dn_norm': out['m_ab_dn_norm'], 'm_ab_lru_conv_w': out['m_ab_lru_conv_w'], 'm_ab_lru_conv_b': out['m_ab_lru_conv_b'], 'm_ab_lru_w_r': out['m_ab_lru_w_r'], 'm_ab_lru_b_r': out['m_ab_lru_b_r'], 'm_ab_lru_w_i': out['m_ab_lru_w_i'], 'm_ab_lru_b_i': out['m_ab_lru_b_i'], 'm_ab_lru_lambda': out['m_ab_lru_lambda'], 'm_ab_w_out': out['m_ab_w_out'], 'm_sc_w_in': out['m_sc_w_in'], 'm_sc_conv': out['m_sc_conv'], 'm_sc_w_out': out['m_sc_w_out'], 'm_final_norm_w': out['m_final_norm_w'], 'v_c_ctx': out['v_c_ctx'], 'v_mod_w': out['v_mod_w'], 'v_mod_b': out['v_mod_b'], 'v_norm_w': out['v_norm_w'], 'v_ab_w_in': out['v_ab_w_in'], 'v_ab_qkv_conv': out['v_ab_qkv_conv'], 'v_ab_a_log': out['v_ab_a_log'], 'v_ab_dt_bias': out['v_ab_dt_bias'], 'v_ab_dn_norm': out['v_ab_dn_norm'], 'v_ab_lru_conv_w': out['v_ab_lru_conv_w'], 'v_ab_lru_conv_b': out['v_ab_lru_conv_b'], 'v_ab_lru_w_r': out['v_ab_lru_w_r'], 'v_ab_lru_b_r': out['v_ab_lru_b_r'], 'v_ab_lru_w_i': out['v_ab_lru_w_i'], 'v_ab_lru_b_i': out['v_ab_lru_b_i'], 'v_ab_lru_lambda': out['v_ab_lru_lambda'], 'v_ab_w_out': out['v_ab_w_out'], 'v_sc_w_in': out['v_sc_w_in'], 'v_sc_conv': out['v_sc_conv'], 'v_sc_w_out': out['v_sc_w_out'], 'v_final_norm_w': out['v_final_norm_w']}


def _loss(weights, diff, rest, loss_target):
    with _jax.named_scope("forward"):
        args = {**rest, TWIN_DIFF_INPUT: diff, **{k: w.astype(_WEIGHT_DTYPES[k]) for k, w in weights.items()}}
        y = _forward(args)
    with _jax.named_scope("loss_head"):
        err = _jnp.square(y.astype(_jnp.float32) - loss_target)
        return 0.5 * _jnp.sum(_jnp.mean(err, axis=-1)) if err.ndim else 0.5 * err


def _adamw(w, g, m, v):
    m = ADAM_B1 * m + (1.0 - ADAM_B1) * g
    v = ADAM_B2 * v + (1.0 - ADAM_B2) * _jnp.square(g)
    m_hat = m / (1.0 - ADAM_B1 ** ADAM_STEP)
    v_hat = v / (1.0 - ADAM_B2 ** ADAM_STEP)
    delta = -ADAM_LR * (m_hat / (_jnp.sqrt(v_hat) + ADAM_EPS) + ADAM_WD * w)
    return delta, m, v


def reference(x, c, ctx, c_ctx, mod_w, mod_b, norm_w, ab_w_in, ab_qkv_conv, ab_a_log, ab_dt_bias, ab_dn_norm, ab_lru_conv_w, ab_lru_conv_b, ab_lru_w_r, ab_lru_b_r, ab_lru_w_i, ab_lru_b_i, ab_lru_lambda, ab_w_out, sc_w_in, sc_conv, sc_w_out, final_norm_w, loss_target, m_c_ctx, m_mod_w, m_mod_b, m_norm_w, m_ab_w_in, m_ab_qkv_conv, m_ab_a_log, m_ab_dt_bias, m_ab_dn_norm, m_ab_lru_conv_w, m_ab_lru_conv_b, m_ab_lru_w_r, m_ab_lru_b_r, m_ab_lru_w_i, m_ab_lru_b_i, m_ab_lru_lambda, m_ab_w_out, m_sc_w_in, m_sc_conv, m_sc_w_out, m_final_norm_w, v_c_ctx, v_mod_w, v_mod_b, v_norm_w, v_ab_w_in, v_ab_qkv_conv, v_ab_a_log, v_ab_dt_bias, v_ab_dn_norm, v_ab_lru_conv_w, v_ab_lru_conv_b, v_ab_lru_w_r, v_ab_lru_b_r, v_ab_lru_w_i, v_ab_lru_b_i, v_ab_lru_lambda, v_ab_w_out, v_sc_w_in, v_sc_conv, v_sc_w_out, v_final_norm_w):
    given = dict(x=x, c=c, ctx=ctx, c_ctx=c_ctx, mod_w=mod_w, mod_b=mod_b, norm_w=norm_w, ab_w_in=ab_w_in, ab_qkv_conv=ab_qkv_conv, ab_a_log=ab_a_log, ab_dt_bias=ab_dt_bias, ab_dn_norm=ab_dn_norm, ab_lru_conv_w=ab_lru_conv_w, ab_lru_conv_b=ab_lru_conv_b, ab_lru_w_r=ab_lru_w_r, ab_lru_b_r=ab_lru_b_r, ab_lru_w_i=ab_lru_w_i, ab_lru_b_i=ab_lru_b_i, ab_lru_lambda=ab_lru_lambda, ab_w_out=ab_w_out, sc_w_in=sc_w_in, sc_conv=sc_conv, sc_w_out=sc_w_out, final_norm_w=final_norm_w, loss_target=loss_target, m_c_ctx=m_c_ctx, m_mod_w=m_mod_w, m_mod_b=m_mod_b, m_norm_w=m_norm_w, m_ab_w_in=m_ab_w_in, m_ab_qkv_conv=m_ab_qkv_conv, m_ab_a_log=m_ab_a_log, m_ab_dt_bias=m_ab_dt_bias, m_ab_dn_norm=m_ab_dn_norm, m_ab_lru_conv_w=m_ab_lru_conv_w, m_ab_lru_conv_b=m_ab_lru_conv_b, m_ab_lru_w_r=m_ab_lru_w_r, m_ab_lru_b_r=m_ab_lru_b_r, m_ab_lru_w_i=m_ab_lru_w_i, m_ab_lru_b_i=m_ab_lru_b_i, m_ab_lru_lambda=m_ab_lru_lambda, m_ab_w_out=m_ab_w_out, m_sc_w_in=m_sc_w_in, m_sc_conv=m_sc_conv, m_sc_w_out=m_sc_w_out, m_final_norm_w=m_final_norm_w, v_c_ctx=v_c_ctx, v_mod_w=v_mod_w, v_mod_b=v_mod_b, v_norm_w=v_norm_w, v_ab_w_in=v_ab_w_in, v_ab_qkv_conv=v_ab_qkv_conv, v_ab_a_log=v_ab_a_log, v_ab_dt_bias=v_ab_dt_bias, v_ab_dn_norm=v_ab_dn_norm, v_ab_lru_conv_w=v_ab_lru_conv_w, v_ab_lru_conv_b=v_ab_lru_conv_b, v_ab_lru_w_r=v_ab_lru_w_r, v_ab_lru_b_r=v_ab_lru_b_r, v_ab_lru_w_i=v_ab_lru_w_i, v_ab_lru_b_i=v_ab_lru_b_i, v_ab_lru_lambda=v_ab_lru_lambda, v_ab_w_out=v_ab_w_out, v_sc_w_in=v_sc_w_in, v_sc_conv=v_sc_conv, v_sc_w_out=v_sc_w_out, v_final_norm_w=v_final_norm_w)
    weights = {n: given[n] for n in TWIN_WEIGHTS}
    shared = {n: given[n] for n in SHARED_INPUTS}
    per_example = {n: given[n] for n in ['x', 'c', 'ctx']}
    grad_fn = _jax.value_and_grad(_loss, argnums=(0, 1))

    def one_microbatch(ex, loss_target):
        ex = dict(ex)
        diff = ex.pop(TWIN_DIFF_INPUT)
        return grad_fn(weights, diff, {**shared, **ex}, loss_target)

    if N_MICROBATCH == 1:
        loss, (grad_w, grad_x) = one_microbatch(per_example, given["loss_target"])
    else:
        def body(carry, xs):
            loss_sum, grad_sum = carry
            l_k, (gw_k, gx_k) = one_microbatch(xs[0], xs[1])
            with _jax.named_scope("update"):
                return (loss_sum + l_k, _jax.tree.map(_jnp.add, grad_sum, gw_k)), gx_k

        init = (_jnp.zeros((), _jnp.float32), _jax.tree.map(_jnp.zeros_like, weights))
        (loss, grad_w), grad_x = _jax.lax.scan(body, init, (per_example, given["loss_target"]))
    with _jax.named_scope("update"):
        delta_w, new_m, new_v = {}, {}, {}
        for n in TWIN_WEIGHTS:
            delta_w[n], new_m[n], new_v[n] = _adamw(weights[n], grad_w[n], given["m_" + n], given["v_" + n])
    return (loss, grad_x, *[grad_w[n] for n in TWIN_WEIGHTS], *[delta_w[n] for n in TWIN_WEIGHTS],
            *[new_m[n] for n in TWIN_WEIGHTS], *[new_v[n] for n in TWIN_WEIGHTS])
```

```python
import functools

import jax
import jax.numpy as jnp
from jax import lax
from jax.experimental import pallas as pl
from jax.experimental.pallas import tpu as pltpu

F32 = jnp.float32
BF16 = jnp.bfloat16
MXU_DTYPE = BF16
ACT_DTYPE = BF16
WIRE_DTYPE = BF16

EPS = 1e-6
GRID_W = 64
CHUNK = 64
DN_CONV_OFFSETS = (-2, -1, 0, 1)
SC_CONV_OFFSETS = (-1, 0, 1)
LRU_C = 8.0
ADAM_LR, ADAM_B1, ADAM_B2, ADAM_EPS, ADAM_WD, ADAM_STEP = 0.001, 0.9, 0.999, 1e-08, 0.01, 10

LANES = 128
SUBLANES = 8
N_CHIPS, N_CORES = 4, 2
N_DEV = N_CHIPS * N_CORES
INV_PRECISION = lax.Precision.HIGH
INV_BLOCK = 16

_MESH = pl.DeviceIdType.MESH
_ANY = pl.BlockSpec(memory_space=pl.ANY)
_NN = (((1,), (0,)), ((), ()))
_NT = (((1,), (1,)), ((), ()))
_TN = (((0,), (0,)), ((), ()))


def _tile(n, cap, mult):
    best = None
    for t in range(mult, min(n, cap) + 1, mult):
        if n % t == 0:
            best = t
    return n if best is None else best


def _iota(shape, dim):
    return lax.broadcasted_iota(jnp.int32, shape, dim)


def _dot(a, b, dims):
    return lax.dot_general(a.astype(MXU_DTYPE), b.astype(MXU_DTYPE), dims, preferred_element_type=F32)


def _silu(x):
    return x * jax.nn.sigmoid(x)


def _dsilu(x):
    s = jax.nn.sigmoid(x)
    return s * (1.0 + x * (1.0 - s))


V7X_VMEM_BYTES = 64 * 1024 * 1024
BIG_KERNEL_VMEM = V7X_VMEM_BYTES * 15 // 16


def _params(*sem, vmem=None):
    return pltpu.CompilerParams(dimension_semantics=sem, vmem_limit_bytes=vmem)


def _place():
    return lax.axis_index("x"), lax.axis_index("y"), lax.axis_index("c")


def all_gather_devices(block, name):
    def body(x_ref, out_ref, send_sems, recv_sems, local_sem):
        x, y, c = _place()
        me, sibling = (x, y, c), (x, y, 1 - c)
        chips = [(1 - x, y), (x, 1 - y), (1 - x, 1 - y)]

        def slot(px, py, pc):
            return out_ref.at[4 * px + 2 * py + pc]

        def copy(k, block_of, to, src=None):
            return pltpu.make_async_remote_copy(
                src_ref=slot(*block_of) if src is None else src, dst_ref=slot(*block_of),
                send_sem=send_sems.at[k], recv_sem=recv_sems.at[k], device_id=to, device_id_type=_MESH)

        mine = pltpu.make_async_copy(x_ref, slot(*me), local_sem)
        mine.start()
        first = [copy(0, me, sibling, src=x_ref)]
        first += [copy(1 + j, me, (*chip, c), src=x_ref) for j, chip in enumerate(chips)]
        for cp in first:
            cp.start()
        passed = [copy(4 + j, (*chip, c), sibling) for j, chip in enumerate(chips)]
        for j, chip in enumerate(chips):
            copy(1 + j, (*chip, c), me).wait_recv()
            passed[j].start()
        copy(0, sibling, me).wait_recv()
        for j, chip in enumerate(chips):
            copy(4 + j, (*chip, 1 - c), me).wait_recv()
        for cp in first + passed:
            cp.wait_send()
        mine.wait()

    return pl.pallas_call(
        body, name=name,
        out_shape=jax.ShapeDtypeStruct((N_DEV,) + block.shape, block.dtype),
        in_specs=[_ANY], out_specs=_ANY,
        scratch_shapes=[pltpu.SemaphoreType.DMA((7,)), pltpu.SemaphoreType.DMA((7,)), pltpu.SemaphoreType.DMA],
    )(block)


def chip_exchange(srcs, *, gather, name):
    n = len(srcs)

    def body(*refs):
        src, out = refs[:n], refs[n:2 * n]
        send_sems, recv_sems, local_sems = refs[2 * n:]
        x, y, c = _place()
        my = 2 * x + y
        peers = [(1 - x, y), (x, 1 - y), (1 - x, 1 - y)]
        copies = []
        for k in range(n):
            own = src[k].at[c] if gather else src[k].at[my]
            loc = pltpu.make_async_copy(own, out[k].at[my], local_sems.at[k])
            loc.start()
            copies.append(loc)
            for j, (px, py) in enumerate(peers):
                cp = pltpu.make_async_remote_copy(
                    src_ref=own if gather else src[k].at[2 * px + py], dst_ref=out[k].at[my],
                    send_sem=send_sems.at[k, j], recv_sem=recv_sems.at[k, j],
                    device_id=(px, py, c), device_id_type=_MESH)
                cp.start()
                copies.append(cp)
        for cp in copies:
            cp.wait()

    return pl.pallas_call(
        body, name=name,
        out_shape=[jax.ShapeDtypeStruct((N_CHIPS,) + s.shape[1:], s.dtype) for s in srcs],
        in_specs=[_ANY] * n, out_specs=[_ANY] * n,
        scratch_shapes=[pltpu.SemaphoreType.DMA((n, 3)), pltpu.SemaphoreType.DMA((n, 3)),
                        pltpu.SemaphoreType.DMA((n,))],
    )(*srcs)


def sibling_exchange(srcs, *, pick, name):
    n = len(srcs)

    def body(*refs):
        src, out = refs[:n], refs[n:2 * n]
        send_sems, recv_sems, local_sems = refs[2 * n:]
        x, y, c = _place()
        copies = []
        for k in range(n):
            a = src[k].shape[0]
            if pick:
                s_ref, d_ref = src[k].at[pl.ds(0, a), 1 - c], out[k]
            else:
                s_ref, d_ref = src[k], out[k].at[pl.ds(0, a), c]
                loc = pltpu.make_async_copy(s_ref, d_ref, local_sems.at[k])
                loc.start()
                copies.append(loc)
            cp = pltpu.make_async_remote_copy(
                src_ref=s_ref, dst_ref=d_ref, send_sem=send_sems.at[k], recv_sem=recv_sems.at[k],
                device_id=(x, y, 1 - c), device_id_type=_MESH)
            cp.start()
            copies.append(cp)
        for cp in copies:
            cp.wait()

    if pick:
        outs = [jax.ShapeDtypeStruct(s.shape[:1] + s.shape[2:], s.dtype) for s in srcs]
    else:
        outs = [jax.ShapeDtypeStruct(s.shape[:1] + (N_CORES,) + s.shape[1:], s.dtype) for s in srcs]
    return pl.pallas_call(
        body, name=name, out_shape=outs, in_specs=[_ANY] * n, out_specs=[_ANY] * n,
        scratch_shapes=[pltpu.SemaphoreType.DMA((n,)), pltpu.SemaphoreType.DMA((n,)),
                        pltpu.SemaphoreType.DMA((n,))],
    )(*srcs)


def matmul(a, b, *, m, n, k, tm, tn, tk, a_spec, b_spec, dims, out_dtype, name,
           out_spec=None, out_shape=None, resid=None, gate=None, aux_dtype=None):
    nk = k // tk
    o_spec = out_spec or pl.BlockSpec((tm, tn), lambda i, j, kk: (i, j))
    o_shape = out_shape or (m, n)

    def body(*refs):
        a_ref, b_ref = refs[0], refs[1]
        pos = 2
        r_ref = g_ref = aux_ref = None
        if resid is not None:
            r_ref, pos = refs[pos], pos + 1
        if gate is not None:
            g_ref, pos = refs[pos], pos + 1
        o_ref, pos = refs[pos], pos + 1
        if aux_dtype is not None:
            aux_ref, pos = refs[pos], pos + 1
        acc = refs[pos]
        kk = pl.program_id(2)

        @pl.when(kk == 0)
        def _():
            acc[...] = jnp.zeros_like(acc)

        acc[...] += _dot(a_ref[...], b_ref[...], dims)

        @pl.when(kk == nk - 1)
        def _():
            y = acc[...]
            if aux_ref is not None:
                aux_ref[...] = y.astype(aux_dtype)
            if g_ref is not None:
                y = y * g_ref[...]
            if r_ref is not None:
                y = y + r_ref[...]
            o_ref[...] = y.astype(out_dtype)

    ins, in_specs = [a, b], [a_spec, b_spec]
    if resid is not None:
        ins.append(resid)
        in_specs.append(pl.BlockSpec((tm, tn), lambda i, j, kk: (i, j)))
    if gate is not None:
        ins.append(gate)
        in_specs.append(pl.BlockSpec((1, tn), lambda i, j, kk: (0, j)))
    outs, out_specs = [jax.ShapeDtypeStruct(o_shape, out_dtype)], [o_spec]
    if aux_dtype is not None:
        outs.append(jax.ShapeDtypeStruct((m, n), aux_dtype))
        out_specs.append(pl.BlockSpec((tm, tn), lambda i, j, kk: (i, j)))
    res = pl.pallas_call(
        body, name=name, grid=(m // tm, n // tn, nk), in_specs=in_specs, out_specs=out_specs,
        out_shape=outs, scratch_shapes=[pltpu.VMEM((tm, tn), F32)],
        compiler_params=_params("parallel", "parallel", "arbitrary"),
    )(*ins)
    return res if aux_dtype is not None else res[0]


def mm_nn(a, b, *, out_dtype, name, tm_cap=1088, tn_cap=1024, tk_cap=512, **kw):
    m, k = a.shape
    n = b.shape[1]
    tm, tn, tk = _tile(m, tm_cap, 16), _tile(n, tn_cap, LANES), _tile(k, tk_cap, LANES)
    return matmul(a, b, m=m, n=n, k=k, tm=tm, tn=tn, tk=tk, dims=_NN, out_dtype=out_dtype, name=name,
                  a_spec=pl.BlockSpec((tm, tk), lambda i, j, kk: (i, kk)),
                  b_spec=pl.BlockSpec((tk, tn), lambda i, j, kk: (kk, j)), **kw)


def mm_nt(a, b, *, out_dtype, name, tm_cap=1088, tn_cap=1024, tk_cap=512, **kw):
    m, k = a.shape
    n = b.shape[0]
    tm, tn, tk = _tile(m, tm_cap, 16), _tile(n, tn_cap, LANES), _tile(k, tk_cap, LANES)
    return matmul(a, b, m=m, n=n, k=k, tm=tm, tn=tn, tk=tk, dims=_NT, out_dtype=out_dtype, name=name,
                  a_spec=pl.BlockSpec((tm, tk), lambda i, j, kk: (i, kk)),
                  b_spec=pl.BlockSpec((tn, tk), lambda i, j, kk: (j, kk)), **kw)


def mm_tn(a, b, *, out_dtype, name, tm_cap=1024, tn_cap=1024, tk_cap=544, **kw):
    k, m = a.shape
    n = b.shape[1]
    tm, tn, tk = _tile(m, tm_cap, LANES), _tile(n, tn_cap, LANES), _tile(k, tk_cap, 16)
    return matmul(a, b, m=m, n=n, k=k, tm=tm, tn=tn, tk=tk, dims=_TN, out_dtype=out_dtype, name=name,
                  a_spec=pl.BlockSpec((tk, tm), lambda i, j, kk: (kk, i)),
                  b_spec=pl.BlockSpec((tk, tn), lambda i, j, kk: (kk, j)), **kw)


def _rms(x):
    r = lax.rsqrt(jnp.mean(x * x, axis=-1, keepdims=True) + EPS)
    return x * r, r


def norm_mod_fwd(x, ctx, nw, scale2, shift2, *, name):
    s, d = x.shape
    cl = 0 if ctx is None else ctx.shape[0]
    tr = _tile(s if ctx is None else cl, 256, 16)
    n_lat = s // tr

    def body(*refs):
        if ctx is None:
            x_ref, nw_ref, sc_ref, sh_ref, o_ref = refs
            v = x_ref[...]
        else:
            x_ref, c_ref, nw_ref, sc_ref, sh_ref, o_ref = refs
            v = jnp.where(pl.program_id(0) < n_lat, x_ref[...], c_ref[...])
        y = _rms(v)[0] * nw_ref[...]
        o_ref[...] = (y * (1.0 + sc_ref[...]) + sh_ref[...]).astype(o_ref.dtype)

    sel = pl.BlockSpec((None, 1, d), lambda i: (i // n_lat, 0, 0))
    ins = [x] if ctx is None else [x, ctx]
    specs = [pl.BlockSpec((tr, d), lambda i: (jnp.minimum(i, n_lat - 1), 0))]
    if ctx is not None:
        specs.append(pl.BlockSpec((tr, d), lambda i: (jnp.maximum(i - n_lat, 0), 0)))
    return pl.pallas_call(
        body, name=name, grid=((s + cl) // tr,),
        in_specs=specs + [pl.BlockSpec((1, d), lambda i: (0, 0)), sel, sel],
        out_specs=pl.BlockSpec((tr, d), lambda i: (i, 0)),
        out_shape=jax.ShapeDtypeStruct((s + cl, d), ACT_DTYPE),
        compiler_params=_params("parallel"),
    )(*ins, nw, scale2, shift2)


def norm_mod_bwd(x, nw, scale, d_hn, *, row0, resid, init, name):
    r, d = x.shape
    tr = _tile(r, 256, 16)
    off = row0 // tr
    want_dx = resid is not None

    def body(*refs):
        x_ref, nw_ref, sc_ref, dh_ref = refs[:4]
        pos = 4
        res_ref = init_ref = dx_ref = None
        if want_dx:
            res_ref, pos = refs[pos], pos + 1
        if init is not None:
            init_ref, pos = refs[pos], pos + 1
        if want_dx:
            dx_ref, pos = refs[pos], pos + 1
        dnw_ref, dsc_ref, dsh_ref = refs[pos:pos + 3]
        i = pl.program_id(0)

        @pl.when(i == 0)
        def _():
            dnw_ref[...] = jnp.zeros_like(dnw_ref) if init_ref is None else init_ref[...]
            dsc_ref[...] = jnp.zeros_like(dsc_ref)
            dsh_ref[...] = jnp.zeros_like(dsh_ref)

        nrm, rs = _rms(x_ref[...])
        w = nw_ref[...]
        dh = dh_ref[...].astype(F32)
        dsh_ref[...] += jnp.sum(dh, axis=0, keepdims=True)
        dsc_ref[...] += jnp.sum(dh * (nrm * w), axis=0, keepdims=True)
        dy = dh * (1.0 + sc_ref[...])
        dnw_ref[...] += jnp.sum(dy * nrm, axis=0, keepdims=True)
        if want_dx:
            dn = dy * w
            dx = rs * (dn - nrm * jnp.mean(dn * nrm, axis=-1, keepdims=True))
            dx_ref[...] = dx + res_ref[...]

    row = pl.BlockSpec((tr, d), lambda i: (i, 0))
    vec = pl.BlockSpec((1, d), lambda i: (0, 0))
    ins, specs = [x, nw, scale, d_hn], [row, vec, vec, pl.BlockSpec((tr, d), lambda i: (i + off, 0))]
    if want_dx:
        ins.append(resid)
        specs.append(row)
    if init is not None:
        ins.append(init)
        specs.append(vec)
    vshape = jax.ShapeDtypeStruct((1, d), F32)
    outs, ospecs = [vshape] * 3, [vec] * 3
    if want_dx:
        outs, ospecs = [jax.ShapeDtypeStruct((r, d), F32)] + outs, [row] + ospecs
    res = pl.pallas_call(body, name=name, grid=(r // tr,), in_specs=specs, out_specs=ospecs, out_shape=outs,
                         compiler_params=_params("arbitrary"))(*ins)
    return tuple(res) if want_dx else (None,) + tuple(res)


def gate_bwd(dx, yo, gate, *, name):
    s, d = dx.shape
    tr = _tile(s, 256, 16)

    def body(dx_ref, yo_ref, g_ref, dyo_ref, dg_ref):
        @pl.when(pl.program_id(0) == 0)
        def _():
            dg_ref[...] = jnp.zeros_like(dg_ref)

        g = dx_ref[...]
        dg_ref[...] += jnp.sum(g * yo_ref[...].astype(F32), axis=0, keepdims=True)
        dyo_ref[...] = (g * g_ref[...]).astype(dyo_ref.dtype)

    row = pl.BlockSpec((tr, d), lambda i: (i, 0))
    vec = pl.BlockSpec((1, d), lambda i: (0, 0))
    return pl.pallas_call(
        body, name=name, grid=(s // tr,), in_specs=[row, row, vec], out_specs=[row, vec],
        out_shape=[jax.ShapeDtypeStruct((s, d), ACT_DTYPE), jax.ShapeDtypeStruct((1, d), F32)],
        compiler_params=_params("arbitrary"))(dx, yo, gate)


def loss_head(x, fw, target, *, name):
    s, d = x.shape
    tr = _tile(s, 256, 16)

    def body(x_ref, w_ref, t_ref, loss_ref, dx_ref, dw_ref):
        @pl.when(pl.program_id(0) == 0)
        def _():
            loss_ref[...] = jnp.zeros_like(loss_ref)
            dw_ref[...] = jnp.zeros_like(dw_ref)

        nrm, rs = _rms(x_ref[...])
        w = w_ref[...]
        err = nrm * w - t_ref[...]
        loss_ref[...] += 0.5 * jnp.sum(jnp.mean(err * err, axis=-1, keepdims=True))
        d_out = err * (1.0 / d)
        dw_ref[...] += jnp.sum(d_out * nrm, axis=0, keepdims=True)
        dn = d_out * w
        dx_ref[...] = rs * (dn - nrm * jnp.mean(dn * nrm, axis=-1, keepdims=True))

    row = pl.BlockSpec((tr, d), lambda i: (i, 0))
    vec = pl.BlockSpec((1, d), lambda i: (0, 0))
    return pl.pallas_call(
        body, name=name, grid=(s // tr,), in_specs=[row, vec, row],
        out_specs=[pl.BlockSpec((SUBLANES, LANES), lambda i: (0, 0)), row, vec],
        out_shape=[jax.ShapeDtypeStruct((SUBLANES, LANES), F32), jax.ShapeDtypeStruct((s, d), F32),
                   jax.ShapeDtypeStruct((1, d), F32)],
        compiler_params=_params("arbitrary"))(x, fw, target)


def _segments(rows, seg_a, seg_b):
    t = _iota((rows, 1), 0)
    if seg_b == 0:
        return t % seg_a, seg_a
    return jnp.where(t < seg_a, t, t - seg_a), jnp.where(t < seg_a, seg_a, seg_b)


def _shift(x, o, seg):
    if o == 0:
        return x
    pos, length = _segments(x.shape[0], *seg)
    y = pltpu.roll(x, (-o) % x.shape[0], 0)
    return jnp.where((pos + o >= 0) & (pos + o < length), y, 0.0)


def _conv(x, w, offsets, seg):
    acc = None
    for j, o in enumerate(offsets):
        term = w[j:j + 1, :] * _shift(x, o, seg)
        acc = term if acc is None else acc + term
    return acc


def _conv_bwd(x, w, dy, offsets, seg):
    dx = None
    dw = jnp.zeros(w.shape, F32)
    row = _iota(w.shape, 0)
    for j, o in enumerate(offsets):
        term = w[j:j + 1, :] * _shift(dy, -o, seg)
        dx = term if dx is None else dx + term
        dwj = jnp.sum(dy * _shift(x, o, seg), axis=0, keepdims=True)
        dw = dw + jnp.where(row == j, dwj, 0.0)
    return dx, dw


def _qkv_post(y, group, scale):
    a = _silu(y)
    n = a * lax.rsqrt(jnp.sum(a * a, axis=-1, keepdims=True) + EPS)
    return jnp.where(group == 0, n * scale, jnp.where(group == 1, n, a))


def qkv_conv_fwd(proj, conv_w, *, s, cl, heads, name):
    m = s + cl
    dh = LANES
    scale = dh ** -0.5

    def body(x_ref, w_ref, o_ref):
        group = pl.program_id(0) // heads
        y = _conv(x_ref[...], w_ref[...], DN_CONV_OFFSETS, (s, cl))
        o_ref[...] = _qkv_post(y, group, scale).astype(o_ref.dtype)

    return pl.pallas_call(
        body, name=name, grid=(3 * heads,),
        in_specs=[pl.BlockSpec((m, dh), lambda j: (0, j)), pl.BlockSpec((len(DN_CONV_OFFSETS), dh), lambda j: (0, j))],
        out_specs=pl.BlockSpec((m, dh), lambda j: (0, j)),
        out_shape=jax.ShapeDtypeStruct((m, 3 * heads * dh), ACT_DTYPE),
        compiler_params=_params("parallel"))(proj, conv_w)


def qkv_conv_bwd(proj, conv_w, dqkv, *, s, cl, heads, name):
    m = s + cl
    dh = LANES
    scale = dh ** -0.5
    kk = len(DN_CONV_OFFSETS)

    def body(x_ref, w_ref, d_ref, dx_ref, dw_ref):
        group = pl.program_id(0) // heads
        x, w = x_ref[...], w_ref[...]
        y = _conv(x, w, DN_CONV_OFFSETS, (s, cl))
        a = _silu(y)
        rn = lax.rsqrt(jnp.sum(a * a, axis=-1, keepdims=True) + EPS)
        n = a * rn
        dout = d_ref[...] * jnp.where(group == 0, scale, 1.0)
        da_norm = rn * (dout - n * jnp.sum(dout * n, axis=-1, keepdims=True))
        dy = jnp.where(group == 2, dout, da_norm) * _dsilu(y)
        dx, dw = _conv_bwd(x, w, dy, DN_CONV_OFFSETS, (s, cl))
        dx_ref[...] = dx.astype(dx_ref.dtype)
        dw_ref[...] = dw

    col = pl.BlockSpec((m, dh), lambda j: (0, j))
    wspec = pl.BlockSpec((kk, dh), lambda j: (0, j))
    return pl.pallas_call(
        body, name=name, grid=(3 * heads,),
        in_specs=[col, wspec, pl.BlockSpec((None, m, dh), lambda j: (j // heads, 0, j % heads))],
        out_specs=[col, wspec],
        out_shape=[jax.ShapeDtypeStruct((m, 3 * heads * dh), ACT_DTYPE),
                   jax.ShapeDtypeStruct((kk, 3 * heads * dh), F32)],
        compiler_params=_params("parallel"))(proj, conv_w, dqkv)


def _scan_masks(d):
    t, s = _iota((CHUNK, CHUNK), 0), _iota((CHUNK, CHUNK), 1)
    return ((s <= t), (s < t)) if d == 0 else ((s >= t), (s > t))


def _bmm(spec, a, b, precision=None):
    if precision is None:
        a, b = a.astype(MXU_DTYPE), b.astype(MXU_DTYPE)
    return jnp.einsum(spec, a, b, precision=precision, preferred_element_type=F32)


def _unit_tri_inverse(a):
    mm = functools.partial(_bmm, 'nts,nsr->ntr', precision=INV_PRECISION)
    row, col = _iota((CHUNK, CHUNK), 0), _iota((CHUNK, CHUNK), 1)
    eye = (row == col).astype(F32)
    dg = jnp.where(row // INV_BLOCK == col // INV_BLOCK, a, 0.0)
    off = a - dg
    p = eye - dg
    pw = dg
    for _ in range(3):
        pw = mm(pw, pw)
        p = p + mm(p, pw)
    n = mm(p, off)
    r = eye - n
    return mm(r + mm(r, mm(n, n)), p)


def _dn_intra(q, k, v, beta_b, gc_b, d):
    dh = q.shape[-1]
    incl, strict = _scan_masks(d)
    gc64 = gc_b[:, :, :CHUNK]
    diff = gc64 - jnp.swapaxes(gc64, 1, 2)
    decay = jnp.where(incl, jnp.exp(jnp.where(incl, diff, 0.0)), 0.0)
    qk_kk = _bmm('ntd,nsd->nts', jnp.concatenate([q, k], axis=1), k)
    qk, kk = qk_kk[:, :CHUNK], qk_kk[:, CHUNK:]
    a = jnp.where(strict, beta_b[:, :, :CHUNK] * kk * decay, 0.0)
    tinv = _unit_tri_inverse(a)
    rhs = jnp.concatenate([beta_b * jnp.exp(gc_b) * k, beta_b * v], axis=-1)
    wu = _bmm('nts,nsd->ntd', tinv, rhs, INV_PRECISION)
    w, u = wu[:, :, :dh], wu[:, :, dh:]
    last = CHUNK - 1 if d == 0 else 0
    gl = gc_b[:, last:last + 1, :]
    ke = k * jnp.exp(gl - gc_b)
    ge = jnp.exp(gl)
    return w, u, ke, ge, qk * decay, q * jnp.exp(gc_b)


def _dn_step(s, w, u, ke, ge, aqk, qg):
    ws_qs = _dot(jnp.concatenate([w, qg], axis=0), s, _NN)
    u2 = u - ws_qs[:CHUNK]
    s_new = ge * s + _dot(ke, u2, _TN)
    o = ws_qs[CHUNK:] + _dot(aqk, u2, _NN)
    return s_new, o


def _chunk_cumsum(x, d):
    rows = x.shape[0]
    pos = _iota((rows, 1), 0) % CHUNK
    step = 1
    while step < CHUNK:
        if d == 0:
            x = x + jnp.where(pos >= step, pltpu.roll(x, step, 0), 0.0)
        else:
            x = x + jnp.where(pos < CHUNK - step, pltpu.roll(x, rows - step, 0), 0.0)
        step *= 2
    return x


def _pick_lane(x, j):
    return jnp.sum(jnp.where(_iota(x.shape, 1) == j, x, 0.0), axis=1, keepdims=True)


def _dn_gates(ba, a_log, dt_bias, d, h, heads):
    braw = _pick_lane(ba, d * heads + h)
    araw = _pick_lane(ba, (2 + d) * heads + h)
    a_neg = -jnp.exp(_pick_lane(a_log[d:d + 1, :], h))
    pre = araw + _pick_lane(dt_bias[d:d + 1, :], h)
    return jax.nn.sigmoid(braw), a_neg * jax.nn.softplus(pre), pre, a_neg


_DN_SUB_FWD = 16
_DN_SUB_BWD = 8


def _for_sub_batches(s, cl, fn, sub=_DN_SUB_FWD):
    for base, total in ((0, s), (s, cl)):
        nch = min(sub, total // CHUNK)
        rows_per = nch * CHUNK
        count = total // rows_per

        def run(i, carry, base=base, nch=nch, rows_per=rows_per):
            row0 = pl.multiple_of(base + i * rows_per, rows_per)
            ge0 = pl.multiple_of((base // CHUNK + i * nch) * SUBLANES, nch * SUBLANES)
            fn(pl.ds(row0, rows_per), pl.ds(ge0, nch * SUBLANES), nch)
            return carry

        if count == 1:
            fn(pl.ds(base, rows_per), pl.ds(base // CHUNK * SUBLANES, nch * SUBLANES), nch)
        else:
            lax.fori_loop(0, count, run, 0)


def _dn_chunk_order(t, d, n_lat, n_ctx):
    if d == 0:
        return jnp.where(t < n_ctx, n_lat + t, t - n_ctx)
    return n_lat + n_ctx - 1 - t


def _dn_fill_intra(q_ref, k_ref, v_ref, bb_s, gc_s, w_s, u_s, ke_s, ge_s, aqk_s, qg_s, d, s, cl):
    dh = LANES

    def fill(rows, ge_rows, nch):
        def load(ref):
            return ref[rows, :].astype(F32).reshape(nch, CHUNK, dh)

        w, u, ke, ge, aqk, qg = _dn_intra(load(q_ref), load(k_ref), load(v_ref), load(bb_s), load(gc_s), d)
        w_s[rows, :] = w.reshape(nch * CHUNK, dh)
        u_s[rows, :] = u.reshape(nch * CHUNK, dh)
        ke_s[rows, :] = ke.reshape(nch * CHUNK, dh)
        qg_s[rows, :] = qg.reshape(nch * CHUNK, dh)
        aqk_s[rows, :] = aqk.reshape(nch * CHUNK, CHUNK)
        ge_s[ge_rows, :] = jnp.broadcast_to(ge, (nch, SUBLANES, dh)).reshape(nch * SUBLANES, dh)

    _for_sub_batches(s, cl, fill)


def _dn_chunk_refs(cid, w_s, u_s, ke_s, ge_s, aqk_s, qg_s):
    rows = pl.ds(pl.multiple_of(cid * CHUNK, CHUNK), CHUNK)
    ge = ge_s[pl.ds(pl.multiple_of(cid * SUBLANES, SUBLANES), SUBLANES), :][0:1]
    return rows, (w_s[rows, :], u_s[rows, :], ke_s[rows, :], ge, aqk_s[rows, :], qg_s[rows, :])


def _dn_scratch(m):
    dh = LANES
    big = pltpu.VMEM((m, dh), F32)
    return [big, big, big, big, big, pltpu.VMEM((m // CHUNK * SUBLANES, dh), F32), pltpu.VMEM((m, CHUNK), F32), big]


def _once(shape, index_map):
    return pl.BlockSpec(shape, index_map, pipeline_mode=pl.Buffered(1))


def dn_fwd(qkv, ba, a_log, dt_bias, *, s, cl, heads, name):
    m = s + cl
    dh = LANES
    n_lat, n_ctx = s // CHUNK, cl // CHUNK

    def body(q_ref, k_ref, v_ref, ba_ref, al_ref, dt_ref, o_ref, bb_s, gc_s, w_s, u_s, ke_s, ge_s, aqk_s, qg_s):
        h = pl.program_id(0)
        for d in (0, 1):
            beta, g, _, _ = _dn_gates(ba_ref[...], al_ref[...], dt_ref[...], d, h, heads)
            bb_s[...] = jnp.broadcast_to(beta, (m, dh))
            gc_s[...] = _chunk_cumsum(jnp.broadcast_to(g, (m, dh)), d)
            _dn_fill_intra(q_ref, k_ref, v_ref, bb_s, gc_s, w_s, u_s, ke_s, ge_s, aqk_s, qg_s, d, s, cl)

            def step(t, state):
                cid = _dn_chunk_order(t, d, n_lat, n_ctx)
                rows, terms = _dn_chunk_refs(cid, w_s, u_s, ke_s, ge_s, aqk_s, qg_s)
                state, o = _dn_step(state, *terms)

                @pl.when(cid < n_lat)
                def _():
                    if d == 0:
                        o_ref[rows, :] = o
                    else:
                        o_ref[rows, :] += o

                return state

            lax.fori_loop(0, n_lat + n_ctx, step, jnp.zeros((dh, dh), F32))

    def col(j0):
        return _once((m, dh), lambda h: (0, j0 + h))

    small = pl.BlockSpec((SUBLANES, LANES), lambda h: (0, 0))
    return pl.pallas_call(
        body, name=name, grid=(heads,),
        in_specs=[col(0), col(heads), col(2 * heads), _once((m, LANES), lambda h: (0, 0)), small, small],
        out_specs=pl.BlockSpec((s, dh), lambda h: (0, h)),
        out_shape=jax.ShapeDtypeStruct((s, heads * dh), F32),
        scratch_shapes=_dn_scratch(m),
        compiler_params=_params("parallel", vmem=BIG_KERNEL_VMEM))(qkv, qkv, qkv, ba, a_log, dt_bias)


def dn_out_fwd(o, proj, dn_norm, *, heads, gate_col0, name):
    s = o.shape[0]
    dh = LANES
    tr = _tile(s, 1024, 16)

    def body(o_ref, g_ref, nw_ref, y_ref):
        y_ref[...] = (_rms(o_ref[...])[0] * nw_ref[...] * _silu(g_ref[...])).astype(y_ref.dtype)

    blk = pl.BlockSpec((tr, dh), lambda i, h: (i, h))
    return pl.pallas_call(
        body, name=name, grid=(s // tr, heads),
        in_specs=[blk, pl.BlockSpec((tr, dh), lambda i, h: (i, gate_col0 + h)), pl.BlockSpec((1, dh), lambda i, h: (0, 0))],
        out_specs=blk, out_shape=jax.ShapeDtypeStruct((s, heads * dh), ACT_DTYPE),
        compiler_params=_params("parallel", "parallel"))(o, proj, dn_norm)


def dn_out_bwd(o, proj, d_y, dn_norm, *, heads, gate_col0, name):
    s = o.shape[0]
    dh = LANES
    tr = _tile(s, 1024, 16)

    def body(o_ref, g_ref, dy_ref, nw_ref, do_ref, dg_ref, dnw_ref):
        @pl.when((pl.program_id(0) == 0) & (pl.program_id(1) == 0))
        def _():
            dnw_ref[...] = jnp.zeros_like(dnw_ref)

        nrm, rs = _rms(o_ref[...])
        nw, gate, dy = nw_ref[...], g_ref[...], dy_ref[...]
        dg_ref[...] = (dy * (nrm * nw) * _dsilu(gate)).astype(dg_ref.dtype)
        dy0 = dy * _silu(gate)
        dnw_ref[0:1, :] += jnp.sum(dy0 * nrm, axis=0, keepdims=True)
        dn = dy0 * nw
        do_ref[...] = rs * (dn - nrm * jnp.mean(dn * nrm, axis=-1, keepdims=True))

    blk = pl.BlockSpec((tr, dh), lambda i, h: (i, h))
    return pl.pallas_call(
        body, name=name, grid=(s // tr, heads),
        in_specs=[blk, pl.BlockSpec((tr, dh), lambda i, h: (i, gate_col0 + h)), blk,
                  pl.BlockSpec((1, dh), lambda i, h: (0, 0))],
        out_specs=[blk, blk, pl.BlockSpec((SUBLANES, LANES), lambda i, h: (0, 0))],
        out_shape=[jax.ShapeDtypeStruct((s, heads * dh), F32), jax.ShapeDtypeStruct((s, heads * dh), ACT_DTYPE),
                   jax.ShapeDtypeStruct((SUBLANES, LANES), F32)],
        compiler_params=_params("arbitrary", "arbitrary"))(o, proj, d_y, dn_norm)


def dn_bwd(qkv, ba, d_o, a_log, dt_bias, *, s, cl, heads, name):
    m = s + cl
    dh = LANES
    n_lat, n_ctx = s // CHUNK, cl // CHUNK
    n_all = n_lat + n_ctx

    def body(q_ref, k_ref, v_ref, ba_ref, do_ref, al_ref, dt_ref, dqkv_ref, dba_ref, dal_ref, ddt_ref,
             bb_s, gc_s, w_s, u_s, ke_s, ge_s, aqk_s, qg_s, sall_s):
        h = pl.program_id(0)
        dq_ref, dk_ref, dv_ref = dqkv_ref.at[0], dqkv_ref.at[1], dqkv_ref.at[2]

        @pl.when(h == 0)
        def _():
            dba_ref[...] = jnp.zeros_like(dba_ref)

        lane = _iota((m, LANES), 1)
        for d in (0, 1):
            beta, g, pre, a_neg = _dn_gates(ba_ref[...], al_ref[...], dt_ref[...], d, h, heads)
            bb_s[...] = jnp.broadcast_to(beta, (m, dh))
            gc_s[...] = _chunk_cumsum(jnp.broadcast_to(g, (m, dh)), d)
            _dn_fill_intra(q_ref, k_ref, v_ref, bb_s, gc_s, w_s, u_s, ke_s, ge_s, aqk_s, qg_s, d, s, cl)

            def fwd_step(t, state):
                cid = _dn_chunk_order(t, d, n_lat, n_ctx)
                _, terms = _dn_chunk_refs(cid, w_s, u_s, ke_s, ge_s, aqk_s, qg_s)
                sall_s[pl.ds(pl.multiple_of(cid * dh, dh), dh), :] = state
                return _dn_step(state, *terms)[0]

            lax.fori_loop(0, n_all, fwd_step, jnp.zeros((dh, dh), F32))

            def bwd_step(i, dstate):
                cid = _dn_chunk_order(n_all - 1 - i, d, n_lat, n_ctx)
                rows, terms = _dn_chunk_refs(cid, w_s, u_s, ke_s, ge_s, aqk_s, qg_s)
                state = sall_s[pl.ds(pl.multiple_of(cid * dh, dh), dh), :]
                _, vjp = jax.vjp(_dn_step, state, *terms)
                lat_rows = pl.ds(pl.multiple_of(jnp.minimum(cid, n_lat - 1) * CHUNK, CHUNK), CHUNK)
                do = jnp.where(cid < n_lat, do_ref[lat_rows, :], 0.0)
                dstate, dw, du, dke, dge, daqk, dqg = vjp((dstate, do))
                w_s[rows, :] = dw
                u_s[rows, :] = du
                ke_s[rows, :] = dke
                qg_s[rows, :] = dqg
                aqk_s[rows, :] = daqk
                ge_s[pl.ds(pl.multiple_of(cid * SUBLANES, SUBLANES), SUBLANES), :] = jnp.broadcast_to(
                    dge, (SUBLANES, dh))
                return dstate

            lax.fori_loop(0, n_all, bwd_step, jnp.zeros((dh, dh), F32))

            def intra_bwd(rows, ge_rows, nch):
                def load(ref, width=dh):
                    return ref[rows, :].astype(F32).reshape(nch, CHUNK, width)

                _, vjp = jax.vjp(functools.partial(_dn_intra, d=d), load(q_ref), load(k_ref), load(v_ref),
                                 load(bb_s), load(gc_s))
                dge = ge_s[ge_rows, :].reshape(nch, SUBLANES, dh)[:, 0:1]
                dq, dk, dv, dbb, dgc = vjp((load(w_s), load(u_s), load(ke_s), dge, load(aqk_s, CHUNK), load(qg_s)))
                flat = lambda x: x.reshape(nch * CHUNK, dh)
                if d == 0:
                    dq_ref[rows, :], dk_ref[rows, :], dv_ref[rows, :] = flat(dq), flat(dk), flat(dv)
                else:
                    dq_ref[rows, :] += flat(dq)
                    dk_ref[rows, :] += flat(dk)
                    dv_ref[rows, :] += flat(dv)
                bb_s[rows, :] = flat(dbb)
                gc_s[rows, :] = flat(dgc)

            _for_sub_batches(s, cl, intra_bwd, _DN_SUB_BWD)

            dbeta = jnp.sum(bb_s[...], axis=1, keepdims=True)
            dg = jnp.sum(_chunk_cumsum(gc_s[...], 1 - d), axis=1, keepdims=True)
            dbraw = dbeta * beta * (1.0 - beta)
            dpre = dg * a_neg * jax.nn.sigmoid(pre)
            dba_ref[...] += (jnp.where(lane == d * heads + h, dbraw, 0.0)
                             + jnp.where(lane == (2 + d) * heads + h, dpre, 0.0))
            dal_ref[d:d + 1, :] = jnp.broadcast_to(jnp.sum(dg * g, axis=0, keepdims=True), (1, LANES))
            ddt_ref[d:d + 1, :] = jnp.broadcast_to(jnp.sum(dpre, axis=0, keepdims=True), (1, LANES))
        dal_ref[2:SUBLANES, :] = jnp.zeros((SUBLANES - 2, LANES), F32)
        ddt_ref[2:SUBLANES, :] = jnp.zeros((SUBLANES - 2, LANES), F32)

    def col(j0, rows=m):
        return _once((rows, dh), lambda h: (0, j0 + h))

    small = pl.BlockSpec((SUBLANES, LANES), lambda h: (0, 0))
    tile_h = pl.BlockSpec((None, SUBLANES, LANES), lambda h: (h, 0, 0))
    tiles = jax.ShapeDtypeStruct((heads, SUBLANES, LANES), F32)
    return pl.pallas_call(
        body, name=name, grid=(heads,),
        in_specs=[col(0), col(heads), col(2 * heads), _once((m, LANES), lambda h: (0, 0)), col(0, s), small, small],
        out_specs=[_once((3, m, dh), lambda h: (0, 0, h)), _once((m, LANES), lambda h: (0, 0)), tile_h, tile_h],
        out_shape=[jax.ShapeDtypeStruct((3, m, heads * dh), F32), jax.ShapeDtypeStruct((m, LANES), F32), tiles, tiles],
        scratch_shapes=_dn_scratch(m) + [pltpu.VMEM((n_all * dh, dh), F32)],
        compiler_params=_params("arbitrary", vmem=BIG_KERNEL_VMEM))(qkv, qkv, qkv, ba, d_o, a_log, dt_bias)


def _lin_scan(a, b, d):
    rows = a.shape[0]
    t = _iota((rows, 1), 0)
    step = 1
    while step < rows:
        if d == 0:
            ok, sa, sb = t >= step, pltpu.roll(a, step, 0), pltpu.roll(b, step, 0)
        else:
            ok, sa, sb = t < rows - step, pltpu.roll(a, rows - step, 0), pltpu.roll(b, rows - step, 0)
        b = b + a * jnp.where(ok, sb, 0.0)
        a = a * jnp.where(ok, sa, 1.0)
        step *= 2
    return b


def _lru_gates(xc, w_r, b_r, w_i, b_i, lam):
    r = jax.nn.sigmoid(_dot(xc, w_r, _NN) + b_r)
    i = jax.nn.sigmoid(_dot(xc, w_i, _NN) + b_i)
    log_a = -LRU_C * r * jax.nn.softplus(-lam)
    z = 2.0 * log_a
    series = -(z * (1.0 + z * (0.5 + z * (1.0 / 6.0))))
    one_minus = jnp.where(z > -0.01, series, 1.0 - jnp.exp(z))
    return jnp.exp(log_a), jnp.sqrt(one_minus) * (i * xc)


def _lru_states(a, b, d, s, cl):
    ac, bc = a[s:], b[s:]
    hc = _lin_scan(ac, bc, d)
    h0 = hc[cl - 1:cl] if d == 0 else hc[0:1]
    first = 0 if d == 0 else s - 1
    al = a[:s]
    bl = b[:s] + jnp.where(_iota((s, 1), 0) == first, al * h0, 0.0)
    return _lin_scan(al, bl, d), hc, h0


def _lru_specs(m, nb_dim):
    j_col = lambda rows: pl.BlockSpec((rows, LANES), lambda j: (0, j))
    w_blk = pl.BlockSpec((2, None, nb_dim, nb_dim), lambda j: (0, j, 0, 0))
    return j_col, w_blk


def lru_fwd(xcm, conv_w, conv_b, w_r, b_r, w_i, b_i, lam, *, s, cl, name):
    m, width = xcm.shape
    j_col, w_blk = _lru_specs(m, w_r.shape[-1])

    def body(x_ref, cw_ref, cb_ref, wr_ref, br_ref, wi_ref, bi_ref, lam_ref, h_ref):
        xc = _conv(x_ref[...], cw_ref[...], DN_CONV_OFFSETS, (s, cl)) + cb_ref[...]
        for d in (0, 1):
            a, b = _lru_gates(xc, wr_ref[d], br_ref[d:d + 1, :], wi_ref[d], bi_ref[d:d + 1, :], lam_ref[d:d + 1, :])
            h = _lru_states(a, b, d, s, cl)[0]
            if d == 0:
                h_ref[...] = h
            else:
                h_ref[...] += h

    return pl.pallas_call(
        body, name=name, grid=(width // LANES,),
        in_specs=[_once((m, LANES), lambda j: (0, j)), j_col(len(DN_CONV_OFFSETS)), j_col(1), w_blk, j_col(2), w_blk,
                  j_col(2), j_col(2)],
        out_specs=j_col(s), out_shape=jax.ShapeDtypeStruct((s, width), F32),
        compiler_params=_params("parallel", vmem=BIG_KERNEL_VMEM))(xcm, conv_w, conv_b, w_r, b_r, w_i, b_i, lam)


def lru_bwd(xcm, d_h, conv_w, conv_b, w_r, b_r, w_i, b_i, lam, *, s, cl, name):
    m, width = xcm.shape
    nb_dim = w_r.shape[-1]
    j_col, w_blk = _lru_specs(m, nb_dim)

    def body(x_ref, dh_ref, cw_ref, cb_ref, wr_ref, br_ref, wi_ref, bi_ref, lam_ref,
             dx_ref, dcw_ref, dcb_ref, dwr_ref, dbr_ref, dwi_ref, dbi_ref, dlam_ref):
        x, cw = x_ref[...], cw_ref[...]
        xc = _conv(x, cw, DN_CONV_OFFSETS, (s, cl)) + cb_ref[...]
        d_hl = dh_ref[...]
        d_xc = None
        for d in (0, 1):
            (a, b), vjp = jax.vjp(_lru_gates, xc, wr_ref[d], br_ref[d:d + 1, :], wi_ref[d], bi_ref[d:d + 1, :],
                                  lam_ref[d:d + 1, :])
            h, hc, h0 = _lru_states(a, b, d, s, cl)
            nxt = 1 if d == 0 else -1
            first = 0 if d == 0 else s - 1
            al, ac = a[:s], a[s:]
            lam_l = _lin_scan(_shift(al, nxt, (s, 0)), d_hl, 1 - d)
            h_prev = _shift(h, -nxt, (s, 0)) + jnp.where(_iota((s, 1), 0) == first, h0, 0.0)
            d_h0 = (al * lam_l)[first:first + 1]
            last_c = cl - 1 if d == 0 else 0
            d_hc = jnp.where(_iota((cl, 1), 0) == last_c, d_h0, 0.0)
            lam_c = _lin_scan(_shift(ac, nxt, (cl, 0)), d_hc, 1 - d)
            da = jnp.concatenate([lam_l * h_prev, lam_c * _shift(hc, -nxt, (cl, 0))], axis=0)
            db = jnp.concatenate([lam_l, lam_c], axis=0)
            g_xc, g_wr, g_br, g_wi, g_bi, g_lam = vjp((da, db))
            d_xc = g_xc if d_xc is None else d_xc + g_xc
            dwr_ref[d], dwi_ref[d] = g_wr, g_wi
            dbr_ref[d:d + 1, :], dbi_ref[d:d + 1, :], dlam_ref[d:d + 1, :] = g_br, g_bi, g_lam
        dx, dcw = _conv_bwd(x, cw, d_xc, DN_CONV_OFFSETS, (s, cl))
        dx_ref[...] = dx.astype(dx_ref.dtype)
        dcw_ref[...] = dcw
        dcb_ref[...] = jnp.sum(d_xc, axis=0, keepdims=True)

    kk = len(DN_CONV_OFFSETS)
    vec2 = jax.ShapeDtypeStruct((2, width), F32)
    return pl.pallas_call(
        body, name=name, grid=(width // LANES,),
        in_specs=[_once((m, LANES), lambda j: (0, j)), _once((s, LANES), lambda j: (0, j)), j_col(kk), j_col(1),
                  w_blk, j_col(2), w_blk, j_col(2), j_col(2)],
        out_specs=[j_col(m), j_col(kk), j_col(1), w_blk, j_col(2), w_blk, j_col(2), j_col(2)],
        out_shape=[jax.ShapeDtypeStruct((m, width), ACT_DTYPE), jax.ShapeDtypeStruct((kk, width), F32),
                   jax.ShapeDtypeStruct((1, width), F32), jax.ShapeDtypeStruct(w_r.shape, F32), vec2,
                   jax.ShapeDtypeStruct(w_i.shape, F32), vec2, vec2],
        compiler_params=_params("parallel", vmem=BIG_KERNEL_VMEM))(xcm, d_h, conv_w, conv_b, w_r, b_r, w_i, b_i, lam)


def lru_gate_fwd(h, proj, *, gate_col0, name):
    s, width = h.shape
    tr, tc = _tile(s, 512, 16), _tile(width, 512, LANES)
    c0 = gate_col0 * LANES // tc

    def body(h_ref, g_ref, y_ref):
        y_ref[...] = (h_ref[...] * _silu(g_ref[...])).astype(y_ref.dtype)

    blk = pl.BlockSpec((tr, tc), lambda i, j: (i, j))
    return pl.pallas_call(
        body, name=name, grid=(s // tr, width // tc),
        in_specs=[blk, pl.BlockSpec((tr, tc), lambda i, j: (i, c0 + j))], out_specs=blk,
        out_shape=jax.ShapeDtypeStruct((s, width), ACT_DTYPE),
        compiler_params=_params("parallel", "parallel"))(h, proj)


def lru_gate_bwd(h, proj, d_y, *, gate_col0, dy_col0, name):
    s, width = h.shape
    tr, tc = _tile(s, 512, 16), _tile(width, 512, LANES)
    c0, y0 = gate_col0 * LANES // tc, dy_col0 * LANES // tc

    def body(h_ref, g_ref, dy_ref, dh_ref, dg_ref):
        g, dy = g_ref[...], dy_ref[...]
        dh_ref[...] = dy * _silu(g)
        dg_ref[...] = (dy * h_ref[...] * _dsilu(g)).astype(dg_ref.dtype)

    blk = pl.BlockSpec((tr, tc), lambda i, j: (i, j))
    return pl.pallas_call(
        body, name=name, grid=(s // tr, width // tc),
        in_specs=[blk, pl.BlockSpec((tr, tc), lambda i, j: (i, c0 + j)), pl.BlockSpec((tr, tc), lambda i, j: (i, y0 + j))],
        out_specs=[blk, blk],
        out_shape=[jax.ShapeDtypeStruct((s, width), F32), jax.ShapeDtypeStruct((s, width), ACT_DTYPE)],
        compiler_params=_params("parallel", "parallel"))(h, proj, d_y)


def _sc_parts(p, width):
    return [p[:, k * width:(k + 1) * width] for k in range(4)]


def sc_mix_fwd(p, conv_w, *, name):
    s, width = p.shape[0], conv_w.shape[1]
    tr = 2 * GRID_W

    def body(p_ref, w_ref, y_ref):
        b_g, c_g, x_in, gate = _sc_parts(p_ref[...], width)
        z = _conv(c_g * x_in, w_ref[...], SC_CONV_OFFSETS, (GRID_W, 0))
        y_ref[...] = (b_g * z * _silu(gate)).astype(y_ref.dtype)

    return pl.pallas_call(
        body, name=name, grid=(s // tr,),
        in_specs=[pl.BlockSpec((tr, 4 * width), lambda i: (i, 0)), pl.BlockSpec(conv_w.shape, lambda i: (0, 0))],
        out_specs=pl.BlockSpec((tr, width), lambda i: (i, 0)),
        out_shape=jax.ShapeDtypeStruct((s, width), ACT_DTYPE),
        compiler_params=_params("parallel"))(p, conv_w)


def sc_mix_bwd(p, d_y, conv_w, *, name):
    s, width = p.shape[0], conv_w.shape[1]
    tr = 2 * GRID_W

    def body(p_ref, dy_ref, w_ref, dp_ref, dw_ref):
        @pl.when(pl.program_id(0) == 0)
        def _():
            dw_ref[...] = jnp.zeros_like(dw_ref)

        b_g, c_g, x_in, gate = _sc_parts(p_ref[...], width)
        w, dy = w_ref[...], dy_ref[...]
        u = c_g * x_in
        z = _conv(u, w, SC_CONV_OFFSETS, (GRID_W, 0))
        sg = _silu(gate)
        du, dw = _conv_bwd(u, w, dy * b_g * sg, SC_CONV_OFFSETS, (GRID_W, 0))
        dw_ref[...] += dw
        parts = (dy * z * sg, du * x_in, du * c_g, dy * b_g * z * _dsilu(gate))
        for k, part in enumerate(parts):
            dp_ref[:, k * width:(k + 1) * width] = part.astype(dp_ref.dtype)

    return pl.pallas_call(
        body, name=name, grid=(s // tr,),
        in_specs=[pl.BlockSpec((tr, 4 * width), lambda i: (i, 0)), pl.BlockSpec((tr, width), lambda i: (i, 0)),
                  pl.BlockSpec(conv_w.shape, lambda i: (0, 0))],
        out_specs=[pl.BlockSpec((tr, 4 * width), lambda i: (i, 0)), pl.BlockSpec(conv_w.shape, lambda i: (0, 0))],
        out_shape=[jax.ShapeDtypeStruct((s, 4 * width), ACT_DTYPE), jax.ShapeDtypeStruct(conv_w.shape, F32)],
        compiler_params=_params("arbitrary"))(p, d_y, conv_w)


MOD_ROWS = 16


def mod_fwd(cond, mod_w, mod_b, *, name):
    nl, d, ns = mod_w.shape
    tn = _tile(ns, 512, LANES)

    def body(c_ref, w_ref, b_ref, o_ref):
        o_ref[...] = _dot(_silu(c_ref[...]), w_ref[...], _NN) + b_ref[...]

    return pl.pallas_call(
        body, name=name, grid=(nl, ns // tn),
        in_specs=[pl.BlockSpec((MOD_ROWS, d), lambda l, j: (0, 0)), pl.BlockSpec((None, d, tn), lambda l, j: (l, 0, j)),
                  pl.BlockSpec((None, 1, tn), lambda l, j: (l, 0, j))],
        out_specs=pl.BlockSpec((None, MOD_ROWS, tn), lambda l, j: (l, 0, j)),
        out_shape=jax.ShapeDtypeStruct((nl, MOD_ROWS, ns), F32),
        compiler_params=_params("parallel", "parallel"))(cond, mod_w, mod_b)


def _adamw(w, g, m, v):
    m = ADAM_B1 * m + (1.0 - ADAM_B1) * g
    v = ADAM_B2 * v + (1.0 - ADAM_B2) * (g * g)
    m_hat = m / (1.0 - ADAM_B1 ** ADAM_STEP)
    v_hat = v / (1.0 - ADAM_B2 ** ADAM_STEP)
    return -ADAM_LR * (m_hat / (jnp.sqrt(v_hat) + ADAM_EPS) + ADAM_WD * w), m, v


def mod_adam(cond, d_mod, w, m, v, *, name):
    nl, d, ns = w.shape
    tr, tn = _tile(d, 256, SUBLANES), _tile(ns, 1024, LANES)

    def body(c_ref, dm_ref, w_ref, m_ref, v_ref, g_ref, dl_ref, nm_ref, nv_ref, ds_ref):
        @pl.when(pl.program_id(2) == 0)
        def _():
            ds_ref[...] = jnp.zeros_like(ds_ref)

        wv, dm = w_ref[...], dm_ref[...]
        g = _dot(_silu(c_ref[...]), dm, _TN)
        ds_ref[...] += _dot(dm, wv, _NT)
        g_ref[...] = g
        dl_ref[...], nm_ref[...], nv_ref[...] = _adamw(wv, g, m_ref[...], v_ref[...])

    blk = pl.BlockSpec((None, tr, tn), lambda l, i, j: (l, i, j))
    full = jax.ShapeDtypeStruct(w.shape, F32)
    return pl.pallas_call(
        body, name=name, grid=(nl, d // tr, ns // tn),
        in_specs=[pl.BlockSpec((MOD_ROWS, tr), lambda l, i, j: (0, i)),
                  pl.BlockSpec((None, MOD_ROWS, tn), lambda l, i, j: (l, 0, j)), blk, blk, blk],
        out_specs=[blk, blk, blk, blk, pl.BlockSpec((None, MOD_ROWS, tr), lambda l, i, j: (l, 0, i))],
        out_shape=[full, full, full, full, jax.ShapeDtypeStruct((nl, MOD_ROWS, d), F32)],
        compiler_params=_params("parallel", "parallel", "arbitrary"))(cond, d_mod, w, m, v)


def _row_tile(rows, cols, itemsize, mult):
    return _tile(rows, max(mult, (2 << 20) // (cols * itemsize)), mult)


def adam_update(w, g, m, v, *, name):
    r, c = w.shape
    tr = _row_tile(r, c, 4, SUBLANES)

    def body(w_ref, g_ref, m_ref, v_ref, dl_ref, nm_ref, nv_ref):
        dl_ref[...], nm_ref[...], nv_ref[...] = _adamw(w_ref[...], g_ref[...], m_ref[...], v_ref[...])

    blk = pl.BlockSpec((tr, c), lambda i: (i, 0))
    return pl.pallas_call(
        body, name=name, grid=(r // tr,), in_specs=[blk] * 4, out_specs=[blk] * 3,
        out_shape=[jax.ShapeDtypeStruct((r, c), F32)] * 3, compiler_params=_params("parallel"))(w, g, m, v)


def cast_rows(x, dtype, *, name):
    r, c = x.shape
    tr = _row_tile(r, c, 4, 16)

    def body(x_ref, o_ref):
        o_ref[...] = x_ref[...].astype(dtype)

    blk = pl.BlockSpec((tr, c), lambda i: (i, 0))
    return pl.pallas_call(body, name=name, grid=(r // tr,), in_specs=[blk], out_specs=blk,
                          out_shape=jax.ShapeDtypeStruct((r, c), dtype), compiler_params=_params("parallel"))(x)


def add_sibling_half(core, mine, other, *, name):
    a, _, r, c = mine.shape
    tr = _row_tile(r, c, 4, 16)

    def body(core_ref, x_ref, y_ref, o_ref):
        o_ref[...] = (x_ref[...].astype(F32) + y_ref[...].astype(F32)).astype(o_ref.dtype)

    grid_spec = pltpu.PrefetchScalarGridSpec(
        num_scalar_prefetch=1, grid=(a, r // tr),
        in_specs=[pl.BlockSpec((None, None, tr, c), lambda k, i, core_ref: (k, core_ref[0], i, 0)),
                  pl.BlockSpec((None, tr, c), lambda k, i, core_ref: (k, i, 0))],
        out_specs=pl.BlockSpec((None, tr, c), lambda k, i, core_ref: (k, i, 0)))
    return pl.pallas_call(body, name=name, grid_spec=grid_spec, out_shape=jax.ShapeDtypeStruct((a, r, c), WIRE_DTYPE),
                          compiler_params=_params("parallel", "parallel"))(core, mine, other)


def sum_slots(x, *, name):
    n, r, c = x.shape
    tr = _row_tile(r, c * n, 4, 16)

    def body(x_ref, o_ref):
        acc = x_ref[0].astype(F32)
        for k in range(1, n):
            acc = acc + x_ref[k].astype(F32)
        o_ref[...] = acc

    return pl.pallas_call(
        body, name=name, grid=(r // tr,), in_specs=[pl.BlockSpec((n, tr, c), lambda i: (0, i, 0))],
        out_specs=pl.BlockSpec((tr, c), lambda i: (i, 0)), out_shape=jax.ShapeDtypeStruct((r, c), F32),
        compiler_params=_params("parallel"))(x)


def ctx_cond_grad(parts, c_ctx, *, name):
    def body(p_ref, c_ref, o_ref):
        acc = p_ref[0]
        for k in range(1, N_CHIPS):
            acc = acc + p_ref[k]
        o_ref[...] = acc * _dsilu(c_ref[...])

    return pl.pallas_call(body, name=name, out_shape=jax.ShapeDtypeStruct(c_ctx.shape, F32))(parts, c_ctx)


PACK_ROWS = 256


def _tile_rows(shape):
    n = 1
    for dim in shape:
        n *= dim
    return -(-n // (SUBLANES * LANES)) * SUBLANES


def _pack(arrs):
    parts = []
    for a in arrs:
        flat = a.reshape(-1).astype(F32)
        rows = _tile_rows(a.shape)
        parts.append(jnp.pad(flat, (0, rows * LANES - flat.shape[0])).reshape(rows, LANES))
    total = sum(t.shape[0] for t in parts)
    parts.append(jnp.zeros((-total % PACK_ROWS, LANES), F32))
    return jnp.concatenate(parts, axis=0)


def _unpack(flat, shapes):
    lead = flat.shape[:-2]
    outs, r0 = [], 0
    for shape in shapes:
        rows = _tile_rows(shape)
        n = 1
        for dim in shape:
            n *= dim
        piece = flat[..., r0:r0 + rows, :].reshape(lead + (rows * LANES,))[..., :n]
        outs.append(piece.reshape(lead + tuple(shape)))
        r0 += rows
    return outs


WEIGHTS = ('c_ctx', 'mod_w', 'mod_b', 'norm_w', 'ab_w_in', 'ab_qkv_conv', 'ab_a_log', 'ab_dt_bias', 'ab_dn_norm',
           'ab_lru_conv_w', 'ab_lru_conv_b', 'ab_lru_w_r', 'ab_lru_b_r', 'ab_lru_w_i', 'ab_lru_b_i', 'ab_lru_lambda',
           'ab_w_out', 'sc_w_in', 'sc_conv', 'sc_w_out', 'final_norm_w')
BIG_WEIGHTS = ('ab_w_in', 'ab_w_out', 'sc_w_in', 'sc_w_out')


def kernel(x, c, ctx, c_ctx, mod_w, mod_b, norm_w, ab_w_in, ab_qkv_conv, ab_a_log, ab_dt_bias, ab_dn_norm, ab_lru_conv_w, ab_lru_conv_b, ab_lru_w_r, ab_lru_b_r, ab_lru_w_i, ab_lru_b_i, ab_lru_lambda, ab_w_out, sc_w_in, sc_conv, sc_w_out, final_norm_w, loss_target, m_c_ctx, m_mod_w, m_mod_b, m_norm_w, m_ab_w_in, m_ab_qkv_conv, m_ab_a_log, m_ab_dt_bias, m_ab_dn_norm, m_ab_lru_conv_w, m_ab_lru_conv_b, m_ab_lru_w_r, m_ab_lru_b_r, m_ab_lru_w_i, m_ab_lru_b_i, m_ab_lru_lambda, m_ab_w_out, m_sc_w_in, m_sc_conv, m_sc_w_out, m_final_norm_w, v_c_ctx, v_mod_w, v_mod_b, v_norm_w, v_ab_w_in, v_ab_qkv_conv, v_ab_a_log, v_ab_dt_bias, v_ab_dn_norm, v_ab_lru_conv_w, v_ab_lru_conv_b, v_ab_lru_w_r, v_ab_lru_b_r, v_ab_lru_w_i, v_ab_lru_b_i, v_ab_lru_lambda, v_ab_w_out, v_sc_w_in, v_sc_conv, v_sc_w_out, v_final_norm_w):
    given = dict(locals())
    weights = {n: given[n] for n in WEIGHTS}
    mom1 = {n: given['m_' + n] for n in WEIGHTS}
    mom2 = {n: given['v_' + n] for n in WEIGHTS}

    xi, yi, ci = lax.axis_index("x"), lax.axis_index("y"), lax.axis_index("c")
    chip = 2 * xi + yi
    dev = 2 * chip + ci
    x2d, ctx2d, target = x[0], ctx[0], loss_target[0]
    s, d = x2d.shape
    cl = ctx2d.shape[0]
    heads = ab_a_log.shape[-1]
    wdn = heads * LANES
    nb = ab_lru_w_r.shape[2]
    wl = nb * ab_lru_w_r.shape[3]
    sc = sc_w_in.shape[-1]
    ab_out = wdn + wl
    off_beta = 3 * wdn + wl
    ab_state = off_beta + 4 * heads
    ab_in = ab_state + wdn + wl
    ni = ab_in // N_CHIPS
    ns = mod_w.shape[-1]
    grid_rows = s // GRID_W

    def to_col_major(t):
        return t.reshape(grid_rows, GRID_W, t.shape[-1]).swapaxes(0, 1).reshape(s, t.shape[-1])

    def to_raster(t):
        return t.reshape(GRID_W, grid_rows, t.shape[-1]).swapaxes(0, 1).reshape(s, t.shape[-1])

    def from_chips(t):
        return jnp.moveaxis(t[0::N_CORES], 0, 1).reshape(t.shape[1], -1)

    def own_columns(t, width):
        return lax.dynamic_slice_in_dim(t, chip * width, width, axis=t.ndim - 1)

    small_shards = [c[0], ab_qkv_conv[0], ab_lru_conv_w[0], ab_lru_b_r[0], ab_lru_b_i[0], ab_lru_lambda[0], sc_conv[0]]
    gathered0 = all_gather_devices(_pack(small_shards), "ag_small_params")
    c_all, qkv_sh, lcw_sh, lbr_sh, lbi_sh, llam_sh, scv_sh = _unpack(gathered0, [t.shape for t in small_shards])
    qkv_conv, lru_conv_w, sc_conv_w = from_chips(qkv_sh), from_chips(lcw_sh), from_chips(scv_sh)
    lru_b_r, lru_b_i, lru_lam = from_chips(lbr_sh), from_chips(lbi_sh), from_chips(llam_sh)

    shards = [weights[n][0] for n in BIG_WEIGHTS]
    halves = [cast_rows(t, WIRE_DTYPE, name=f"cast_w{k}").reshape(N_CORES, t.shape[0] // N_CORES, t.shape[1])
              for k, t in enumerate(shards)]
    from_chips_half = chip_exchange(halves, gather=True, name="ag_w_chips")
    full = sibling_exchange(from_chips_half, pick=False, name="ag_w_cores")
    w_in_full = jnp.moveaxis(full[0].reshape(N_CHIPS, d, ni), 0, 1).reshape(d, ab_in)
    w_main = jnp.concatenate([w_in_full[:, :off_beta], w_in_full[:, ab_state:]], axis=1)
    w_ba = jnp.pad(w_in_full[:, off_beta:ab_state], ((0, 0), (0, LANES - 4 * heads)))
    w_ab_out = full[1].reshape(ab_out, d)
    w_sc_in = full[2].reshape(N_CHIPS, d, sc)
    w_sc_out = full[3].reshape(sc, d)

    cond = jnp.zeros((MOD_ROWS, d), F32).at[:N_DEV].set(c_all).at[N_DEV].set(c_ctx)
    mod_shard = mod_fwd(cond, mod_w, own_columns(mod_b, ns)[:, None, :], name="mod_fwd")
    gathered_mod = all_gather_devices(mod_shard.reshape(-1, LANES), "ag_mod")
    mod_all = jnp.moveaxis(gathered_mod[0::N_CORES].reshape(N_CHIPS, 2, MOD_ROWS, ns), 0, 2).reshape(2, MOD_ROWS, 3 * d)
    own_mod = lax.dynamic_index_in_dim(mod_all, dev, axis=1, keepdims=False)
    shift, scale, gate = own_mod[:, :d], own_mod[:, d:2 * d], own_mod[:, 2 * d:]
    shift_c, scale_c = mod_all[0, N_DEV, :d], mod_all[0, N_DEV, d:2 * d]

    def pair(a, b):
        return jnp.stack([a, b])[:, None, :]

    hn_all = norm_mod_fwd(x2d, ctx2d, norm_w[0:1], pair(scale[0], scale_c), pair(shift[0], shift_c), name="norm0_fwd")
    proj = mm_nn(hn_all, w_main, out_dtype=F32, name="proj0")
    ba = mm_nn(hn_all, w_ba, out_dtype=F32, name="proj0_ba")
    qkv = qkv_conv_fwd(proj, qkv_conv, s=s, cl=cl, heads=heads, name="qkv_conv_fwd")

    def pad_dh(t):
        return jnp.zeros((SUBLANES, LANES), F32).at[:2, :heads].set(t)

    a_log_t, dt_bias_t = pad_dh(ab_a_log[0]), pad_dh(ab_dt_bias[0])
    gate_dn0 = (3 * wdn + wl) // LANES
    gate_lru0 = gate_dn0 + wdn // LANES
    o_dn = dn_fwd(qkv, ba, a_log_t, dt_bias_t, s=s, cl=cl, heads=heads, name="dn_fwd")
    y_dn = dn_out_fwd(o_dn, proj, ab_dn_norm, heads=heads, gate_col0=gate_dn0, name="dn_out_fwd")
    lru_in = proj[:, 3 * wdn:3 * wdn + wl]
    xcm = jnp.concatenate([to_col_major(lru_in[:s]), lru_in[s:]], axis=0)
    lru_w = (lru_conv_w, ab_lru_conv_b, ab_lru_w_r[0], lru_b_r, ab_lru_w_i[0], lru_b_i, lru_lam)
    h_lru = to_raster(lru_fwd(xcm, *lru_w, s=s, cl=cl, name="lru_fwd"))
    y_lru = lru_gate_fwd(h_lru, proj, gate_col0=gate_lru0, name="lru_gate_fwd")
    y_ab = jnp.concatenate([y_dn, y_lru], axis=1)
    x1, yo0 = mm_nn(y_ab, w_ab_out, out_dtype=F32, name="out0", resid=x2d, gate=gate[0:1], aux_dtype=ACT_DTYPE)

    hn1 = norm_mod_fwd(x1, None, norm_w[1:2], scale[1][None, None, :], shift[1][None, None, :], name="norm1_fwd")
    tm1, tn1, tk1 = _tile(s, 1024, 16), _tile(sc, 1024, LANES), _tile(d, 512, LANES)
    p = matmul(hn1, w_sc_in, m=s, n=4 * sc, k=d, tm=tm1, tn=tn1, tk=tk1, dims=_NN, out_dtype=F32, name="proj1",
               a_spec=pl.BlockSpec((tm1, tk1), lambda i, j, kk: (i, kk)),
               b_spec=pl.BlockSpec((None, tk1, tn1), lambda i, j, kk: (j // (sc // tn1), kk, j % (sc // tn1))))
    y_sc = sc_mix_fwd(p, sc_conv_w, name="sc_mix_fwd")
    x2, yo1 = mm_nn(y_sc, w_sc_out, out_dtype=F32, name="out1", resid=x1, gate=gate[1:2], aux_dtype=ACT_DTYPE)
    loss_t, dx2, d_fnw = loss_head(x2, final_norm_w[None], target, name="loss_head")

    d_yo1, d_gate1 = gate_bwd(dx2, yo1, gate[1:2], name="gate1_bwd")
    d_ysc = mm_nt(d_yo1, w_sc_out, out_dtype=F32, name="out1_dx")
    dw_sc_out = mm_tn(y_sc, d_yo1, out_dtype=WIRE_DTYPE, name="out1_dw")
    d_p, d_sc_conv = sc_mix_bwd(p, d_ysc, sc_conv_w, name="sc_mix_bwd")
    tkp = _tile(sc, 512, LANES)
    tnd = _tile(d, 1024, LANES)
    d_hn1 = matmul(d_p, w_sc_in, m=s, n=d, k=4 * sc, tm=tm1, tn=tnd, tk=tkp, dims=_NT, out_dtype=F32, name="proj1_dx",
                   a_spec=pl.BlockSpec((tm1, tkp), lambda i, j, kk: (i, kk)),
                   b_spec=pl.BlockSpec((None, tnd, tkp), lambda i, j, kk: (kk // (sc // tkp), j, kk % (sc // tkp))))
    tks = _tile(s, 512, 16)
    dw_sc_in = matmul(hn1, d_p, m=d, n=4 * sc, k=s, tm=tnd, tn=tn1, tk=tks, dims=_TN, out_dtype=WIRE_DTYPE,
                      name="proj1_dw", a_spec=pl.BlockSpec((tks, tnd), lambda i, j, kk: (kk, i)),
                      b_spec=pl.BlockSpec((tks, tn1), lambda i, j, kk: (kk, j)),
                      out_spec=pl.BlockSpec((None, tnd, tn1), lambda i, j, kk: (j // (sc // tn1), i, j % (sc // tn1))),
                      out_shape=(N_CHIPS, d, sc))
    dx1, d_nw1, d_scale1, d_shift1 = norm_mod_bwd(x1, norm_w[1:2], scale[1:2], d_hn1, row0=0, resid=dx2, init=None,
                                                  name="norm1_bwd")

    d_yo0, d_gate0 = gate_bwd(dx1, yo0, gate[0:1], name="gate0_bwd")
    d_y = mm_nt(d_yo0, w_ab_out, out_dtype=F32, name="out0_dx")
    dw_ab_out = mm_tn(y_ab, d_yo0, out_dtype=WIRE_DTYPE, name="out0_dw")
    d_o, d_gate_dn, d_dn_norm = dn_out_bwd(o_dn, proj, d_y, ab_dn_norm, heads=heads, gate_col0=gate_dn0, name="dn_out_bwd")
    d_qkv, d_ba, d_alog_t, d_dtb_t = dn_bwd(qkv, ba, d_o, a_log_t, dt_bias_t, s=s, cl=cl, heads=heads, name="dn_bwd")
    d_qkv_raw, d_qkv_conv = qkv_conv_bwd(proj, qkv_conv, d_qkv, s=s, cl=cl, heads=heads, name="qkv_conv_bwd")
    d_h, d_gate_lru = lru_gate_bwd(h_lru, proj, d_y, gate_col0=gate_lru0, dy_col0=wdn // LANES, name="lru_gate_bwd")
    (d_xcm, d_lcw, d_lcb, d_wr, d_br, d_wi, d_bi, d_lam) = lru_bwd(xcm, to_col_major(d_h), *lru_w, s=s, cl=cl,
                                                                  name="lru_bwd")
    d_lru_in = jnp.concatenate([to_raster(d_xcm[:s]), d_xcm[s:]], axis=0)
    ctx_zeros = jnp.zeros((cl, wdn + wl), ACT_DTYPE)
    d_gates = jnp.concatenate([jnp.concatenate([d_gate_dn, d_gate_lru], axis=1), ctx_zeros], axis=0)
    d_proj = jnp.concatenate([d_qkv_raw, d_lru_in, d_gates], axis=1)
    d_hn_ba = mm_nt(d_ba, w_ba, out_dtype=F32, name="proj0_ba_dx")
    d_hn_all = mm_nt(d_proj, w_main, out_dtype=F32, name="proj0_dx", resid=d_hn_ba)
    dw_main = mm_tn(hn_all, d_proj, out_dtype=WIRE_DTYPE, name="proj0_dw")
    dw_ba = mm_tn(hn_all, d_ba, out_dtype=WIRE_DTYPE, name="proj0_ba_dw")
    _, d_nw0c, d_scale_c, d_shift_c = norm_mod_bwd(ctx2d, norm_w[0:1], scale_c[None], d_hn_all, row0=s, resid=None,
                                                   init=None, name="norm0_bwd_ctx")
    dx, d_nw0, d_scale0, d_shift0 = norm_mod_bwd(x2d, norm_w[0:1], scale[0:1], d_hn_all, row0=0, resid=dx1, init=d_nw0c,
                                                 name="norm0_bwd")

    dw_in_full = jnp.concatenate([dw_main[:, :off_beta], dw_ba[:, :4 * heads], dw_main[:, off_beta:]], axis=1)
    by_chip = [jnp.moveaxis(dw_in_full.reshape(d, N_CHIPS, ni), 1, 0), dw_ab_out, dw_sc_in, dw_sc_out]
    slot_rows = [d, ab_out // N_CHIPS, d, sc // N_CHIPS]
    by_chip = [t.reshape(N_CHIPS, N_CORES, r // N_CORES, t.shape[-1]) for t, r in zip(by_chip, slot_rows)]
    from_sibling = sibling_exchange(by_chip, pick=True, name="rs_cores")
    core = ci.astype(jnp.int32).reshape(1)
    halves_sum = [add_sibling_half(core, a, b, name=f"rs_add{k}") for k, (a, b) in enumerate(zip(by_chip, from_sibling))]
    from_chips_sum = chip_exchange(halves_sum, gather=False, name="rs_chips")
    reduced = [sum_slots(t, name=f"rs_sum{k}")[None] for k, t in enumerate(from_chips_sum)]
    both_halves = sibling_exchange(reduced, pick=False, name="rs_back")
    grads = {n: t.reshape(weights[n].shape) for n, t in zip(BIG_WEIGHTS, both_halves)}

    d_mod_own = jnp.stack([jnp.concatenate([d_shift0[0], d_scale0[0], d_gate0[0]]),
                           jnp.concatenate([d_shift1[0], d_scale1[0], d_gate1[0]])])
    d_mod_ctx = jnp.concatenate([d_shift_c[0], d_scale_c[0], jnp.zeros((d,), F32)])
    summable = [
        loss_t[0, 0:1], d_mod_own.at[0].add(d_mod_ctx), d_mod_ctx, jnp.concatenate([d_nw0, d_nw1], axis=0), d_qkv_conv,
        d_alog_t[:, :2, 0].T, d_dtb_t[:, :2, 0].T, d_dn_norm[0], d_lcw, d_lcb[0], d_wr, d_br, d_wi, d_bi, d_lam,
        d_sc_conv, d_fnw[0]]
    sum_shapes = [t.shape for t in summable]
    gathered1 = all_gather_devices(_pack(summable + [d_mod_own]), "ag_small_grads")
    totals = _unpack(sum_slots(gathered1, name="sum_small_grads"), sum_shapes)
    (loss_sum, g_mod_b, d_mod_ctx_sum, g_norm_w, g_qkv_conv, g_a_log, g_dt_bias, g_dn_norm, g_lcw, g_lcb, g_wr, g_br,
     g_wi, g_bi, g_lam, g_sc_conv, g_fnw) = totals
    d_mod_rows = _unpack(gathered1, sum_shapes + [d_mod_own.shape])[-1]

    d_mod_all = jnp.zeros((2, MOD_ROWS, 3 * d), F32).at[:, :N_DEV].set(jnp.moveaxis(d_mod_rows, 0, 1))
    d_mod_all = d_mod_all.at[0, N_DEV].set(d_mod_ctx_sum)
    g_mod_w, dl_mod_w, nm_mod_w, nv_mod_w, d_silu = mod_adam(cond, own_columns(d_mod_all, ns), mod_w, m_mod_w, v_mod_w,
                                                             name="mod_adam")
    gathered2 = all_gather_devices(d_silu[0, N_DEV].reshape(-1, LANES), "ag_ctx_cond")
    g_c_ctx = ctx_cond_grad(gathered2[0::N_CORES], c_ctx.reshape(-1, LANES), name="ctx_cond_grad").reshape(d)

    grads.update({
        'c_ctx': g_c_ctx, 'mod_w': g_mod_w, 'mod_b': g_mod_b, 'norm_w': g_norm_w,
        'ab_qkv_conv': own_columns(g_qkv_conv, qkv_conv.shape[1] // N_CHIPS), 'ab_a_log': g_a_log, 'ab_dt_bias': g_dt_bias,
        'ab_dn_norm': g_dn_norm, 'ab_lru_conv_w': own_columns(g_lcw, wl // N_CHIPS), 'ab_lru_conv_b': g_lcb,
        'ab_lru_w_r': g_wr, 'ab_lru_b_r': own_columns(g_br, wl // N_CHIPS), 'ab_lru_w_i': g_wi,
        'ab_lru_b_i': own_columns(g_bi, wl // N_CHIPS), 'ab_lru_lambda': own_columns(g_lam, wl // N_CHIPS),
        'sc_conv': own_columns(g_sc_conv, sc // N_CHIPS), 'final_norm_w': g_fnw})
    grads = {n: grads[n].reshape(weights[n].shape) for n in WEIGHTS}

    delta, new_m, new_v = {'mod_w': dl_mod_w}, {'mod_w': nm_mod_w}, {'mod_w': nv_mod_w}
    for k, n in enumerate(BIG_WEIGHTS):
        as2d = lambda t: t.reshape(-1, t.shape[-1])
        upd = adam_update(as2d(weights[n]), as2d(grads[n]), as2d(mom1[n]), as2d(mom2[n]), name=f"adam_big{k}")
        delta[n], new_m[n], new_v[n] = (t.reshape(weights[n].shape) for t in upd)
    small = [n for n in WEIGHTS if n not in BIG_WEIGHTS and n != 'mod_w']
    small_shapes = [weights[n].shape for n in small]
    upd = adam_update(*[_pack([src[n] for n in small]) for src in (weights, grads, mom1, mom2)], name="adam_small")
    for out, flat in zip((delta, new_m, new_v), upd):
        out.update(dict(zip(small, _unpack(flat, small_shapes))))

    return (loss_sum[0], dx[None], *[grads[n] for n in WEIGHTS], *[delta[n] for n in WEIGHTS],
            *[new_m[n] for n in WEIGHTS], *[new_v[n] for n in WEIGHTS])
```

```python
import functools

import jax
import jax.numpy as jnp
from jax import lax
from jax.experimental import pallas as pl
from jax.experimental.pallas import tpu as pltpu

F32 = jnp.float32
BF16 = jnp.bfloat16
MXU_DTYPE = BF16
ACT_DTYPE = BF16
WIRE_DTYPE = BF16

EPS = 1e-6
GRID_W = 64
CHUNK = 64
DN_CONV_OFFSETS = (-2, -1, 0, 1)
SC_CONV_OFFSETS = (-1, 0, 1)
LRU_C = 8.0
ADAM_LR, ADAM_B1, ADAM_B2, ADAM_EPS, ADAM_WD, ADAM_STEP = 0.001, 0.9, 0.999, 1e-08, 0.01, 10

LANES = 128
SUBLANES = 8
N_CHIPS, N_CORES = 4, 2
N_DEV = N_CHIPS * N_CORES
INV_PRECISION = None
INV_BLOCK = 16

_MESH = pl.DeviceIdType.MESH
_ANY = pl.BlockSpec(memory_space=pl.ANY)
_NN = (((1,), (0,)), ((), ()))
_NT = (((1,), (1,)), ((), ()))
_TN = (((0,), (0,)), ((), ()))


def _tile(n, cap, mult):
    best = None
    for t in range(mult, min(n, cap) + 1, mult):
        if n % t == 0:
            best = t
    return n if best is None else best


def _iota(shape, dim):
    return lax.broadcasted_iota(jnp.int32, shape, dim)


def _dot(a, b, dims):
    return lax.dot_general(a.astype(MXU_DTYPE), b.astype(MXU_DTYPE), dims, preferred_element_type=F32)


def _silu(x):
    return x * jax.nn.sigmoid(x)


def _dsilu(x):
    s = jax.nn.sigmoid(x)
    return s * (1.0 + x * (1.0 - s))


V7X_VMEM_BYTES = 64 * 1024 * 1024
BIG_KERNEL_VMEM = V7X_VMEM_BYTES * 15 // 16
MATMUL_TK = 4096
MATMUL_TK_TOKENS = 8192
MATMUL_VMEM = V7X_VMEM_BYTES * 7 // 8


def _params(*sem, vmem=None):
    return pltpu.CompilerParams(dimension_semantics=sem, vmem_limit_bytes=vmem)


def _place():
    return lax.axis_index("x"), lax.axis_index("y"), lax.axis_index("c")


def all_gather_devices(block, name):
    def body(x_ref, out_ref, send_sems, recv_sems, local_sem):
        x, y, c = _place()
        me, sibling = (x, y, c), (x, y, 1 - c)
        chips = [(1 - x, y), (x, 1 - y), (1 - x, 1 - y)]

        def slot(px, py, pc):
            return out_ref.at[4 * px + 2 * py + pc]

        def copy(k, block_of, to, src=None):
            return pltpu.make_async_remote_copy(
                src_ref=slot(*block_of) if src is None else src, dst_ref=slot(*block_of),
                send_sem=send_sems.at[k], recv_sem=recv_sems.at[k], device_id=to, device_id_type=_MESH)

        mine = pltpu.make_async_copy(x_ref, slot(*me), local_sem)
        mine.start()
        first = [copy(0, me, sibling, src=x_ref)]
        first += [copy(1 + j, me, (*chip, c), src=x_ref) for j, chip in enumerate(chips)]
        for cp in first:
            cp.start()
        passed = [copy(4 + j, (*chip, c), sibling) for j, chip in enumerate(chips)]
        for j, chip in enumerate(chips):
            copy(1 + j, (*chip, c), me).wait_recv()
            passed[j].start()
        copy(0, sibling, me).wait_recv()
        for j, chip in enumerate(chips):
            copy(4 + j, (*chip, 1 - c), me).wait_recv()
        for cp in first + passed:
            cp.wait_send()
        mine.wait()

    return pl.pallas_call(
        body, name=name,
        out_shape=jax.ShapeDtypeStruct((N_DEV,) + block.shape, block.dtype),
        in_specs=[_ANY], out_specs=_ANY,
        scratch_shapes=[pltpu.SemaphoreType.DMA((7,)), pltpu.SemaphoreType.DMA((7,)), pltpu.SemaphoreType.DMA],
    )(block)


def chip_exchange(srcs, *, gather, name):
    n = len(srcs)

    def body(*refs):
        src, out = refs[:n], refs[n:2 * n]
        send_sems, recv_sems, local_sems = refs[2 * n:]
        x, y, c = _place()
        my = 2 * x + y
        peers = [(1 - x, y), (x, 1 - y), (1 - x, 1 - y)]
        copies = []
        for k in range(n):
            own = src[k].at[c] if gather else src[k].at[my]
            loc = pltpu.make_async_copy(own, out[k].at[my], local_sems.at[k])
            loc.start()
            copies.append(loc)
            for j, (px, py) in enumerate(peers):
                cp = pltpu.make_async_remote_copy(
                    src_ref=own if gather else src[k].at[2 * px + py], dst_ref=out[k].at[my],
                    send_sem=send_sems.at[k, j], recv_sem=recv_sems.at[k, j],
                    device_id=(px, py, c), device_id_type=_MESH)
                cp.start()
                copies.append(cp)
        for cp in copies:
            cp.wait()

    return pl.pallas_call(
        body, name=name,
        out_shape=[jax.ShapeDtypeStruct((N_CHIPS,) + s.shape[1:], s.dtype) for s in srcs],
        in_specs=[_ANY] * n, out_specs=[_ANY] * n,
        scratch_shapes=[pltpu.SemaphoreType.DMA((n, 3)), pltpu.SemaphoreType.DMA((n, 3)),
                        pltpu.SemaphoreType.DMA((n,))],
    )(*srcs)


def sibling_exchange(srcs, *, pick, name):
    n = len(srcs)

    def body(*refs):
        src, out = refs[:n], refs[n:2 * n]
        send_sems, recv_sems = refs[2 * n:]
        x, y, c = _place()
        copies = []
        for k in range(n):
            s_ref = src[k].at[pl.ds(0, src[k].shape[0]), 1 - c] if pick else src[k]
            cp = pltpu.make_async_remote_copy(
                src_ref=s_ref, dst_ref=out[k], send_sem=send_sems.at[k], recv_sem=recv_sems.at[k],
                device_id=(x, y, 1 - c), device_id_type=_MESH)
            cp.start()
            copies.append(cp)
        for cp in copies:
            cp.wait()

    outs = [jax.ShapeDtypeStruct(s.shape[:1] + s.shape[2:] if pick else s.shape, s.dtype) for s in srcs]
    return pl.pallas_call(
        body, name=name, out_shape=outs, in_specs=[_ANY] * n, out_specs=[_ANY] * n,
        scratch_shapes=[pltpu.SemaphoreType.DMA((n,)), pltpu.SemaphoreType.DMA((n,))],
    )(*srcs)


def core_halves(core, mine, other):
    return jnp.where(core == 0, jnp.stack([mine, other], axis=1), jnp.stack([other, mine], axis=1))


def matmul(a, b, *, m, n, k, tm, tn, tk, a_spec, b_spec, dims, out_dtype, name,
           out_spec=None, out_shape=None, resid=None, gate=None, aux_dtype=None):
    nk = k // tk
    o_spec = out_spec or pl.BlockSpec((tm, tn), lambda i, j, kk: (i, j))
    o_shape = out_shape or (m, n)

    def body(*refs):
        a_ref, b_ref = refs[0], refs[1]
        pos = 2
        r_ref = g_ref = aux_ref = None
        if resid is not None:
            r_ref, pos = refs[pos], pos + 1
        if gate is not None:
            g_ref, pos = refs[pos], pos + 1
        o_ref, pos = refs[pos], pos + 1
        if aux_dtype is not None:
            aux_ref, pos = refs[pos], pos + 1
        prod = _dot(a_ref[...], b_ref[...], dims)

        def finish(y):
            if aux_ref is not None:
                aux_ref[...] = y.astype(aux_dtype)
            if g_ref is not None:
                y = y * g_ref[...]
            if r_ref is not None:
                y = y + r_ref[...]
            o_ref[...] = y.astype(out_dtype)

        if nk == 1:
            finish(prod)
            return
        acc = refs[pos]
        kk = pl.program_id(2)

        @pl.when(kk == 0)
        def _():
            acc[...] = prod

        @pl.when((kk > 0) & (kk < nk - 1))
        def _():
            acc[...] += prod

        @pl.when(kk == nk - 1)
        def _():
            finish(acc[...] + prod)

    ins, in_specs = [a, b], [a_spec, b_spec]
    if resid is not None:
        ins.append(resid)
        in_specs.append(pl.BlockSpec((tm, tn), lambda i, j, kk: (i, j)))
    if gate is not None:
        ins.append(gate)
        in_specs.append(pl.BlockSpec((1, tn), lambda i, j, kk: (0, j)))
    outs, out_specs = [jax.ShapeDtypeStruct(o_shape, out_dtype)], [o_spec]
    if aux_dtype is not None:
        outs.append(jax.ShapeDtypeStruct((m, n), aux_dtype))
        out_specs.append(pl.BlockSpec((tm, tn), lambda i, j, kk: (i, j)))
    res = pl.pallas_call(
        body, name=name, grid=(m // tm, n // tn, nk), in_specs=in_specs, out_specs=out_specs,
        out_shape=outs, scratch_shapes=[pltpu.VMEM((tm, tn), F32)] if nk > 1 else [],
        compiler_params=_params("parallel", "parallel", "arbitrary", vmem=MATMUL_VMEM),
    )(*ins)
    return res if aux_dtype is not None else res[0]


def mm_nn(a, b, *, out_dtype, name, tm_cap=1088, tn_cap=1024, tk_cap=MATMUL_TK, **kw):
    m, k = a.shape
    n = b.shape[1]
    tm, tn, tk = _tile(m, tm_cap, 16), _tile(n, tn_cap, LANES), _tile(k, tk_cap, LANES)
    return matmul(a, b, m=m, n=n, k=k, tm=tm, tn=tn, tk=tk, dims=_NN, out_dtype=out_dtype, name=name,
                  a_spec=pl.BlockSpec((tm, tk), lambda i, j, kk: (i, kk)),
                  b_spec=pl.BlockSpec((tk, tn), lambda i, j, kk: (kk, j)), **kw)


def mm_nt(a, b, *, out_dtype, name, tm_cap=1088, tn_cap=1024, tk_cap=MATMUL_TK // 2, **kw):
    m, k = a.shape
    n = b.shape[0]
    tm, tn, tk = _tile(m, tm_cap, 16), _tile(n, tn_cap, LANES), _tile(k, tk_cap, LANES)
    return matmul(a, b, m=m, n=n, k=k, tm=tm, tn=tn, tk=tk, dims=_NT, out_dtype=out_dtype, name=name,
                  a_spec=pl.BlockSpec((tm, tk), lambda i, j, kk: (i, kk)),
                  b_spec=pl.BlockSpec((tn, tk), lambda i, j, kk: (j, kk)), **kw)


def mm_tn(a, b, *, out_dtype, name, tm_cap=1024, tn_cap=1024, tk_cap=MATMUL_TK_TOKENS, **kw):
    k, m = a.shape
    n = b.shape[1]
    tm, tn, tk = _tile(m, tm_cap, LANES), _tile(n, tn_cap, LANES), _tile(k, tk_cap, 16)
    return matmul(a, b, m=m, n=n, k=k, tm=tm, tn=tn, tk=tk, dims=_TN, out_dtype=out_dtype, name=name,
                  a_spec=pl.BlockSpec((tk, tm), lambda i, j, kk: (kk, i)),
                  b_spec=pl.BlockSpec((tk, tn), lambda i, j, kk: (kk, j)), **kw)


def _rms(x):
    r = lax.rsqrt(jnp.mean(x * x, axis=-1, keepdims=True) + EPS)
    return x * r, r


def norm_mod_fwd(x, ctx, nw, scale2, shift2, *, name):
    s, d = x.shape
    cl = 0 if ctx is None else ctx.shape[0]
    tr = _tile(s if ctx is None else cl, 256, 16)
    n_lat = s // tr

    def body(*refs):
        if ctx is None:
            x_ref, nw_ref, sc_ref, sh_ref, o_ref = refs
            v = x_ref[...]
        else:
            x_ref, c_ref, nw_ref, sc_ref, sh_ref, o_ref = refs
            v = jnp.where(pl.program_id(0) < n_lat, x_ref[...], c_ref[...])
        y = _rms(v)[0] * nw_ref[...]
        o_ref[...] = (y * (1.0 + sc_ref[...]) + sh_ref[...]).astype(o_ref.dtype)

    sel = pl.BlockSpec((None, 1, d), lambda i: (i // n_lat, 0, 0))
    ins = [x] if ctx is None else [x, ctx]
    specs = [pl.BlockSpec((tr, d), lambda i: (jnp.minimum(i, n_lat - 1), 0))]
    if ctx is not None:
        specs.append(pl.BlockSpec((tr, d), lambda i: (jnp.maximum(i - n_lat, 0), 0)))
    return pl.pallas_call(
        body, name=name, grid=((s + cl) // tr,),
        in_specs=specs + [pl.BlockSpec((1, d), lambda i: (0, 0)), sel, sel],
        out_specs=pl.BlockSpec((tr, d), lambda i: (i, 0)),
        out_shape=jax.ShapeDtypeStruct((s + cl, d), ACT_DTYPE),
        compiler_params=_params("parallel"),
    )(*ins, nw, scale2, shift2)


def norm_mod_bwd(x, nw, scale, d_hn, *, row0, resid, init, name):
    r, d = x.shape
    tr = _tile(r, 256, 16)
    off = row0 // tr
    want_dx = resid is not None

    def body(*refs):
        x_ref, nw_ref, sc_ref, dh_ref = refs[:4]
        pos = 4
        res_ref = init_ref = dx_ref = None
        if want_dx:
            res_ref, pos = refs[pos], pos + 1
        if init is not None:
            init_ref, pos = refs[pos], pos + 1
        if want_dx:
            dx_ref, pos = refs[pos], pos + 1
        dnw_ref, dsc_ref, dsh_ref = refs[pos:pos + 3]
        i = pl.program_id(0)

        @pl.when(i == 0)
        def _():
            dnw_ref[...] = jnp.zeros_like(dnw_ref) if init_ref is None else init_ref[...]
            dsc_ref[...] = jnp.zeros_like(dsc_ref)
            dsh_ref[...] = jnp.zeros_like(dsh_ref)

        nrm, rs = _rms(x_ref[...])
        w = nw_ref[...]
        dh = dh_ref[...].astype(F32)
        dsh_ref[...] += jnp.sum(dh, axis=0, keepdims=True)
        dsc_ref[...] += jnp.sum(dh * (nrm * w), axis=0, keepdims=True)
        dy = dh * (1.0 + sc_ref[...])
        dnw_ref[...] += jnp.sum(dy * nrm, axis=0, keepdims=True)
        if want_dx:
            dn = dy * w
            dx = rs * (dn - nrm * jnp.mean(dn * nrm, axis=-1, keepdims=True))
            dx_ref[...] = dx + res_ref[...]

    row = pl.BlockSpec((tr, d), lambda i: (i, 0))
    vec = pl.BlockSpec((1, d), lambda i: (0, 0))
    ins, specs = [x, nw, scale, d_hn], [row, vec, vec, pl.BlockSpec((tr, d), lambda i: (i + off, 0))]
    if want_dx:
        ins.append(resid)
        specs.append(row)
    if init is not None:
        ins.append(init)
        specs.append(vec)
    vshape = jax.ShapeDtypeStruct((1, d), F32)
    outs, ospecs = [vshape] * 3, [vec] * 3
    if want_dx:
        outs, ospecs = [jax.ShapeDtypeStruct((r, d), F32)] + outs, [row] + ospecs
    res = pl.pallas_call(body, name=name, grid=(r // tr,), in_specs=specs, out_specs=ospecs, out_shape=outs,
                         compiler_params=_params("arbitrary"))(*ins)
    return tuple(res) if want_dx else (None,) + tuple(res)


def gate_bwd(dx, yo, gate, *, name):
    s, d = dx.shape
    tr = _tile(s, 256, 16)

    def body(dx_ref, yo_ref, g_ref, dyo_ref, dg_ref):
        @pl.when(pl.program_id(0) == 0)
        def _():
            dg_ref[...] = jnp.zeros_like(dg_ref)

        g = dx_ref[...]
        dg_ref[...] += jnp.sum(g * yo_ref[...].astype(F32), axis=0, keepdims=True)
        dyo_ref[...] = (g * g_ref[...]).astype(dyo_ref.dtype)

    row = pl.BlockSpec((tr, d), lambda i: (i, 0))
    vec = pl.BlockSpec((1, d), lambda i: (0, 0))
    return pl.pallas_call(
        body, name=name, grid=(s // tr,), in_specs=[row, row, vec], out_specs=[row, vec],
        out_shape=[jax.ShapeDtypeStruct((s, d), ACT_DTYPE), jax.ShapeDtypeStruct((1, d), F32)],
        compiler_params=_params("arbitrary"))(dx, yo, gate)


def loss_head(x, fw, target, *, name):
    s, d = x.shape
    tr = _tile(s, 256, 16)

    def body(x_ref, w_ref, t_ref, loss_ref, dx_ref, dw_ref):
        @pl.when(pl.program_id(0) == 0)
        def _():
            loss_ref[...] = jnp.zeros_like(loss_ref)
            dw_ref[...] = jnp.zeros_like(dw_ref)

        nrm, rs = _rms(x_ref[...])
        w = w_ref[...]
        err = nrm * w - t_ref[...]
        loss_ref[...] += 0.5 * jnp.sum(jnp.mean(err * err, axis=-1, keepdims=True))
        d_out = err * (1.0 / d)
        dw_ref[...] += jnp.sum(d_out * nrm, axis=0, keepdims=True)
        dn = d_out * w
        dx_ref[...] = rs * (dn - nrm * jnp.mean(dn * nrm, axis=-1, keepdims=True))

    row = pl.BlockSpec((tr, d), lambda i: (i, 0))
    vec = pl.BlockSpec((1, d), lambda i: (0, 0))
    return pl.pallas_call(
        body, name=name, grid=(s // tr,), in_specs=[row, vec, row],
        out_specs=[pl.BlockSpec((SUBLANES, LANES), lambda i: (0, 0)), row, vec],
        out_shape=[jax.ShapeDtypeStruct((SUBLANES, LANES), F32), jax.ShapeDtypeStruct((s, d), F32),
                   jax.ShapeDtypeStruct((1, d), F32)],
        compiler_params=_params("arbitrary"))(x, fw, target)


def _segments(rows, seg_a, seg_b):
    t = _iota((rows, 1), 0)
    if seg_b == 0:
        return t % seg_a, seg_a
    return jnp.where(t < seg_a, t, t - seg_a), jnp.where(t < seg_a, seg_a, seg_b)


def _shift(x, o, seg):
    if o == 0:
        return x
    pos, length = _segments(x.shape[0], *seg)
    y = pltpu.roll(x, (-o) % x.shape[0], 0)
    return jnp.where((pos + o >= 0) & (pos + o < length), y, 0.0)


def _conv(x, w, offsets, seg):
    acc = None
    for j, o in enumerate(offsets):
        term = w[j:j + 1, :] * _shift(x, o, seg)
        acc = term if acc is None else acc + term
    return acc


def _conv_bwd(x, w, dy, offsets, seg):
    dx = None
    dw = jnp.zeros(w.shape, F32)
    row = _iota(w.shape, 0)
    for j, o in enumerate(offsets):
        term = w[j:j + 1, :] * _shift(dy, -o, seg)
        dx = term if dx is None else dx + term
        dwj = jnp.sum(dy * _shift(x, o, seg), axis=0, keepdims=True)
        dw = dw + jnp.where(row == j, dwj, 0.0)
    return dx, dw


def _qkv_post(y, group, scale):
    a = _silu(y)
    n = a * lax.rsqrt(jnp.sum(a * a, axis=-1, keepdims=True) + EPS)
    return jnp.where(group == 0, n * scale, jnp.where(group == 1, n, a))


def qkv_conv_fwd(proj, conv_w, *, s, cl, heads, name):
    m = s + cl
    dh = LANES
    scale = dh ** -0.5

    def body(x_ref, w_ref, o_ref):
        group = pl.program_id(0) // heads
        y = _conv(x_ref[...], w_ref[...], DN_CONV_OFFSETS, (s, cl))
        o_ref[...] = _qkv_post(y, group, scale).astype(o_ref.dtype)

    return pl.pallas_call(
        body, name=name, grid=(3 * heads,),
        in_specs=[pl.BlockSpec((m, dh), lambda j: (0, j)), pl.BlockSpec((len(DN_CONV_OFFSETS), dh), lambda j: (0, j))],
        out_specs=pl.BlockSpec((m, dh), lambda j: (0, j)),
        out_shape=jax.ShapeDtypeStruct((m, 3 * heads * dh), ACT_DTYPE),
        compiler_params=_params("parallel"))(proj, conv_w)


def qkv_conv_bwd(proj, conv_w, dqkv, *, s, cl, heads, name):
    m = s + cl
    dh = LANES
    scale = dh ** -0.5
    kk = len(DN_CONV_OFFSETS)

    def body(x_ref, w_ref, d_ref, dx_ref, dw_ref):
        group = pl.program_id(0) // heads
        x, w = x_ref[...], w_ref[...]
        y = _conv(x, w, DN_CONV_OFFSETS, (s, cl))
        a = _silu(y)
        rn = lax.rsqrt(jnp.sum(a * a, axis=-1, keepdims=True) + EPS)
        n = a * rn
        dout = d_ref[...] * jnp.where(group == 0, scale, 1.0)
        da_norm = rn * (dout - n * jnp.sum(dout * n, axis=-1, keepdims=True))
        dy = jnp.where(group == 2, dout, da_norm) * _dsilu(y)
        dx, dw = _conv_bwd(x, w, dy, DN_CONV_OFFSETS, (s, cl))
        dx_ref[...] = dx.astype(dx_ref.dtype)
        dw_ref[...] = dw

    col = pl.BlockSpec((m, dh), lambda j: (0, j))
    wspec = pl.BlockSpec((kk, dh), lambda j: (0, j))
    return pl.pallas_call(
        body, name=name, grid=(3 * heads,),
        in_specs=[col, wspec, pl.BlockSpec((None, m, dh), lambda j: (j // heads, 0, j % heads))],
        out_specs=[col, wspec],
        out_shape=[jax.ShapeDtypeStruct((m, 3 * heads * dh), ACT_DTYPE),
                   jax.ShapeDtypeStruct((kk, 3 * heads * dh), F32)],
        compiler_params=_params("parallel"))(proj, conv_w, dqkv)


def _scan_masks(d):
    t, s = _iota((CHUNK, CHUNK), 0), _iota((CHUNK, CHUNK), 1)
    return ((s <= t), (s < t)) if d == 0 else ((s >= t), (s > t))


def _bmm(spec, a, b, precision=None):
    if precision is None:
        a, b = a.astype(MXU_DTYPE), b.astype(MXU_DTYPE)
    return jnp.einsum(spec, a, b, precision=precision, preferred_element_type=F32)


def _unit_tri_inverse(a):
    mm = functools.partial(_bmm, 'nts,nsr->ntr', precision=INV_PRECISION)
    row, col = _iota((CHUNK, CHUNK), 0), _iota((CHUNK, CHUNK), 1)
    eye = (row == col).astype(F32)
    dg = jnp.where(row // INV_BLOCK == col // INV_BLOCK, a, 0.0)
    off = a - dg
    p = eye - dg
    pw = dg
    for _ in range(3):
        pw = mm(pw, pw)
        p = p + mm(p, pw)
    n = mm(p, off)
    r = eye - n
    return mm(r + mm(r, mm(n, n)), p)


def _dn_intra(q, k, v, beta_b, gc_b, d):
    dh = q.shape[-1]
    incl, strict = _scan_masks(d)
    gc64 = gc_b[:, :, :CHUNK]
    diff = gc64 - jnp.swapaxes(gc64, 1, 2)
    decay = jnp.where(incl, jnp.exp(jnp.where(incl, diff, 0.0)), 0.0)
    qk_kk = _bmm('ntd,nsd->nts', jnp.concatenate([q, k], axis=1), k)
    qk, kk = qk_kk[:, :CHUNK], qk_kk[:, CHUNK:]
    a = jnp.where(strict, beta_b[:, :, :CHUNK] * kk * decay, 0.0)
    tinv = _unit_tri_inverse(a)
    rhs = jnp.concatenate([beta_b * jnp.exp(gc_b) * k, beta_b * v], axis=-1)
    wu = _bmm('nts,nsd->ntd', tinv, rhs, INV_PRECISION)
    w, u = wu[:, :, :dh], wu[:, :, dh:]
    last = CHUNK - 1 if d == 0 else 0
    gl = gc_b[:, last:last + 1, :]
    ke = k * jnp.exp(gl - gc_b)
    ge = jnp.exp(gl)
    return w, u, ke, ge, qk * decay, q * jnp.exp(gc_b)


def _dn_step(s, w, u, ke, ge, aqk, qg):
    ws_qs = _dot(jnp.concatenate([w, qg], axis=0), s, _NN)
    u2 = u - ws_qs[:CHUNK]
    s_new = ge * s + _dot(ke, u2, _TN)
    o = ws_qs[CHUNK:] + _dot(aqk, u2, _NN)
    return s_new, o


def _chunk_cumsum(x, d):
    rows = x.shape[0]
    pos = _iota((rows, 1), 0) % CHUNK
    step = 1
    while step < CHUNK:
        if d == 0:
            x = x + jnp.where(pos >= step, pltpu.roll(x, step, 0), 0.0)
        else:
            x = x + jnp.where(pos < CHUNK - step, pltpu.roll(x, rows - step, 0), 0.0)
        step *= 2
    return x


def _pick_lane(x, j):
    return jnp.sum(jnp.where(_iota(x.shape, 1) == j, x, 0.0), axis=1, keepdims=True)


def _dn_gates(ba, a_log, dt_bias, d, h, heads):
    braw = _pick_lane(ba, d * heads + h)
    araw = _pick_lane(ba, (2 + d) * heads + h)
    a_neg = -jnp.exp(_pick_lane(a_log[d:d + 1, :], h))
    pre = araw + _pick_lane(dt_bias[d:d + 1, :], h)
    return jax.nn.sigmoid(braw), a_neg * jax.nn.softplus(pre), pre, a_neg


_DN_SUB_FWD = 16
_DN_SUB_BWD = 8


def _for_sub_batches(s, cl, fn, sub=_DN_SUB_FWD):
    for base, total in ((0, s), (s, cl)):
        nch = min(sub, total // CHUNK)
        rows_per = nch * CHUNK
        count = total // rows_per

        def run(i, carry, base=base, nch=nch, rows_per=rows_per):
            row0 = pl.multiple_of(base + i * rows_per, rows_per)
            ge0 = pl.multiple_of((base // CHUNK + i * nch) * SUBLANES, nch * SUBLANES)
            fn(pl.ds(row0, rows_per), pl.ds(ge0, nch * SUBLANES), nch)
            return carry

        if count == 1:
            fn(pl.ds(base, rows_per), pl.ds(base // CHUNK * SUBLANES, nch * SUBLANES), nch)
        else:
            lax.fori_loop(0, count, run, 0)


def _dn_chunk_order(t, d, n_lat, n_ctx):
    if d == 0:
        return jnp.where(t < n_ctx, n_lat + t, t - n_ctx)
    return n_lat + n_ctx - 1 - t


def _dn_fill_intra(q_ref, k_ref, v_ref, bb_s, gc_s, w_s, u_s, ke_s, ge_s, aqk_s, qg_s, d, s, cl):
    dh = LANES

    def fill(rows, ge_rows, nch):
        def load(ref):
            return ref[rows, :].astype(F32).reshape(nch, CHUNK, dh)

        w, u, ke, ge, aqk, qg = _dn_intra(load(q_ref), load(k_ref), load(v_ref), load(bb_s), load(gc_s), d)
        w_s[rows, :] = w.reshape(nch * CHUNK, dh)
        u_s[rows, :] = u.reshape(nch * CHUNK, dh)
        ke_s[rows, :] = ke.reshape(nch * CHUNK, dh)
        qg_s[rows, :] = qg.reshape(nch * CHUNK, dh)
        aqk_s[rows, :] = aqk.reshape(nch * CHUNK, CHUNK)
        ge_s[ge_rows, :] = jnp.broadcast_to(ge, (nch, SUBLANES, dh)).reshape(nch * SUBLANES, dh)

    _for_sub_batches(s, cl, fill)


def _dn_chunk_refs(cid, w_s, u_s, ke_s, ge_s, aqk_s, qg_s):
    rows = pl.ds(pl.multiple_of(cid * CHUNK, CHUNK), CHUNK)
    ge = ge_s[pl.ds(pl.multiple_of(cid * SUBLANES, SUBLANES), SUBLANES), :][0:1]
    return rows, (w_s[rows, :], u_s[rows, :], ke_s[rows, :], ge, aqk_s[rows, :], qg_s[rows, :])


def _dn_scratch(m):
    dh = LANES
    big = pltpu.VMEM((m, dh), F32)
    return [big, big, big, big, big, pltpu.VMEM((m // CHUNK * SUBLANES, dh), F32), pltpu.VMEM((m, CHUNK), F32), big]


def _once(shape, index_map):
    return pl.BlockSpec(shape, index_map, pipeline_mode=pl.Buffered(1))


def dn_fwd(qkv, ba, a_log, dt_bias, *, s, cl, heads, name):
    m = s + cl
    dh = LANES
    n_lat, n_ctx = s // CHUNK, cl // CHUNK

    def body(q_ref, k_ref, v_ref, ba_ref, al_ref, dt_ref, o_ref, bb_s, gc_s, w_s, u_s, ke_s, ge_s, aqk_s, qg_s):
        h = pl.program_id(0)
        for d in (0, 1):
            beta, g, _, _ = _dn_gates(ba_ref[...], al_ref[...], dt_ref[...], d, h, heads)
            bb_s[...] = jnp.broadcast_to(beta, (m, dh))
            gc_s[...] = _chunk_cumsum(jnp.broadcast_to(g, (m, dh)), d)
            _dn_fill_intra(q_ref, k_ref, v_ref, bb_s, gc_s, w_s, u_s, ke_s, ge_s, aqk_s, qg_s, d, s, cl)

            def step(t, state):
                cid = _dn_chunk_order(t, d, n_lat, n_ctx)
                rows, terms = _dn_chunk_refs(cid, w_s, u_s, ke_s, ge_s, aqk_s, qg_s)
                state, o = _dn_step(state, *terms)

                @pl.when(cid < n_lat)
                def _():
                    if d == 0:
                        o_ref[rows, :] = o
                    else:
                        o_ref[rows, :] += o

                return state

            lax.fori_loop(0, n_lat + n_ctx, step, jnp.zeros((dh, dh), F32))

    def col(j0):
        return _once((m, dh), lambda h: (0, j0 + h))

    small = pl.BlockSpec((SUBLANES, LANES), lambda h: (0, 0))
    return pl.pallas_call(
        body, name=name, grid=(heads,),
        in_specs=[col(0), col(heads), col(2 * heads), _once((m, LANES), lambda h: (0, 0)), small, small],
        out_specs=pl.BlockSpec((s, dh), lambda h: (0, h)),
        out_shape=jax.ShapeDtypeStruct((s, heads * dh), F32),
        scratch_shapes=_dn_scratch(m),
        compiler_params=_params("parallel", vmem=BIG_KERNEL_VMEM))(qkv, qkv, qkv, ba, a_log, dt_bias)


def dn_out_fwd(o, proj, dn_norm, *, heads, gate_col0, name):
    s = o.shape[0]
    dh = LANES
    tr = _tile(s, 1024, 16)

    def body(o_ref, g_ref, nw_ref, y_ref):
        y_ref[...] = (_rms(o_ref[...])[0] * nw_ref[...] * _silu(g_ref[...])).astype(y_ref.dtype)

    blk = pl.BlockSpec((tr, dh), lambda i, h: (i, h))
    return pl.pallas_call(
        body, name=name, grid=(s // tr, heads),
        in_specs=[blk, pl.BlockSpec((tr, dh), lambda i, h: (i, gate_col0 + h)), pl.BlockSpec((1, dh), lambda i, h: (0, 0))],
        out_specs=blk, out_shape=jax.ShapeDtypeStruct((s, heads * dh), ACT_DTYPE),
        compiler_params=_params("parallel", "parallel"))(o, proj, dn_norm)


def dn_out_bwd(o, proj, d_y, dn_norm, *, heads, gate_col0, name):
    s = o.shape[0]
    dh = LANES
    tr = _tile(s, 1024, 16)

    def body(o_ref, g_ref, dy_ref, nw_ref, do_ref, dg_ref, dnw_ref):
        @pl.when((pl.program_id(0) == 0) & (pl.program_id(1) == 0))
        def _():
            dnw_ref[...] = jnp.zeros_like(dnw_ref)

        nrm, rs = _rms(o_ref[...])
        nw, gate, dy = nw_ref[...], g_ref[...], dy_ref[...]
        dg_ref[...] = (dy * (nrm * nw) * _dsilu(gate)).astype(dg_ref.dtype)
        dy0 = dy * _silu(gate)
        dnw_ref[0:1, :] += jnp.sum(dy0 * nrm, axis=0, keepdims=True)
        dn = dy0 * nw
        do_ref[...] = rs * (dn - nrm * jnp.mean(dn * nrm, axis=-1, keepdims=True))

    blk = pl.BlockSpec((tr, dh), lambda i, h: (i, h))
    return pl.pallas_call(
        body, name=name, grid=(s // tr, heads),
        in_specs=[blk, pl.BlockSpec((tr, dh), lambda i, h: (i, gate_col0 + h)), blk,
                  pl.BlockSpec((1, dh), lambda i, h: (0, 0))],
        out_specs=[blk, blk, pl.BlockSpec((SUBLANES, LANES), lambda i, h: (0, 0))],
        out_shape=[jax.ShapeDtypeStruct((s, heads * dh), F32), jax.ShapeDtypeStruct((s, heads * dh), ACT_DTYPE),
                   jax.ShapeDtypeStruct((SUBLANES, LANES), F32)],
        compiler_params=_params("arbitrary", "arbitrary"))(o, proj, d_y, dn_norm)


def dn_bwd(qkv, ba, d_o, a_log, dt_bias, *, s, cl, heads, name):
    m = s + cl
    dh = LANES
    n_lat, n_ctx = s // CHUNK, cl // CHUNK
    n_all = n_lat + n_ctx

    def body(q_ref, k_ref, v_ref, ba_ref, do_ref, al_ref, dt_ref, dqkv_ref, dba_ref, dal_ref, ddt_ref,
             bb_s, gc_s, w_s, u_s, ke_s, ge_s, aqk_s, qg_s, sall_s):
        h = pl.program_id(0)
        dq_ref, dk_ref, dv_ref = dqkv_ref.at[0], dqkv_ref.at[1], dqkv_ref.at[2]

        @pl.when(h == 0)
        def _():
            dba_ref[...] = jnp.zeros_like(dba_ref)

        lane = _iota((m, LANES), 1)
        for d in (0, 1):
            beta, g, pre, a_neg = _dn_gates(ba_ref[...], al_ref[...], dt_ref[...], d, h, heads)
            bb_s[...] = jnp.broadcast_to(beta, (m, dh))
            gc_s[...] = _chunk_cumsum(jnp.broadcast_to(g, (m, dh)), d)
            _dn_fill_intra(q_ref, k_ref, v_ref, bb_s, gc_s, w_s, u_s, ke_s, ge_s, aqk_s, qg_s, d, s, cl)

            def fwd_step(t, state):
                cid = _dn_chunk_order(t, d, n_lat, n_ctx)
                _, terms = _dn_chunk_refs(cid, w_s, u_s, ke_s, ge_s, aqk_s, qg_s)
                sall_s[pl.ds(pl.multiple_of(cid * dh, dh), dh), :] = state
                return _dn_step(state, *terms)[0]

            lax.fori_loop(0, n_all, fwd_step, jnp.zeros((dh, dh), F32))

            def bwd_step(i, dstate):
                cid = _dn_chunk_order(n_all - 1 - i, d, n_lat, n_ctx)
                rows, terms = _dn_chunk_refs(cid, w_s, u_s, ke_s, ge_s, aqk_s, qg_s)
                state = sall_s[pl.ds(pl.multiple_of(cid * dh, dh), dh), :]
                _, vjp = jax.vjp(_dn_step, state, *terms)
                lat_rows = pl.ds(pl.multiple_of(jnp.minimum(cid, n_lat - 1) * CHUNK, CHUNK), CHUNK)
                do = jnp.where(cid < n_lat, do_ref[lat_rows, :], 0.0)
                dstate, dw, du, dke, dge, daqk, dqg = vjp((dstate, do))
                w_s[rows, :] = dw
                u_s[rows, :] = du
                ke_s[rows, :] = dke
                qg_s[rows, :] = dqg
                aqk_s[rows, :] = daqk
                ge_s[pl.ds(pl.multiple_of(cid * SUBLANES, SUBLANES), SUBLANES), :] = jnp.broadcast_to(
                    dge, (SUBLANES, dh))
                return dstate

            lax.fori_loop(0, n_all, bwd_step, jnp.zeros((dh, dh), F32))

            def intra_bwd(rows, ge_rows, nch):
                def load(ref, width=dh):
                    return ref[rows, :].astype(F32).reshape(nch, CHUNK, width)

                _, vjp = jax.vjp(functools.partial(_dn_intra, d=d), load(q_ref), load(k_ref), load(v_ref),
                                 load(bb_s), load(gc_s))
                dge = ge_s[ge_rows, :].reshape(nch, SUBLANES, dh)[:, 0:1]
                dq, dk, dv, dbb, dgc = vjp((load(w_s), load(u_s), load(ke_s), dge, load(aqk_s, CHUNK), load(qg_s)))
                flat = lambda x: x.reshape(nch * CHUNK, dh)
                if d == 0:
                    dq_ref[rows, :], dk_ref[rows, :], dv_ref[rows, :] = flat(dq), flat(dk), flat(dv)
                else:
                    dq_ref[rows, :] += flat(dq)
                    dk_ref[rows, :] += flat(dk)
                    dv_ref[rows, :] += flat(dv)
                bb_s[rows, :] = flat(dbb)
                gc_s[rows, :] = flat(dgc)

            _for_sub_batches(s, cl, intra_bwd, _DN_SUB_BWD)

            dbeta = jnp.sum(bb_s[...], axis=1, keepdims=True)
            dg = jnp.sum(_chunk_cumsum(gc_s[...], 1 - d), axis=1, keepdims=True)
            dbraw = dbeta * beta * (1.0 - beta)
            dpre = dg * a_neg * jax.nn.sigmoid(pre)
            dba_ref[...] += (jnp.where(lane == d * heads + h, dbraw, 0.0)
                             + jnp.where(lane == (2 + d) * heads + h, dpre, 0.0))
            dal_ref[d:d + 1, :] = jnp.broadcast_to(jnp.sum(dg * g, axis=0, keepdims=True), (1, LANES))
            ddt_ref[d:d + 1, :] = jnp.broadcast_to(jnp.sum(dpre, axis=0, keepdims=True), (1, LANES))
        dal_ref[2:SUBLANES, :] = jnp.zeros((SUBLANES - 2, LANES), F32)
        ddt_ref[2:SUBLANES, :] = jnp.zeros((SUBLANES - 2, LANES), F32)

    def col(j0, rows=m):
        return _once((rows, dh), lambda h: (0, j0 + h))

    small = pl.BlockSpec((SUBLANES, LANES), lambda h: (0, 0))
    tile_h = pl.BlockSpec((None, SUBLANES, LANES), lambda h: (h, 0, 0))
    tiles = jax.ShapeDtypeStruct((heads, SUBLANES, LANES), F32)
    return pl.pallas_call(
        body, name=name, grid=(heads,),
        in_specs=[col(0), col(heads), col(2 * heads), _once((m, LANES), lambda h: (0, 0)), col(0, s), small, small],
        out_specs=[_once((3, m, dh), lambda h: (0, 0, h)), _once((m, LANES), lambda h: (0, 0)), tile_h, tile_h],
        out_shape=[jax.ShapeDtypeStruct((3, m, heads * dh), F32), jax.ShapeDtypeStruct((m, LANES), F32), tiles, tiles],
        scratch_shapes=_dn_scratch(m) + [pltpu.VMEM((n_all * dh, dh), F32)],
        compiler_params=_params("arbitrary", vmem=BIG_KERNEL_VMEM))(qkv, qkv, qkv, ba, d_o, a_log, dt_bias)


def _lin_scan(a, b, d):
    rows = a.shape[0]
    t = _iota((rows, 1), 0)
    step = 1
    while step < rows:
        if d == 0:
            ok, sa, sb = t >= step, pltpu.roll(a, step, 0), pltpu.roll(b, step, 0)
        else:
            ok, sa, sb = t < rows - step, pltpu.roll(a, rows - step, 0), pltpu.roll(b, rows - step, 0)
        b = b + a * jnp.where(ok, sb, 0.0)
        a = a * jnp.where(ok, sa, 1.0)
        step *= 2
    return b


def _lru_gates(xc, w_r, b_r, w_i, b_i, lam):
    r = jax.nn.sigmoid(_dot(xc, w_r, _NN) + b_r)
    i = jax.nn.sigmoid(_dot(xc, w_i, _NN) + b_i)
    log_a = -LRU_C * r * jax.nn.softplus(-lam)
    z = 2.0 * log_a
    series = -(z * (1.0 + z * (0.5 + z * (1.0 / 6.0))))
    one_minus = jnp.where(z > -0.01, series, 1.0 - jnp.exp(z))
    return jnp.exp(log_a), jnp.sqrt(one_minus) * (i * xc)


def _lru_states(a, b, d, s, cl):
    ac, bc = a[s:], b[s:]
    hc = _lin_scan(ac, bc, d)
    h0 = hc[cl - 1:cl] if d == 0 else hc[0:1]
    first = 0 if d == 0 else s - 1
    al = a[:s]
    bl = b[:s] + jnp.where(_iota((s, 1), 0) == first, al * h0, 0.0)
    return _lin_scan(al, bl, d), hc, h0


def _lru_specs(m, nb_dim):
    j_col = lambda rows: pl.BlockSpec((rows, LANES), lambda j: (0, j))
    w_blk = pl.BlockSpec((2, None, nb_dim, nb_dim), lambda j: (0, j, 0, 0))
    return j_col, w_blk


def lru_fwd(xcm, conv_w, conv_b, w_r, b_r, w_i, b_i, lam, *, s, cl, name):
    m, width = xcm.shape
    j_col, w_blk = _lru_specs(m, w_r.shape[-1])

    def body(x_ref, cw_ref, cb_ref, wr_ref, br_ref, wi_ref, bi_ref, lam_ref, h_ref):
        xc = _conv(x_ref[...], cw_ref[...], DN_CONV_OFFSETS, (s, cl)) + cb_ref[...]
        for d in (0, 1):
            a, b = _lru_gates(xc, wr_ref[d], br_ref[d:d + 1, :], wi_ref[d], bi_ref[d:d + 1, :], lam_ref[d:d + 1, :])
            h = _lru_states(a, b, d, s, cl)[0]
            if d == 0:
                h_ref[...] = h
            else:
                h_ref[...] += h

    return pl.pallas_call(
        body, name=name, grid=(width // LANES,),
        in_specs=[_once((m, LANES), lambda j: (0, j)), j_col(len(DN_CONV_OFFSETS)), j_col(1), w_blk, j_col(2), w_blk,
                  j_col(2), j_col(2)],
        out_specs=j_col(s), out_shape=jax.ShapeDtypeStruct((s, width), F32),
        compiler_params=_params("parallel", vmem=BIG_KERNEL_VMEM))(xcm, conv_w, conv_b, w_r, b_r, w_i, b_i, lam)


def lru_bwd(xcm, d_h, conv_w, conv_b, w_r, b_r, w_i, b_i, lam, *, s, cl, name):
    m, width = xcm.shape
    nb_dim = w_r.shape[-1]
    j_col, w_blk = _lru_specs(m, nb_dim)

    def body(x_ref, dh_ref, cw_ref, cb_ref, wr_ref, br_ref, wi_ref, bi_ref, lam_ref,
             dx_ref, dcw_ref, dcb_ref, dwr_ref, dbr_ref, dwi_ref, dbi_ref, dlam_ref):
        x, cw = x_ref[...], cw_ref[...]
        xc = _conv(x, cw, DN_CONV_OFFSETS, (s, cl)) + cb_ref[...]
        d_hl = dh_ref[...]
        d_xc = None
        for d in (0, 1):
            (a, b), vjp = jax.vjp(_lru_gates, xc, wr_ref[d], br_ref[d:d + 1, :], wi_ref[d], bi_ref[d:d + 1, :],
                                  lam_ref[d:d + 1, :])
            h, hc, h0 = _lru_states(a, b, d, s, cl)
            nxt = 1 if d == 0 else -1
            first = 0 if d == 0 else s - 1
            al, ac = a[:s], a[s:]
            lam_l = _lin_scan(_shift(al, nxt, (s, 0)), d_hl, 1 - d)
            h_prev = _shift(h, -nxt, (s, 0)) + jnp.where(_iota((s, 1), 0) == first, h0, 0.0)
            d_h0 = (al * lam_l)[first:first + 1]
            last_c = cl - 1 if d == 0 else 0
            d_hc = jnp.where(_iota((cl, 1), 0) == last_c, d_h0, 0.0)
            lam_c = _lin_scan(_shift(ac, nxt, (cl, 0)), d_hc, 1 - d)
            da = jnp.concatenate([lam_l * h_prev, lam_c * _shift(hc, -nxt, (cl, 0))], axis=0)
            db = jnp.concatenate([lam_l, lam_c], axis=0)
            g_xc, g_wr, g_br, g_wi, g_bi, g_lam = vjp((da, db))
            d_xc = g_xc if d_xc is None else d_xc + g_xc
            dwr_ref[d], dwi_ref[d] = g_wr, g_wi
            dbr_ref[d:d + 1, :], dbi_ref[d:d + 1, :], dlam_ref[d:d + 1, :] = g_br, g_bi, g_lam
        dx, dcw = _conv_bwd(x, cw, d_xc, DN_CONV_OFFSETS, (s, cl))
        dx_ref[...] = dx.astype(dx_ref.dtype)
        dcw_ref[...] = dcw
        dcb_ref[...] = jnp.sum(d_xc, axis=0, keepdims=True)

    kk = len(DN_CONV_OFFSETS)
    vec2 = jax.ShapeDtypeStruct((2, width), F32)
    return pl.pallas_call(
        body, name=name, grid=(width // LANES,),
        in_specs=[_once((m, LANES), lambda j: (0, j)), _once((s, LANES), lambda j: (0, j)), j_col(kk), j_col(1),
                  w_blk, j_col(2), w_blk, j_col(2), j_col(2)],
        out_specs=[j_col(m), j_col(kk), j_col(1), w_blk, j_col(2), w_blk, j_col(2), j_col(2)],
        out_shape=[jax.ShapeDtypeStruct((m, width), ACT_DTYPE), jax.ShapeDtypeStruct((kk, width), F32),
                   jax.ShapeDtypeStruct((1, width), F32), jax.ShapeDtypeStruct(w_r.shape, F32), vec2,
                   jax.ShapeDtypeStruct(w_i.shape, F32), vec2, vec2],
        compiler_params=_params("parallel", vmem=BIG_KERNEL_VMEM))(xcm, d_h, conv_w, conv_b, w_r, b_r, w_i, b_i, lam)


def lru_gate_fwd(h, proj, *, gate_col0, name):
    s, width = h.shape
    tr, tc = _tile(s, 512, 16), _tile(width, 512, LANES)
    c0 = gate_col0 * LANES // tc

    def body(h_ref, g_ref, y_ref):
        y_ref[...] = (h_ref[...] * _silu(g_ref[...])).astype(y_ref.dtype)

    blk = pl.BlockSpec((tr, tc), lambda i, j: (i, j))
    return pl.pallas_call(
        body, name=name, grid=(s // tr, width // tc),
        in_specs=[blk, pl.BlockSpec((tr, tc), lambda i, j: (i, c0 + j))], out_specs=blk,
        out_shape=jax.ShapeDtypeStruct((s, width), ACT_DTYPE),
        compiler_params=_params("parallel", "parallel"))(h, proj)


def lru_gate_bwd(h, proj, d_y, *, gate_col0, dy_col0, name):
    s, width = h.shape
    tr, tc = _tile(s, 512, 16), _tile(width, 512, LANES)
    c0, y0 = gate_col0 * LANES // tc, dy_col0 * LANES // tc

    def body(h_ref, g_ref, dy_ref, dh_ref, dg_ref):
        g, dy = g_ref[...], dy_ref[...]
        dh_ref[...] = dy * _silu(g)
        dg_ref[...] = (dy * h_ref[...] * _dsilu(g)).astype(dg_ref.dtype)

    blk = pl.BlockSpec((tr, tc), lambda i, j: (i, j))
    return pl.pallas_call(
        body, name=name, grid=(s // tr, width // tc),
        in_specs=[blk, pl.BlockSpec((tr, tc), lambda i, j: (i, c0 + j)), pl.BlockSpec((tr, tc), lambda i, j: (i, y0 + j))],
        out_specs=[blk, blk],
        out_shape=[jax.ShapeDtypeStruct((s, width), F32), jax.ShapeDtypeStruct((s, width), ACT_DTYPE)],
        compiler_params=_params("parallel", "parallel"))(h, proj, d_y)


def _sc_parts(p, width):
    return [p[:, k * width:(k + 1) * width] for k in range(4)]


def sc_mix_fwd(p, conv_w, *, name):
    s, width = p.shape[0], conv_w.shape[1]
    tr = 2 * GRID_W

    def body(p_ref, w_ref, y_ref):
        b_g, c_g, x_in, gate = _sc_parts(p_ref[...], width)
        z = _conv(c_g * x_in, w_ref[...], SC_CONV_OFFSETS, (GRID_W, 0))
        y_ref[...] = (b_g * z * _silu(gate)).astype(y_ref.dtype)

    return pl.pallas_call(
        body, name=name, grid=(s // tr,),
        in_specs=[pl.BlockSpec((tr, 4 * width), lambda i: (i, 0)), pl.BlockSpec(conv_w.shape, lambda i: (0, 0))],
        out_specs=pl.BlockSpec((tr, width), lambda i: (i, 0)),
        out_shape=jax.ShapeDtypeStruct((s, width), ACT_DTYPE),
        compiler_params=_params("parallel"))(p, conv_w)


def sc_mix_bwd(p, d_y, conv_w, *, name):
    s, width = p.shape[0], conv_w.shape[1]
    tr = 2 * GRID_W

    def body(p_ref, dy_ref, w_ref, dp_ref, dw_ref):
        @pl.when(pl.program_id(0) == 0)
        def _():
            dw_ref[...] = jnp.zeros_like(dw_ref)

        b_g, c_g, x_in, gate = _sc_parts(p_ref[...], width)
        w, dy = w_ref[...], dy_ref[...]
        u = c_g * x_in
        z = _conv(u, w, SC_CONV_OFFSETS, (GRID_W, 0))
        sg = _silu(gate)
        du, dw = _conv_bwd(u, w, dy * b_g * sg, SC_CONV_OFFSETS, (GRID_W, 0))
        dw_ref[...] += dw
        parts = (dy * z * sg, du * x_in, du * c_g, dy * b_g * z * _dsilu(gate))
        for k, part in enumerate(parts):
            dp_ref[:, k * width:(k + 1) * width] = part.astype(dp_ref.dtype)

    return pl.pallas_call(
        body, name=name, grid=(s // tr,),
        in_specs=[pl.BlockSpec((tr, 4 * width), lambda i: (i, 0)), pl.BlockSpec((tr, width), lambda i: (i, 0)),
                  pl.BlockSpec(conv_w.shape, lambda i: (0, 0))],
        out_specs=[pl.BlockSpec((tr, 4 * width), lambda i: (i, 0)), pl.BlockSpec(conv_w.shape, lambda i: (0, 0))],
        out_shape=[jax.ShapeDtypeStruct((s, 4 * width), ACT_DTYPE), jax.ShapeDtypeStruct(conv_w.shape, F32)],
        compiler_params=_params("arbitrary"))(p, d_y, conv_w)


MOD_ROWS = 16


def mod_fwd(cond, mod_w, mod_b, *, name):
    nl, d, ns = mod_w.shape
    tn = _tile(ns, 512, LANES)

    def body(c_ref, w_ref, b_ref, o_ref):
        o_ref[...] = _dot(_silu(c_ref[...]), w_ref[...], _NN) + b_ref[...]

    return pl.pallas_call(
        body, name=name, grid=(nl, ns // tn),
        in_specs=[pl.BlockSpec((MOD_ROWS, d), lambda l, j: (0, 0)), pl.BlockSpec((None, d, tn), lambda l, j: (l, 0, j)),
                  pl.BlockSpec((None, 1, tn), lambda l, j: (l, 0, j))],
        out_specs=pl.BlockSpec((None, MOD_ROWS, tn), lambda l, j: (l, 0, j)),
        out_shape=jax.ShapeDtypeStruct((nl, MOD_ROWS, ns), F32),
        compiler_params=_params("parallel", "parallel"))(cond, mod_w, mod_b)


def _adamw(w, g, m, v):
    m = ADAM_B1 * m + (1.0 - ADAM_B1) * g
    v = ADAM_B2 * v + (1.0 - ADAM_B2) * (g * g)
    m_hat = m / (1.0 - ADAM_B1 ** ADAM_STEP)
    v_hat = v / (1.0 - ADAM_B2 ** ADAM_STEP)
    return -ADAM_LR * (m_hat / (jnp.sqrt(v_hat) + ADAM_EPS) + ADAM_WD * w), m, v


def mod_adam(cond, d_mod, w, m, v, *, name):
    nl, d, ns = w.shape
    tr, tn = _tile(d, 256, SUBLANES), _tile(ns, 1024, LANES)

    def body(c_ref, dm_ref, w_ref, m_ref, v_ref, g_ref, dl_ref, nm_ref, nv_ref, ds_ref):
        @pl.when(pl.program_id(2) == 0)
        def _():
            ds_ref[...] = jnp.zeros_like(ds_ref)

        wv, dm = w_ref[...], dm_ref[...]
        g = _dot(_silu(c_ref[...]), dm, _TN)
        ds_ref[...] += _dot(dm, wv, _NT)
        g_ref[...] = g
        dl_ref[...], nm_ref[...], nv_ref[...] = _adamw(wv, g, m_ref[...], v_ref[...])

    blk = pl.BlockSpec((None, tr, tn), lambda l, i, j: (l, i, j))
    full = jax.ShapeDtypeStruct(w.shape, F32)
    return pl.pallas_call(
        body, name=name, grid=(nl, d // tr, ns // tn),
        in_specs=[pl.BlockSpec((MOD_ROWS, tr), lambda l, i, j: (0, i)),
                  pl.BlockSpec((None, MOD_ROWS, tn), lambda l, i, j: (l, 0, j)), blk, blk, blk],
        out_specs=[blk, blk, blk, blk, pl.BlockSpec((None, MOD_ROWS, tr), lambda l, i, j: (l, 0, i))],
        out_shape=[full, full, full, full, jax.ShapeDtypeStruct((nl, MOD_ROWS, d), F32)],
        compiler_params=_params("parallel", "parallel", "arbitrary"))(cond, d_mod, w, m, v)


def _row_tile(rows, cols, itemsize, mult):
    return _tile(rows, max(mult, (2 << 20) // (cols * itemsize)), mult)


def adam_update(w, g, m, v, *, name):
    r, c = w.shape
    tr = _row_tile(r, c, 4, SUBLANES)

    def body(w_ref, g_ref, m_ref, v_ref, dl_ref, nm_ref, nv_ref):
        dl_ref[...], nm_ref[...], nv_ref[...] = _adamw(w_ref[...], g_ref[...], m_ref[...], v_ref[...])

    blk = pl.BlockSpec((tr, c), lambda i: (i, 0))
    return pl.pallas_call(
        body, name=name, grid=(r // tr,), in_specs=[blk] * 4, out_specs=[blk] * 3,
        out_shape=[jax.ShapeDtypeStruct((r, c), F32)] * 3, compiler_params=_params("parallel"))(w, g, m, v)


def cast_rows(x, dtype, *, name):
    r, c = x.shape
    tr = _row_tile(r, c, 4, 16)

    def body(x_ref, o_ref):
        o_ref[...] = x_ref[...].astype(dtype)

    blk = pl.BlockSpec((tr, c), lambda i: (i, 0))
    return pl.pallas_call(body, name=name, grid=(r // tr,), in_specs=[blk], out_specs=blk,
                          out_shape=jax.ShapeDtypeStruct((r, c), dtype), compiler_params=_params("parallel"))(x)


def add_sibling_half(core, mine, other, *, name):
    a, _, r, c = mine.shape
    tr = _row_tile(r, c, 4, 16)

    def body(core_ref, x_ref, y_ref, o_ref):
        o_ref[...] = (x_ref[...].astype(F32) + y_ref[...].astype(F32)).astype(o_ref.dtype)

    grid_spec = pltpu.PrefetchScalarGridSpec(
        num_scalar_prefetch=1, grid=(a, r // tr),
        in_specs=[pl.BlockSpec((None, None, tr, c), lambda k, i, core_ref: (k, core_ref[0], i, 0)),
                  pl.BlockSpec((None, tr, c), lambda k, i, core_ref: (k, i, 0))],
        out_specs=pl.BlockSpec((None, tr, c), lambda k, i, core_ref: (k, i, 0)))
    return pl.pallas_call(body, name=name, grid_spec=grid_spec, out_shape=jax.ShapeDtypeStruct((a, r, c), WIRE_DTYPE),
                          compiler_params=_params("parallel", "parallel"))(core, mine, other)


def sum_slots(x, *, name):
    n, r, c = x.shape
    tr = _row_tile(r, c * n, 4, 16)

    def body(x_ref, o_ref):
        acc = x_ref[0].astype(F32)
        for k in range(1, n):
            acc = acc + x_ref[k].astype(F32)
        o_ref[...] = acc

    return pl.pallas_call(
        body, name=name, grid=(r // tr,), in_specs=[pl.BlockSpec((n, tr, c), lambda i: (0, i, 0))],
        out_specs=pl.BlockSpec((tr, c), lambda i: (i, 0)), out_shape=jax.ShapeDtypeStruct((r, c), F32),
        compiler_params=_params("parallel"))(x)


def ctx_cond_grad(parts, c_ctx, *, name):
    def body(p_ref, c_ref, o_ref):
        acc = p_ref[0]
        for k in range(1, N_CHIPS):
            acc = acc + p_ref[k]
        o_ref[...] = acc * _dsilu(c_ref[...])

    return pl.pallas_call(body, name=name, out_shape=jax.ShapeDtypeStruct(c_ctx.shape, F32))(parts, c_ctx)


PACK_ROWS = 256


def _tile_rows(shape):
    n = 1
    for dim in shape:
        n *= dim
    return -(-n // (SUBLANES * LANES)) * SUBLANES


def _pack(arrs):
    parts = []
    for a in arrs:
        flat = a.reshape(-1).astype(F32)
        rows = _tile_rows(a.shape)
        parts.append(jnp.pad(flat, (0, rows * LANES - flat.shape[0])).reshape(rows, LANES))
    total = sum(t.shape[0] for t in parts)
    parts.append(jnp.zeros((-total % PACK_ROWS, LANES), F32))
    return jnp.concatenate(parts, axis=0)


def _unpack(flat, shapes):
    lead = flat.shape[:-2]
    outs, r0 = [], 0
    for shape in shapes:
        rows = _tile_rows(shape)
        n = 1
        for dim in shape:
            n *= dim
        piece = flat[..., r0:r0 + rows, :].reshape(lead + (rows * LANES,))[..., :n]
        outs.append(piece.reshape(lead + tuple(shape)))
        r0 += rows
    return outs


WEIGHTS = ('c_ctx', 'mod_w', 'mod_b', 'norm_w', 'ab_w_in', 'ab_qkv_conv', 'ab_a_log', 'ab_dt_bias', 'ab_dn_norm',
           'ab_lru_conv_w', 'ab_lru_conv_b', 'ab_lru_w_r', 'ab_lru_b_r', 'ab_lru_w_i', 'ab_lru_b_i', 'ab_lru_lambda',
           'ab_w_out', 'sc_w_in', 'sc_conv', 'sc_w_out', 'final_norm_w')
BIG_WEIGHTS = ('ab_w_in', 'ab_w_out', 'sc_w_in', 'sc_w_out')


def kernel(x, c, ctx, c_ctx, mod_w, mod_b, norm_w, ab_w_in, ab_qkv_conv, ab_a_log, ab_dt_bias, ab_dn_norm, ab_lru_conv_w, ab_lru_conv_b, ab_lru_w_r, ab_lru_b_r, ab_lru_w_i, ab_lru_b_i, ab_lru_lambda, ab_w_out, sc_w_in, sc_conv, sc_w_out, final_norm_w, loss_target, m_c_ctx, m_mod_w, m_mod_b, m_norm_w, m_ab_w_in, m_ab_qkv_conv, m_ab_a_log, m_ab_dt_bias, m_ab_dn_norm, m_ab_lru_conv_w, m_ab_lru_conv_b, m_ab_lru_w_r, m_ab_lru_b_r, m_ab_lru_w_i, m_ab_lru_b_i, m_ab_lru_lambda, m_ab_w_out, m_sc_w_in, m_sc_conv, m_sc_w_out, m_final_norm_w, v_c_ctx, v_mod_w, v_mod_b, v_norm_w, v_ab_w_in, v_ab_qkv_conv, v_ab_a_log, v_ab_dt_bias, v_ab_dn_norm, v_ab_lru_conv_w, v_ab_lru_conv_b, v_ab_lru_w_r, v_ab_lru_b_r, v_ab_lru_w_i, v_ab_lru_b_i, v_ab_lru_lambda, v_ab_w_out, v_sc_w_in, v_sc_conv, v_sc_w_out, v_final_norm_w):
    given = dict(locals())
    weights = {n: given[n] for n in WEIGHTS}
    mom1 = {n: given['m_' + n] for n in WEIGHTS}
    mom2 = {n: given['v_' + n] for n in WEIGHTS}

    xi, yi, ci = lax.axis_index("x"), lax.axis_index("y"), lax.axis_index("c")
    chip = 2 * xi + yi
    dev = 2 * chip + ci
    x2d, ctx2d, target = x[0], ctx[0], loss_target[0]
    s, d = x2d.shape
    cl = ctx2d.shape[0]
    heads = ab_a_log.shape[-1]
    wdn = heads * LANES
    nb = ab_lru_w_r.shape[2]
    wl = nb * ab_lru_w_r.shape[3]
    sc = sc_w_in.shape[-1]
    ab_out = wdn + wl
    off_beta = 3 * wdn + wl
    ab_state = off_beta + 4 * heads
    ab_in = ab_state + wdn + wl
    ni = ab_in // N_CHIPS
    ns = mod_w.shape[-1]
    grid_rows = s // GRID_W

    def to_col_major(t):
        return t.reshape(grid_rows, GRID_W, t.shape[-1]).swapaxes(0, 1).reshape(s, t.shape[-1])

    def to_raster(t):
        return t.reshape(GRID_W, grid_rows, t.shape[-1]).swapaxes(0, 1).reshape(s, t.shape[-1])

    def from_chips(t):
        return jnp.moveaxis(t[0::N_CORES], 0, 1).reshape(t.shape[1], -1)

    def own_columns(t, width):
        return lax.dynamic_slice_in_dim(t, chip * width, width, axis=t.ndim - 1)

    small_shards = [c[0], ab_qkv_conv[0], ab_lru_conv_w[0], ab_lru_b_r[0], ab_lru_b_i[0], ab_lru_lambda[0], sc_conv[0]]
    gathered0 = all_gather_devices(_pack(small_shards), "ag_small_params")
    c_all, qkv_sh, lcw_sh, lbr_sh, lbi_sh, llam_sh, scv_sh = _unpack(gathered0, [t.shape for t in small_shards])
    qkv_conv, lru_conv_w, sc_conv_w = from_chips(qkv_sh), from_chips(lcw_sh), from_chips(scv_sh)
    lru_b_r, lru_b_i, lru_lam = from_chips(lbr_sh), from_chips(lbi_sh), from_chips(llam_sh)

    shards = [weights[n][0] for n in BIG_WEIGHTS]
    halves = [cast_rows(t, WIRE_DTYPE, name=f"cast_w{k}").reshape(N_CORES, t.shape[0] // N_CORES, t.shape[1])
              for k, t in enumerate(shards)]
    from_chips_half = chip_exchange(halves, gather=True, name="ag_w_chips")
    from_sibling_half = sibling_exchange(from_chips_half, pick=False, name="ag_w_cores")
    full = [core_halves(ci, a, b) for a, b in zip(from_chips_half, from_sibling_half)]
    w_in_full = jnp.moveaxis(full[0].reshape(N_CHIPS, d, ni), 0, 1).reshape(d, ab_in)
    w_main = jnp.concatenate([w_in_full[:, :off_beta], w_in_full[:, ab_state:]], axis=1)
    w_ba = jnp.pad(w_in_full[:, off_beta:ab_state], ((0, 0), (0, LANES - 4 * heads)))
    w_ab_out = full[1].reshape(ab_out, d)
    w_sc_in = full[2].reshape(N_CHIPS, d, sc)
    w_sc_out = full[3].reshape(sc, d)

    cond = jnp.zeros((MOD_ROWS, d), F32).at[:N_DEV].set(c_all).at[N_DEV].set(c_ctx)
    mod_shard = mod_fwd(cond, mod_w, own_columns(mod_b, ns)[:, None, :], name="mod_fwd")
    gathered_mod = all_gather_devices(mod_shard.reshape(-1, LANES), "ag_mod")
    mod_all = jnp.moveaxis(gathered_mod[0::N_CORES].reshape(N_CHIPS, 2, MOD_ROWS, ns), 0, 2).reshape(2, MOD_ROWS, 3 * d)
    own_mod = lax.dynamic_index_in_dim(mod_all, dev, axis=1, keepdims=False)
    shift, scale, gate = own_mod[:, :d], own_mod[:, d:2 * d], own_mod[:, 2 * d:]
    shift_c, scale_c = mod_all[0, N_DEV, :d], mod_all[0, N_DEV, d:2 * d]

    def pair(a, b):
        return jnp.stack([a, b])[:, None, :]

    hn_all = norm_mod_fwd(x2d, ctx2d, norm_w[0:1], pair(scale[0], scale_c), pair(shift[0], shift_c), name="norm0_fwd")
    proj = mm_nn(hn_all, w_main, out_dtype=F32, name="proj0")
    ba = mm_nn(hn_all, w_ba, out_dtype=F32, name="proj0_ba")
    qkv = qkv_conv_fwd(proj, qkv_conv, s=s, cl=cl, heads=heads, name="qkv_conv_fwd")

    def pad_dh(t):
        return jnp.zeros((SUBLANES, LANES), F32).at[:2, :heads].set(t)

    a_log_t, dt_bias_t = pad_dh(ab_a_log[0]), pad_dh(ab_dt_bias[0])
    gate_dn0 = (3 * wdn + wl) // LANES
    gate_lru0 = gate_dn0 + wdn // LANES
    o_dn = dn_fwd(qkv, ba, a_log_t, dt_bias_t, s=s, cl=cl, heads=heads, name="dn_fwd")
    y_dn = dn_out_fwd(o_dn, proj, ab_dn_norm, heads=heads, gate_col0=gate_dn0, name="dn_out_fwd")
    lru_in = proj[:, 3 * wdn:3 * wdn + wl]
    xcm = jnp.concatenate([to_col_major(lru_in[:s]), lru_in[s:]], axis=0)
    lru_w = (lru_conv_w, ab_lru_conv_b, ab_lru_w_r[0], lru_b_r, ab_lru_w_i[0], lru_b_i, lru_lam)
    h_lru = to_raster(lru_fwd(xcm, *lru_w, s=s, cl=cl, name="lru_fwd"))
    y_lru = lru_gate_fwd(h_lru, proj, gate_col0=gate_lru0, name="lru_gate_fwd")
    y_ab = jnp.concatenate([y_dn, y_lru], axis=1)
    x1, yo0 = mm_nn(y_ab, w_ab_out, out_dtype=F32, name="out0", resid=x2d, gate=gate[0:1], aux_dtype=ACT_DTYPE,
                    tn_cap=512)

    hn1 = norm_mod_fwd(x1, None, norm_w[1:2], scale[1][None, None, :], shift[1][None, None, :], name="norm1_fwd")
    tm1, tn1, tk1 = _tile(s, 1024, 16), _tile(sc, 1024, LANES), _tile(d, MATMUL_TK, LANES)
    p = matmul(hn1, w_sc_in, m=s, n=4 * sc, k=d, tm=tm1, tn=tn1, tk=tk1, dims=_NN, out_dtype=F32, name="proj1",
               a_spec=pl.BlockSpec((tm1, tk1), lambda i, j, kk: (i, kk)),
               b_spec=pl.BlockSpec((None, tk1, tn1), lambda i, j, kk: (j // (sc // tn1), kk, j % (sc // tn1))))
    y_sc = sc_mix_fwd(p, sc_conv_w, name="sc_mix_fwd")
    x2, yo1 = mm_nn(y_sc, w_sc_out, out_dtype=F32, name="out1", resid=x1, gate=gate[1:2], aux_dtype=ACT_DTYPE,
                    tn_cap=512)
    loss_t, dx2, d_fnw = loss_head(x2, final_norm_w[None], target, name="loss_head")

    d_yo1, d_gate1 = gate_bwd(dx2, yo1, gate[1:2], name="gate1_bwd")
    d_ysc = mm_nt(d_yo1, w_sc_out, out_dtype=F32, name="out1_dx")
    dw_sc_out = mm_tn(y_sc, d_yo1, out_dtype=WIRE_DTYPE, name="out1_dw")
    d_p, d_sc_conv = sc_mix_bwd(p, d_ysc, sc_conv_w, name="sc_mix_bwd")
    tkp = _tile(sc, MATMUL_TK, LANES)
    tnd = _tile(d, 1024, LANES)
    d_hn1 = matmul(d_p, w_sc_in, m=s, n=d, k=4 * sc, tm=tm1, tn=tnd, tk=tkp, dims=_NT, out_dtype=F32, name="proj1_dx",
                   a_spec=pl.BlockSpec((tm1, tkp), lambda i, j, kk: (i, kk)),
                   b_spec=pl.BlockSpec((None, tnd, tkp), lambda i, j, kk: (kk // (sc // tkp), j, kk % (sc // tkp))))
    tks = _tile(s, MATMUL_TK_TOKENS, 16)
    dw_sc_in = matmul(hn1, d_p, m=d, n=4 * sc, k=s, tm=tnd, tn=tn1, tk=tks, dims=_TN, out_dtype=WIRE_DTYPE,
                      name="proj1_dw", a_spec=pl.BlockSpec((tks, tnd), lambda i, j, kk: (kk, i)),
                      b_spec=pl.BlockSpec((tks, tn1), lambda i, j, kk: (kk, j)),
                      out_spec=pl.BlockSpec((None, tnd, tn1), lambda i, j, kk: (j // (sc // tn1), i, j % (sc // tn1))),
                      out_shape=(N_CHIPS, d, sc))
    dx1, d_nw1, d_scale1, d_shift1 = norm_mod_bwd(x1, norm_w[1:2], scale[1:2], d_hn1, row0=0, resid=dx2, init=None,
                                                  name="norm1_bwd")

    d_yo0, d_gate0 = gate_bwd(dx1, yo0, gate[0:1], name="gate0_bwd")
    d_y = mm_nt(d_yo0, w_ab_out, out_dtype=F32, name="out0_dx")
    dw_ab_out = mm_tn(y_ab, d_yo0, out_dtype=WIRE_DTYPE, name="out0_dw")
    d_o, d_gate_dn, d_dn_norm = dn_out_bwd(o_dn, proj, d_y, ab_dn_norm, heads=heads, gate_col0=gate_dn0, name="dn_out_bwd")
    d_qkv, d_ba, d_alog_t, d_dtb_t = dn_bwd(qkv, ba, d_o, a_log_t, dt_bias_t, s=s, cl=cl, heads=heads, name="dn_bwd")
    d_qkv_raw, d_qkv_conv = qkv_conv_bwd(proj, qkv_conv, d_qkv, s=s, cl=cl, heads=heads, name="qkv_conv_bwd")
    d_h, d_gate_lru = lru_gate_bwd(h_lru, proj, d_y, gate_col0=gate_lru0, dy_col0=wdn // LANES, name="lru_gate_bwd")
    (d_xcm, d_lcw, d_lcb, d_wr, d_br, d_wi, d_bi, d_lam) = lru_bwd(xcm, to_col_major(d_h), *lru_w, s=s, cl=cl,
                                                                  name="lru_bwd")
    d_lru_in = jnp.concatenate([to_raster(d_xcm[:s]), d_xcm[s:]], axis=0)
    ctx_zeros = jnp.zeros((cl, wdn + wl), ACT_DTYPE)
    d_gates = jnp.concatenate([jnp.concatenate([d_gate_dn, d_gate_lru], axis=1), ctx_zeros], axis=0)
    d_proj = jnp.concatenate([d_qkv_raw, d_lru_in, d_gates], axis=1)
    d_hn_ba = mm_nt(d_ba, w_ba, out_dtype=F32, name="proj0_ba_dx")
    d_hn_all = mm_nt(d_proj, w_main, out_dtype=F32, name="proj0_dx", resid=d_hn_ba)
    dw_main = mm_tn(hn_all, d_proj, out_dtype=WIRE_DTYPE, name="proj0_dw")
    dw_ba = mm_tn(hn_all, d_ba, out_dtype=WIRE_DTYPE, name="proj0_ba_dw")
    _, d_nw0c, d_scale_c, d_shift_c = norm_mod_bwd(ctx2d, norm_w[0:1], scale_c[None], d_hn_all, row0=s, resid=None,
                                                   init=None, name="norm0_bwd_ctx")
    dx, d_nw0, d_scale0, d_shift0 = norm_mod_bwd(x2d, norm_w[0:1], scale[0:1], d_hn_all, row0=0, resid=dx1, init=d_nw0c,
                                                 name="norm0_bwd")

    dw_in_full = jnp.concatenate([dw_main[:, :off_beta], dw_ba[:, :4 * heads], dw_main[:, off_beta:]], axis=1)
    by_chip = [jnp.moveaxis(dw_in_full.reshape(d, N_CHIPS, ni), 1, 0), dw_ab_out, dw_sc_in, dw_sc_out]
    slot_rows = [d, ab_out // N_CHIPS, d, sc // N_CHIPS]
    by_chip = [t.reshape(N_CHIPS, N_CORES, r // N_CORES, t.shape[-1]) for t, r in zip(by_chip, slot_rows)]
    from_sibling = sibling_exchange(by_chip, pick=True, name="rs_cores")
    core = ci.astype(jnp.int32).reshape(1)
    halves_sum = [add_sibling_half(core, a, b, name=f"rs_add{k}") for k, (a, b) in enumerate(zip(by_chip, from_sibling))]
    from_chips_sum = chip_exchange(halves_sum, gather=False, name="rs_chips")
    reduced = [sum_slots(t, name=f"rs_sum{k}")[None] for k, t in enumerate(from_chips_sum)]
    both_halves = [core_halves(ci, a, b) for a, b in zip(reduced, sibling_exchange(reduced, pick=False, name="rs_back"))]
    grads = {n: t.reshape(weights[n].shape) for n, t in zip(BIG_WEIGHTS, both_halves)}

    d_mod_own = jnp.stack([jnp.concatenate([d_shift0[0], d_scale0[0], d_gate0[0]]),
                           jnp.concatenate([d_shift1[0], d_scale1[0], d_gate1[0]])])
    d_mod_ctx = jnp.concatenate([d_shift_c[0], d_scale_c[0], jnp.zeros((d,), F32)])
    summable = [
        loss_t[0, 0:1], d_mod_own.at[0].add(d_mod_ctx), d_mod_ctx, jnp.concatenate([d_nw0, d_nw1], axis=0), d_qkv_conv,
        d_alog_t[:, :2, 0].T, d_dtb_t[:, :2, 0].T, d_dn_norm[0], d_lcw, d_lcb[0], d_wr, d_br, d_wi, d_bi, d_lam,
        d_sc_conv, d_fnw[0]]
    sum_shapes = [t.shape for t in summable]
    gathered1 = all_gather_devices(_pack(summable + [d_mod_own]), "ag_small_grads")
    totals = _unpack(sum_slots(gathered1, name="sum_small_grads"), sum_shapes)
    (loss_sum, g_mod_b, d_mod_ctx_sum, g_norm_w, g_qkv_conv, g_a_log, g_dt_bias, g_dn_norm, g_lcw, g_lcb, g_wr, g_br,
     g_wi, g_bi, g_lam, g_sc_conv, g_fnw) = totals
    d_mod_rows = _unpack(gathered1, sum_shapes + [d_mod_own.shape])[-1]

    d_mod_all = jnp.zeros((2, MOD_ROWS, 3 * d), F32).at[:, :N_DEV].set(jnp.moveaxis(d_mod_rows, 0, 1))
    d_mod_all = d_mod_all.at[0, N_DEV].set(d_mod_ctx_sum)
    g_mod_w, dl_mod_w, nm_mod_w, nv_mod_w, d_silu = mod_adam(cond, own_columns(d_mod_all, ns), mod_w, m_mod_w, v_mod_w,
                                                             name="mod_adam")
    gathered2 = all_gather_devices(d_silu[0, N_DEV].reshape(-1, LANES), "ag_ctx_cond")
    g_c_ctx = ctx_cond_grad(gathered2[0::N_CORES], c_ctx.reshape(-1, LANES), name="ctx_cond_grad").reshape(d)

    grads.update({
        'c_ctx': g_c_ctx, 'mod_w': g_mod_w, 'mod_b': g_mod_b, 'norm_w': g_norm_w,
        'ab_qkv_conv': own_columns(g_qkv_conv, qkv_conv.shape[1] // N_CHIPS), 'ab_a_log': g_a_log, 'ab_dt_bias': g_dt_bias,
        'ab_dn_norm': g_dn_norm, 'ab_lru_conv_w': own_columns(g_lcw, wl // N_CHIPS), 'ab_lru_conv_b': g_lcb,
        'ab_lru_w_r': g_wr, 'ab_lru_b_r': own_columns(g_br, wl // N_CHIPS), 'ab_lru_w_i': g_wi,
        'ab_lru_b_i': own_columns(g_bi, wl // N_CHIPS), 'ab_lru_lambda': own_columns(g_lam, wl // N_CHIPS),
        'sc_conv': own_columns(g_sc_conv, sc // N_CHIPS), 'final_norm_w': g_fnw})
    grads = {n: grads[n].reshape(weights[n].shape) for n in WEIGHTS}

    delta, new_m, new_v = {'mod_w': dl_mod_w}, {'mod_w': nm_mod_w}, {'mod_w': nv_mod_w}
    for k, n in enumerate(BIG_WEIGHTS):
        as2d = lambda t: t.reshape(-1, t.shape[-1])
        upd = adam_update(as2d(weights[n]), as2d(grads[n]), as2d(mom1[n]), as2d(mom2[n]), name=f"adam_big{k}")
        delta[n], new_m[n], new_v[n] = (t.reshape(weights[n].shape) for t in upd)
    small = [n for n in WEIGHTS if n not in BIG_WEIGHTS and n != 'mod_w']
    small_shapes = [weights[n].shape for n in small]
    upd = adam_update(*[_pack([src[n] for n in small]) for src in (weights, grads, mom1, mom2)], name="adam_small")
    for out, flat in zip((delta, new_m, new_v), upd):
        out.update(dict(zip(small, _unpack(flat, small_shapes))))

    return (loss_sum[0], dx[None], *[grads[n] for n in WEIGHTS], *[delta[n] for n in WEIGHTS],
            *[new_m[n] for n in WEIGHTS], *[new_v[n] for n in WEIGHTS])
```

```python
import functools

import jax
import jax.numpy as jnp
from jax import lax
from jax.experimental import pallas as pl
from jax.experimental.pallas import tpu as pltpu

F32 = jnp.float32
BF16 = jnp.bfloat16
MXU_DTYPE = BF16
ACT_DTYPE = BF16
WIRE_DTYPE = BF16

EPS = 1e-6
GRID_W = 64
CHUNK = 64
DN_CONV_OFFSETS = (-2, -1, 0, 1)
SC_CONV_OFFSETS = (-1, 0, 1)
LRU_C = 8.0
ADAM_LR, ADAM_B1, ADAM_B2, ADAM_EPS, ADAM_WD, ADAM_STEP = 0.001, 0.9, 0.999, 1e-08, 0.01, 10

LANES = 128
SUBLANES = 8
N_CHIPS, N_CORES = 4, 2
N_DEV = N_CHIPS * N_CORES
INV_PRECISION = None
INV_BLOCK = 16

_MESH = pl.DeviceIdType.MESH
_ANY = pl.BlockSpec(memory_space=pl.ANY)
_NN = (((1,), (0,)), ((), ()))
_NT = (((1,), (1,)), ((), ()))
_TN = (((0,), (0,)), ((), ()))


def _tile(n, cap, mult):
    best = None
    for t in range(mult, min(n, cap) + 1, mult):
        if n % t == 0:
            best = t
    return n if best is None else best


def _iota(shape, dim):
    return lax.broadcasted_iota(jnp.int32, shape, dim)


def _dot(a, b, dims):
    return lax.dot_general(a.astype(MXU_DTYPE), b.astype(MXU_DTYPE), dims, preferred_element_type=F32)


def _silu(x):
    return x * jax.nn.sigmoid(x)


def _dsilu(x):
    s = jax.nn.sigmoid(x)
    return s * (1.0 + x * (1.0 - s))


V7X_VMEM_BYTES = 64 * 1024 * 1024
BIG_KERNEL_VMEM = V7X_VMEM_BYTES * 15 // 16
MATMUL_TK = 4096
MATMUL_TK_TOKENS = 8192
MATMUL_VMEM = V7X_VMEM_BYTES * 7 // 8


def _params(*sem, vmem=None):
    return pltpu.CompilerParams(dimension_semantics=sem, vmem_limit_bytes=vmem)


def _place():
    return lax.axis_index("x"), lax.axis_index("y"), lax.axis_index("c")


def all_gather_devices(block, name):
    def body(x_ref, out_ref, send_sems, recv_sems, local_sem):
        x, y, c = _place()
        me, sibling = (x, y, c), (x, y, 1 - c)
        chips = [(1 - x, y), (x, 1 - y), (1 - x, 1 - y)]

        def slot(px, py, pc):
            return out_ref.at[4 * px + 2 * py + pc]

        def copy(k, block_of, to, src=None):
            return pltpu.make_async_remote_copy(
                src_ref=slot(*block_of) if src is None else src, dst_ref=slot(*block_of),
                send_sem=send_sems.at[k], recv_sem=recv_sems.at[k], device_id=to, device_id_type=_MESH)

        mine = pltpu.make_async_copy(x_ref, slot(*me), local_sem)
        mine.start()
        first = [copy(0, me, sibling, src=x_ref)]
        first += [copy(1 + j, me, (*chip, c), src=x_ref) for j, chip in enumerate(chips)]
        for cp in first:
            cp.start()
        passed = [copy(4 + j, (*chip, c), sibling) for j, chip in enumerate(chips)]
        for j, chip in enumerate(chips):
            copy(1 + j, (*chip, c), me).wait_recv()
            passed[j].start()
        copy(0, sibling, me).wait_recv()
        for j, chip in enumerate(chips):
            copy(4 + j, (*chip, 1 - c), me).wait_recv()
        for cp in first + passed:
            cp.wait_send()
        mine.wait()

    return pl.pallas_call(
        body, name=name,
        out_shape=jax.ShapeDtypeStruct((N_DEV,) + block.shape, block.dtype),
        in_specs=[_ANY], out_specs=_ANY,
        scratch_shapes=[pltpu.SemaphoreType.DMA((7,)), pltpu.SemaphoreType.DMA((7,)), pltpu.SemaphoreType.DMA],
    )(block)


def chip_exchange(srcs, *, gather, name):
    n = len(srcs)

    def body(*refs):
        copies = _chip_copies(refs[:n], refs[n:2 * n], refs[2 * n:], gather)
        for cp in copies:
            cp.start()
        for cp in copies:
            cp.wait()

    return pl.pallas_call(
        body, name=name, out_shape=_chip_out_shapes(srcs), in_specs=[_ANY] * n, out_specs=[_ANY] * n,
        scratch_shapes=_chip_sems(n),
    )(*srcs)


def _chip_copies(src, out, sems, gather):
    send_sems, recv_sems, local_sems = sems
    x, y, c = _place()
    my = 2 * x + y
    peers = [(1 - x, y), (x, 1 - y), (1 - x, 1 - y)]
    copies = []
    for k in range(len(src)):
        own = src[k].at[c] if gather else src[k].at[my]
        copies.append(pltpu.make_async_copy(own, out[k].at[my], local_sems.at[k]))
        for j, (px, py) in enumerate(peers):
            copies.append(pltpu.make_async_remote_copy(
                src_ref=own if gather else src[k].at[2 * px + py], dst_ref=out[k].at[my],
                send_sem=send_sems.at[k, j], recv_sem=recv_sems.at[k, j],
                device_id=(px, py, c), device_id_type=_MESH))
    return copies


def _chip_out_shapes(srcs):
    return [jax.ShapeDtypeStruct((N_CHIPS,) + s.shape[1:], s.dtype) for s in srcs]


def _chip_sems(n):
    return [pltpu.SemaphoreType.DMA((n, 3)), pltpu.SemaphoreType.DMA((n, 3)), pltpu.SemaphoreType.DMA((n,))]


class SideExchange:
    def __init__(self, srcs, gather):
        self.srcs, self.gather, self.n = list(srcs), gather, len(srcs)

    def specs(self):
        return [_ANY] * self.n, _chip_out_shapes(self.srcs), [_ANY] * self.n, _chip_sems(self.n)

    def run(self, src_refs, out_refs, sem_refs, step, last_step, compute):
        @pl.when(step == 0)
        def _():
            for cp in _chip_copies(src_refs, out_refs, sem_refs, self.gather):
                cp.start()

        compute()

        @pl.when(step == last_step)
        def _():
            for cp in _chip_copies(src_refs, out_refs, sem_refs, self.gather):
                cp.wait()


def sibling_exchange(srcs, *, pick, name):
    n = len(srcs)

    def body(*refs):
        src, out = refs[:n], refs[n:2 * n]
        send_sems, recv_sems = refs[2 * n:]
        x, y, c = _place()
        copies = []
        for k in range(n):
            s_ref = src[k].at[pl.ds(0, src[k].shape[0]), 1 - c] if pick else src[k]
            cp = pltpu.make_async_remote_copy(
                src_ref=s_ref, dst_ref=out[k], send_sem=send_sems.at[k], recv_sem=recv_sems.at[k],
                device_id=(x, y, 1 - c), device_id_type=_MESH)
            cp.start()
            copies.append(cp)
        for cp in copies:
            cp.wait()

    outs = [jax.ShapeDtypeStruct(s.shape[:1] + s.shape[2:] if pick else s.shape, s.dtype) for s in srcs]
    return pl.pallas_call(
        body, name=name, out_shape=outs, in_specs=[_ANY] * n, out_specs=[_ANY] * n,
        scratch_shapes=[pltpu.SemaphoreType.DMA((n,)), pltpu.SemaphoreType.DMA((n,))],
    )(*srcs)


def core_halves(core, mine, other):
    return jnp.where(core == 0, jnp.stack([mine, other], axis=1), jnp.stack([other, mine], axis=1))


def matmul(a, b, *, m, n, k, tm, tn, tk, a_spec, b_spec, dims, out_dtype, name,
           out_spec=None, out_shape=None, resid=None, gate=None, aux_dtype=None):
    nk = k // tk
    o_spec = out_spec or pl.BlockSpec((tm, tn), lambda i, j, kk: (i, j))
    o_shape = out_shape or (m, n)

    def body(*refs):
        a_ref, b_ref = refs[0], refs[1]
        pos = 2
        r_ref = g_ref = aux_ref = None
        if resid is not None:
            r_ref, pos = refs[pos], pos + 1
        if gate is not None:
            g_ref, pos = refs[pos], pos + 1
        o_ref, pos = refs[pos], pos + 1
        if aux_dtype is not None:
            aux_ref, pos = refs[pos], pos + 1
        prod = _dot(a_ref[...], b_ref[...], dims)

        def finish(y):
            if aux_ref is not None:
                aux_ref[...] = y.astype(aux_dtype)
            if g_ref is not None:
                y = y * g_ref[...]
            if r_ref is not None:
                y = y + r_ref[...]
            o_ref[...] = y.astype(out_dtype)

        if nk == 1:
            finish(prod)
            return
        acc = refs[pos]
        kk = pl.program_id(2)

        @pl.when(kk == 0)
        def _():
            acc[...] = prod

        @pl.when((kk > 0) & (kk < nk - 1))
        def _():
            acc[...] += prod

        @pl.when(kk == nk - 1)
        def _():
            finish(acc[...] + prod)

    ins, in_specs = [a, b], [a_spec, b_spec]
    if resid is not None:
        ins.append(resid)
        in_specs.append(pl.BlockSpec((tm, tn), lambda i, j, kk: (i, j)))
    if gate is not None:
        ins.append(gate)
        in_specs.append(pl.BlockSpec((1, tn), lambda i, j, kk: (0, j)))
    outs, out_specs = [jax.ShapeDtypeStruct(o_shape, out_dtype)], [o_spec]
    if aux_dtype is not None:
        outs.append(jax.ShapeDtypeStruct((m, n), aux_dtype))
        out_specs.append(pl.BlockSpec((tm, tn), lambda i, j, kk: (i, j)))
    res = pl.pallas_call(
        body, name=name, grid=(m // tm, n // tn, nk), in_specs=in_specs, out_specs=out_specs,
        out_shape=outs, scratch_shapes=[pltpu.VMEM((tm, tn), F32)] if nk > 1 else [],
        compiler_params=_params("parallel", "parallel", "arbitrary", vmem=MATMUL_VMEM),
    )(*ins)
    return res if aux_dtype is not None else res[0]


def mm_nn(a, b, *, out_dtype, name, tm_cap=1088, tn_cap=1024, tk_cap=MATMUL_TK, **kw):
    m, k = a.shape
    n = b.shape[1]
    tm, tn, tk = _tile(m, tm_cap, 16), _tile(n, tn_cap, LANES), _tile(k, tk_cap, LANES)
    return matmul(a, b, m=m, n=n, k=k, tm=tm, tn=tn, tk=tk, dims=_NN, out_dtype=out_dtype, name=name,
                  a_spec=pl.BlockSpec((tm, tk), lambda i, j, kk: (i, kk)),
                  b_spec=pl.BlockSpec((tk, tn), lambda i, j, kk: (kk, j)), **kw)


def mm_nt(a, b, *, out_dtype, name, tm_cap=1088, tn_cap=1024, tk_cap=MATMUL_TK // 2, **kw):
    m, k = a.shape
    n = b.shape[0]
    tm, tn, tk = _tile(m, tm_cap, 16), _tile(n, tn_cap, LANES), _tile(k, tk_cap, LANES)
    return matmul(a, b, m=m, n=n, k=k, tm=tm, tn=tn, tk=tk, dims=_NT, out_dtype=out_dtype, name=name,
                  a_spec=pl.BlockSpec((tm, tk), lambda i, j, kk: (i, kk)),
                  b_spec=pl.BlockSpec((tn, tk), lambda i, j, kk: (j, kk)), **kw)


def mm_tn(a, b, *, out_dtype, name, tm_cap=1024, tn_cap=1024, tk_cap=MATMUL_TK_TOKENS, **kw):
    k, m = a.shape
    n = b.shape[1]
    tm, tn, tk = _tile(m, tm_cap, LANES), _tile(n, tn_cap, LANES), _tile(k, tk_cap, 16)
    return matmul(a, b, m=m, n=n, k=k, tm=tm, tn=tn, tk=tk, dims=_TN, out_dtype=out_dtype, name=name,
                  a_spec=pl.BlockSpec((tk, tm), lambda i, j, kk: (kk, i)),
                  b_spec=pl.BlockSpec((tk, tn), lambda i, j, kk: (kk, j)), **kw)


def _rms(x):
    r = lax.rsqrt(jnp.mean(x * x, axis=-1, keepdims=True) + EPS)
    return x * r, r


def norm_mod_fwd(x, ctx, nw, scale2, shift2, *, name):
    s, d = x.shape
    cl = 0 if ctx is None else ctx.shape[0]
    tr = _tile(s if ctx is None else cl, 256, 16)
    n_lat = s // tr

    def body(*refs):
        if ctx is None:
            x_ref, nw_ref, sc_ref, sh_ref, o_ref = refs
            v = x_ref[...]
        else:
            x_ref, c_ref, nw_ref, sc_ref, sh_ref, o_ref = refs
            v = jnp.where(pl.program_id(0) < n_lat, x_ref[...], c_ref[...])
        y = _rms(v)[0] * nw_ref[...]
        o_ref[...] = (y * (1.0 + sc_ref[...]) + sh_ref[...]).astype(o_ref.dtype)

    sel = pl.BlockSpec((None, 1, d), lambda i: (i // n_lat, 0, 0))
    ins = [x] if ctx is None else [x, ctx]
    specs = [pl.BlockSpec((tr, d), lambda i: (jnp.minimum(i, n_lat - 1), 0))]
    if ctx is not None:
        specs.append(pl.BlockSpec((tr, d), lambda i: (jnp.maximum(i - n_lat, 0), 0)))
    return pl.pallas_call(
        body, name=name, grid=((s + cl) // tr,),
        in_specs=specs + [pl.BlockSpec((1, d), lambda i: (0, 0)), sel, sel],
        out_specs=pl.BlockSpec((tr, d), lambda i: (i, 0)),
        out_shape=jax.ShapeDtypeStruct((s + cl, d), ACT_DTYPE),
        compiler_params=_params("parallel"),
    )(*ins, nw, scale2, shift2)


def norm_mod_bwd(x, nw, scale, d_hn, *, row0, resid, init, name):
    r, d = x.shape
    tr = _tile(r, 256, 16)
    off = row0 // tr
    want_dx = resid is not None

    def body(*refs):
        x_ref, nw_ref, sc_ref, dh_ref = refs[:4]
        pos = 4
        res_ref = init_ref = dx_ref = None
        if want_dx:
            res_ref, pos = refs[pos], pos + 1
        if init is not None:
            init_ref, pos = refs[pos], pos + 1
        if want_dx:
            dx_ref, pos = refs[pos], pos + 1
        dnw_ref, dsc_ref, dsh_ref = refs[pos:pos + 3]
        i = pl.program_id(0)

        @pl.when(i == 0)
        def _():
            dnw_ref[...] = jnp.zeros_like(dnw_ref) if init_ref is None else init_ref[...]
            dsc_ref[...] = jnp.zeros_like(dsc_ref)
            dsh_ref[...] = jnp.zeros_like(dsh_ref)

        nrm, rs = _rms(x_ref[...])
        w = nw_ref[...]
        dh = dh_ref[...].astype(F32)
        dsh_ref[...] += jnp.sum(dh, axis=0, keepdims=True)
        dsc_ref[...] += jnp.sum(dh * (nrm * w), axis=0, keepdims=True)
        dy = dh * (1.0 + sc_ref[...])
        dnw_ref[...] += jnp.sum(dy * nrm, axis=0, keepdims=True)
        if want_dx:
            dn = dy * w
            dx = rs * (dn - nrm * jnp.mean(dn * nrm, axis=-1, keepdims=True))
            dx_ref[...] = dx + res_ref[...]

    row = pl.BlockSpec((tr, d), lambda i: (i, 0))
    vec = pl.BlockSpec((1, d), lambda i: (0, 0))
    ins, specs = [x, nw, scale, d_hn], [row, vec, vec, pl.BlockSpec((tr, d), lambda i: (i + off, 0))]
    if want_dx:
        ins.append(resid)
        specs.append(row)
    if init is not None:
        ins.append(init)
        specs.append(vec)
    vshape = jax.ShapeDtypeStruct((1, d), F32)
    outs, ospecs = [vshape] * 3, [vec] * 3
    if want_dx:
        outs, ospecs = [jax.ShapeDtypeStruct((r, d), F32)] + outs, [row] + ospecs
    res = pl.pallas_call(body, name=name, grid=(r // tr,), in_specs=specs, out_specs=ospecs, out_shape=outs,
                         compiler_params=_params("arbitrary"))(*ins)
    return tuple(res) if want_dx else (None,) + tuple(res)


def gate_bwd(dx, yo, gate, *, name):
    s, d = dx.shape
    tr = _tile(s, 256, 16)

    def body(dx_ref, yo_ref, g_ref, dyo_ref, dg_ref):
        @pl.when(pl.program_id(0) == 0)
        def _():
            dg_ref[...] = jnp.zeros_like(dg_ref)

        g = dx_ref[...]
        dg_ref[...] += jnp.sum(g * yo_ref[...].astype(F32), axis=0, keepdims=True)
        dyo_ref[...] = (g * g_ref[...]).astype(dyo_ref.dtype)

    row = pl.BlockSpec((tr, d), lambda i: (i, 0))
    vec = pl.BlockSpec((1, d), lambda i: (0, 0))
    return pl.pallas_call(
        body, name=name, grid=(s // tr,), in_specs=[row, row, vec], out_specs=[row, vec],
        out_shape=[jax.ShapeDtypeStruct((s, d), ACT_DTYPE), jax.ShapeDtypeStruct((1, d), F32)],
        compiler_params=_params("arbitrary"))(dx, yo, gate)


def loss_head(x, fw, target, *, name):
    s, d = x.shape
    tr = _tile(s, 256, 16)

    def body(x_ref, w_ref, t_ref, loss_ref, dx_ref, dw_ref):
        @pl.when(pl.program_id(0) == 0)
        def _():
            loss_ref[...] = jnp.zeros_like(loss_ref)
            dw_ref[...] = jnp.zeros_like(dw_ref)

        nrm, rs = _rms(x_ref[...])
        w = w_ref[...]
        err = nrm * w - t_ref[...]
        loss_ref[...] += 0.5 * jnp.sum(jnp.mean(err * err, axis=-1, keepdims=True))
        d_out = err * (1.0 / d)
        dw_ref[...] += jnp.sum(d_out * nrm, axis=0, keepdims=True)
        dn = d_out * w
        dx_ref[...] = rs * (dn - nrm * jnp.mean(dn * nrm, axis=-1, keepdims=True))

    row = pl.BlockSpec((tr, d), lambda i: (i, 0))
    vec = pl.BlockSpec((1, d), lambda i: (0, 0))
    return pl.pallas_call(
        body, name=name, grid=(s // tr,), in_specs=[row, vec, row],
        out_specs=[pl.BlockSpec((SUBLANES, LANES), lambda i: (0, 0)), row, vec],
        out_shape=[jax.ShapeDtypeStruct((SUBLANES, LANES), F32), jax.ShapeDtypeStruct((s, d), F32),
                   jax.ShapeDtypeStruct((1, d), F32)],
        compiler_params=_params("arbitrary"))(x, fw, target)


def _segments(rows, seg_a, seg_b):
    t = _iota((rows, 1), 0)
    if seg_b == 0:
        return t % seg_a, seg_a
    return jnp.where(t < seg_a, t, t - seg_a), jnp.where(t < seg_a, seg_a, seg_b)


def _shift(x, o, seg):
    if o == 0:
        return x
    pos, length = _segments(x.shape[0], *seg)
    y = pltpu.roll(x, (-o) % x.shape[0], 0)
    return jnp.where((pos + o >= 0) & (pos + o < length), y, 0.0)


def _conv(x, w, offsets, seg):
    acc = None
    for j, o in enumerate(offsets):
        term = w[j:j + 1, :] * _shift(x, o, seg)
        acc = term if acc is None else acc + term
    return acc


def _conv_bwd(x, w, dy, offsets, seg):
    dx = None
    dw = jnp.zeros(w.shape, F32)
    row = _iota(w.shape, 0)
    for j, o in enumerate(offsets):
        term = w[j:j + 1, :] * _shift(dy, -o, seg)
        dx = term if dx is None else dx + term
        dwj = jnp.sum(dy * _shift(x, o, seg), axis=0, keepdims=True)
        dw = dw + jnp.where(row == j, dwj, 0.0)
    return dx, dw


def _qkv_post(y, group, scale):
    a = _silu(y)
    n = a * lax.rsqrt(jnp.sum(a * a, axis=-1, keepdims=True) + EPS)
    return jnp.where(group == 0, n * scale, jnp.where(group == 1, n, a))


def qkv_conv_fwd(proj, conv_w, *, s, cl, heads, name):
    m = s + cl
    dh = LANES
    scale = dh ** -0.5

    def body(x_ref, w_ref, o_ref):
        group = pl.program_id(0) // heads
        y = _conv(x_ref[...], w_ref[...], DN_CONV_OFFSETS, (s, cl))
        o_ref[...] = _qkv_post(y, group, scale).astype(o_ref.dtype)

    return pl.pallas_call(
        body, name=name, grid=(3 * heads,),
        in_specs=[pl.BlockSpec((m, dh), lambda j: (0, j)), pl.BlockSpec((len(DN_CONV_OFFSETS), dh), lambda j: (0, j))],
        out_specs=pl.BlockSpec((m, dh), lambda j: (0, j)),
        out_shape=jax.ShapeDtypeStruct((m, 3 * heads * dh), ACT_DTYPE),
        compiler_params=_params("parallel"))(proj, conv_w)


def qkv_conv_bwd(proj, conv_w, dqkv, *, s, cl, heads, name):
    m = s + cl
    dh = LANES
    scale = dh ** -0.5
    kk = len(DN_CONV_OFFSETS)

    def body(x_ref, w_ref, d_ref, dx_ref, dw_ref):
        group = pl.program_id(0) // heads
        x, w = x_ref[...], w_ref[...]
        y = _conv(x, w, DN_CONV_OFFSETS, (s, cl))
        a = _silu(y)
        rn = lax.rsqrt(jnp.sum(a * a, axis=-1, keepdims=True) + EPS)
        n = a * rn
        dout = d_ref[...] * jnp.where(group == 0, scale, 1.0)
        da_norm = rn * (dout - n * jnp.sum(dout * n, axis=-1, keepdims=True))
        dy = jnp.where(group == 2, dout, da_norm) * _dsilu(y)
        dx, dw = _conv_bwd(x, w, dy, DN_CONV_OFFSETS, (s, cl))
        dx_ref[...] = dx.astype(dx_ref.dtype)
        dw_ref[...] = dw

    col = pl.BlockSpec((m, dh), lambda j: (0, j))
    wspec = pl.BlockSpec((kk, dh), lambda j: (0, j))
    return pl.pallas_call(
        body, name=name, grid=(3 * heads,),
        in_specs=[col, wspec, pl.BlockSpec((None, m, dh), lambda j: (j // heads, 0, j % heads))],
        out_specs=[col, wspec],
        out_shape=[jax.ShapeDtypeStruct((m, 3 * heads * dh), ACT_DTYPE),
                   jax.ShapeDtypeStruct((kk, 3 * heads * dh), F32)],
        compiler_params=_params("parallel"))(proj, conv_w, dqkv)


def _scan_masks(d):
    t, s = _iota((CHUNK, CHUNK), 0), _iota((CHUNK, CHUNK), 1)
    return ((s <= t), (s < t)) if d == 0 else ((s >= t), (s > t))


def _bmm(spec, a, b, precision=None):
    if precision is None:
        a, b = a.astype(MXU_DTYPE), b.astype(MXU_DTYPE)
    return jnp.einsum(spec, a, b, precision=precision, preferred_element_type=F32)


def _unit_tri_inverse(a):
    mm = functools.partial(_bmm, 'nts,nsr->ntr', precision=INV_PRECISION)
    row, col = _iota((CHUNK, CHUNK), 0), _iota((CHUNK, CHUNK), 1)
    eye = (row == col).astype(F32)
    dg = jnp.where(row // INV_BLOCK == col // INV_BLOCK, a, 0.0)
    off = a - dg
    p = eye - dg
    pw = dg
    for _ in range(3):
        pw = mm(pw, pw)
        p = p + mm(p, pw)
    n = mm(p, off)
    r = eye - n
    return mm(r + mm(r, mm(n, n)), p)


def _dn_intra(q, k, v, beta_b, gc_b, d):
    dh = q.shape[-1]
    incl, strict = _scan_masks(d)
    gc64 = gc_b[:, :, :CHUNK]
    diff = gc64 - jnp.swapaxes(gc64, 1, 2)
    decay = jnp.where(incl, jnp.exp(jnp.where(incl, diff, 0.0)), 0.0)
    qk_kk = _bmm('ntd,nsd->nts', jnp.concatenate([q, k], axis=1), k)
    qk, kk = qk_kk[:, :CHUNK], qk_kk[:, CHUNK:]
    a = jnp.where(strict, beta_b[:, :, :CHUNK] * kk * decay, 0.0)
    tinv = _unit_tri_inverse(a)
    rhs = jnp.concatenate([beta_b * jnp.exp(gc_b) * k, beta_b * v], axis=-1)
    wu = _bmm('nts,nsd->ntd', tinv, rhs, INV_PRECISION)
    w, u = wu[:, :, :dh], wu[:, :, dh:]
    last = CHUNK - 1 if d == 0 else 0
    gl = gc_b[:, last:last + 1, :]
    ke = k * jnp.exp(gl - gc_b)
    ge = jnp.exp(gl)
    return w, u, ke, ge, qk * decay, q * jnp.exp(gc_b)


def _dn_step(s, w, u, ke, ge, aqk, qg):
    ws_qs = _dot(jnp.concatenate([w, qg], axis=0), s, _NN)
    u2 = u - ws_qs[:CHUNK]
    s_new = ge * s + _dot(ke, u2, _TN)
    o = ws_qs[CHUNK:] + _dot(aqk, u2, _NN)
    return s_new, o


def _chunk_cumsum(x, d):
    rows = x.shape[0]
    pos = _iota((rows, 1), 0) % CHUNK
    step = 1
    while step < CHUNK:
        if d == 0:
            x = x + jnp.where(pos >= step, pltpu.roll(x, step, 0), 0.0)
        else:
            x = x + jnp.where(pos < CHUNK - step, pltpu.roll(x, rows - step, 0), 0.0)
        step *= 2
    return x


def _pick_lane(x, j):
    return jnp.sum(jnp.where(_iota(x.shape, 1) == j, x, 0.0), axis=1, keepdims=True)


def _dn_gates(ba, a_log, dt_bias, d, h, heads):
    braw = _pick_lane(ba, d * heads + h)
    araw = _pick_lane(ba, (2 + d) * heads + h)
    a_neg = -jnp.exp(_pick_lane(a_log[d:d + 1, :], h))
    pre = araw + _pick_lane(dt_bias[d:d + 1, :], h)
    return jax.nn.sigmoid(braw), a_neg * jax.nn.softplus(pre), pre, a_neg


_DN_SUB_FWD = 16
_DN_SUB_BWD = 8


def _for_sub_batches(s, cl, fn, sub=_DN_SUB_FWD):
    for base, total in ((0, s), (s, cl)):
        nch = min(sub, total // CHUNK)
        rows_per = nch * CHUNK
        count = total // rows_per

        def run(i, carry, base=base, nch=nch, rows_per=rows_per):
            row0 = pl.multiple_of(base + i * rows_per, rows_per)
            ge0 = pl.multiple_of((base // CHUNK + i * nch) * SUBLANES, nch * SUBLANES)
            fn(pl.ds(row0, rows_per), pl.ds(ge0, nch * SUBLANES), nch)
            return carry

        if count == 1:
            fn(pl.ds(base, rows_per), pl.ds(base // CHUNK * SUBLANES, nch * SUBLANES), nch)
        else:
            lax.fori_loop(0, count, run, 0)


def _dn_chunk_order(t, d, n_lat, n_ctx):
    if d == 0:
        return jnp.where(t < n_ctx, n_lat + t, t - n_ctx)
    return n_lat + n_ctx - 1 - t


def _dn_fill_intra(q_ref, k_ref, v_ref, bb_s, gc_s, w_s, u_s, ke_s, ge_s, aqk_s, qg_s, d, s, cl):
    dh = LANES

    def fill(rows, ge_rows, nch):
        def load(ref):
            return ref[rows, :].astype(F32).reshape(nch, CHUNK, dh)

        w, u, ke, ge, aqk, qg = _dn_intra(load(q_ref), load(k_ref), load(v_ref), load(bb_s), load(gc_s), d)
        w_s[rows, :] = w.reshape(nch * CHUNK, dh)
        u_s[rows, :] = u.reshape(nch * CHUNK, dh)
        ke_s[rows, :] = ke.reshape(nch * CHUNK, dh)
        qg_s[rows, :] = qg.reshape(nch * CHUNK, dh)
        aqk_s[rows, :] = aqk.reshape(nch * CHUNK, CHUNK)
        ge_s[ge_rows, :] = jnp.broadcast_to(ge, (nch, SUBLANES, dh)).reshape(nch * SUBLANES, dh)

    _for_sub_batches(s, cl, fill)


def _dn_chunk_refs(cid, w_s, u_s, ke_s, ge_s, aqk_s, qg_s):
    rows = pl.ds(pl.multiple_of(cid * CHUNK, CHUNK), CHUNK)
    ge = ge_s[pl.ds(pl.multiple_of(cid * SUBLANES, SUBLANES), SUBLANES), :][0:1]
    return rows, (w_s[rows, :], u_s[rows, :], ke_s[rows, :], ge, aqk_s[rows, :], qg_s[rows, :])


def _dn_scratch(m):
    dh = LANES
    big = pltpu.VMEM((m, dh), F32)
    return [big, big, big, big, big, pltpu.VMEM((m // CHUNK * SUBLANES, dh), F32), pltpu.VMEM((m, CHUNK), F32), big]


def _once(shape, index_map):
    return pl.BlockSpec(shape, index_map, pipeline_mode=pl.Buffered(1))


def dn_fwd(qkv, ba, a_log, dt_bias, side, *, s, cl, heads, name):
    m = s + cl
    dh = LANES
    n_lat, n_ctx = s // CHUNK, cl // CHUNK
    side_in, side_shapes, side_out, side_sems = side.specs()

    def body(*refs):
        q_ref, k_ref, v_ref, ba_ref, al_ref, dt_ref = refs[:6]
        pos = 6 + side.n
        o_ref = refs[pos]
        scratch = refs[pos + 1 + side.n:]
        side.run(refs[6:pos], refs[pos + 1:pos + 1 + side.n], scratch[8:], pl.program_id(0), heads - 1,
                 lambda: compute(q_ref, k_ref, v_ref, ba_ref, al_ref, dt_ref, o_ref, *scratch[:8]))

    def compute(q_ref, k_ref, v_ref, ba_ref, al_ref, dt_ref, o_ref, bb_s, gc_s, w_s, u_s, ke_s, ge_s, aqk_s, qg_s):
        h = pl.program_id(0)
        for d in (0, 1):
            beta, g, _, _ = _dn_gates(ba_ref[...], al_ref[...], dt_ref[...], d, h, heads)
            bb_s[...] = jnp.broadcast_to(beta, (m, dh))
            gc_s[...] = _chunk_cumsum(jnp.broadcast_to(g, (m, dh)), d)
            _dn_fill_intra(q_ref, k_ref, v_ref, bb_s, gc_s, w_s, u_s, ke_s, ge_s, aqk_s, qg_s, d, s, cl)

            def step(t, state):
                cid = _dn_chunk_order(t, d, n_lat, n_ctx)
                rows, terms = _dn_chunk_refs(cid, w_s, u_s, ke_s, ge_s, aqk_s, qg_s)
                state, o = _dn_step(state, *terms)

                @pl.when(cid < n_lat)
                def _():
                    if d == 0:
                        o_ref[rows, :] = o
                    else:
                        o_ref[rows, :] += o

                return state

            lax.fori_loop(0, n_lat + n_ctx, step, jnp.zeros((dh, dh), F32))

    def col(j0):
        return _once((m, dh), lambda h: (0, j0 + h))

    small = pl.BlockSpec((SUBLANES, LANES), lambda h: (0, 0))
    res = pl.pallas_call(
        body, name=name, grid=(heads,),
        in_specs=[col(0), col(heads), col(2 * heads), _once((m, LANES), lambda h: (0, 0)), small, small] + side_in,
        out_specs=[pl.BlockSpec((s, dh), lambda h: (0, h))] + side_out,
        out_shape=[jax.ShapeDtypeStruct((s, heads * dh), F32)] + side_shapes,
        scratch_shapes=_dn_scratch(m) + side_sems,
        compiler_params=_params("arbitrary", vmem=BIG_KERNEL_VMEM))(qkv, qkv, qkv, ba, a_log, dt_bias, *side.srcs)
    return res[0], res[1:]


def dn_out_fwd(o, proj, dn_norm, *, heads, gate_col0, name):
    s = o.shape[0]
    dh = LANES
    tr = _tile(s, 1024, 16)

    def body(o_ref, g_ref, nw_ref, y_ref):
        y_ref[...] = (_rms(o_ref[...])[0] * nw_ref[...] * _silu(g_ref[...])).astype(y_ref.dtype)

    blk = pl.BlockSpec((tr, dh), lambda i, h: (i, h))
    return pl.pallas_call(
        body, name=name, grid=(s // tr, heads),
        in_specs=[blk, pl.BlockSpec((tr, dh), lambda i, h: (i, gate_col0 + h)), pl.BlockSpec((1, dh), lambda i, h: (0, 0))],
        out_specs=blk, out_shape=jax.ShapeDtypeStruct((s, heads * dh), ACT_DTYPE),
        compiler_params=_params("parallel", "parallel"))(o, proj, dn_norm)


def dn_out_bwd(o, proj, d_y, dn_norm, *, heads, gate_col0, name):
    s = o.shape[0]
    dh = LANES
    tr = _tile(s, 1024, 16)

    def body(o_ref, g_ref, dy_ref, nw_ref, do_ref, dg_ref, dnw_ref):
        @pl.when((pl.program_id(0) == 0) & (pl.program_id(1) == 0))
        def _():
            dnw_ref[...] = jnp.zeros_like(dnw_ref)

        nrm, rs = _rms(o_ref[...])
        nw, gate, dy = nw_ref[...], g_ref[...], dy_ref[...]
        dg_ref[...] = (dy * (nrm * nw) * _dsilu(gate)).astype(dg_ref.dtype)
        dy0 = dy * _silu(gate)
        dnw_ref[0:1, :] += jnp.sum(dy0 * nrm, axis=0, keepdims=True)
        dn = dy0 * nw
        do_ref[...] = rs * (dn - nrm * jnp.mean(dn * nrm, axis=-1, keepdims=True))

    blk = pl.BlockSpec((tr, dh), lambda i, h: (i, h))
    return pl.pallas_call(
        body, name=name, grid=(s // tr, heads),
        in_specs=[blk, pl.BlockSpec((tr, dh), lambda i, h: (i, gate_col0 + h)), blk,
                  pl.BlockSpec((1, dh), lambda i, h: (0, 0))],
        out_specs=[blk, blk, pl.BlockSpec((SUBLANES, LANES), lambda i, h: (0, 0))],
        out_shape=[jax.ShapeDtypeStruct((s, heads * dh), F32), jax.ShapeDtypeStruct((s, heads * dh), ACT_DTYPE),
                   jax.ShapeDtypeStruct((SUBLANES, LANES), F32)],
        compiler_params=_params("arbitrary", "arbitrary"))(o, proj, d_y, dn_norm)


def dn_bwd(qkv, ba, d_o, a_log, dt_bias, side, *, s, cl, heads, name):
    m = s + cl
    dh = LANES
    n_lat, n_ctx = s // CHUNK, cl // CHUNK
    n_all = n_lat + n_ctx
    side_in, side_shapes, side_out, side_sems = side.specs()

    def body(*refs):
        ins, pos = refs[:7], 7 + side.n
        outs = refs[pos:pos + 4]
        scratch = refs[pos + 4 + side.n:]
        side.run(refs[7:pos], refs[pos + 4:pos + 4 + side.n], scratch[9:], pl.program_id(0), heads - 1,
                 lambda: compute(*ins, *outs, *scratch[:9]))

    def compute(q_ref, k_ref, v_ref, ba_ref, do_ref, al_ref, dt_ref, dqkv_ref, dba_ref, dal_ref, ddt_ref,
                bb_s, gc_s, w_s, u_s, ke_s, ge_s, aqk_s, qg_s, sall_s):
        h = pl.program_id(0)
        dq_ref, dk_ref, dv_ref = dqkv_ref.at[0], dqkv_ref.at[1], dqkv_ref.at[2]

        @pl.when(h == 0)
        def _():
            dba_ref[...] = jnp.zeros_like(dba_ref)

        lane = _iota((m, LANES), 1)
        for d in (0, 1):
            beta, g, pre, a_neg = _dn_gates(ba_ref[...], al_ref[...], dt_ref[...], d, h, heads)
            bb_s[...] = jnp.broadcast_to(beta, (m, dh))
            gc_s[...] = _chunk_cumsum(jnp.broadcast_to(g, (m, dh)), d)
            _dn_fill_intra(q_ref, k_ref, v_ref, bb_s, gc_s, w_s, u_s, ke_s, ge_s, aqk_s, qg_s, d, s, cl)

            def fwd_step(t, state):
                cid = _dn_chunk_order(t, d, n_lat, n_ctx)
                _, terms = _dn_chunk_refs(cid, w_s, u_s, ke_s, ge_s, aqk_s, qg_s)
                sall_s[pl.ds(pl.multiple_of(cid * dh, dh), dh), :] = state
                return _dn_step(state, *terms)[0]

            lax.fori_loop(0, n_all, fwd_step, jnp.zeros((dh, dh), F32))

            def bwd_step(i, dstate):
                cid = _dn_chunk_order(n_all - 1 - i, d, n_lat, n_ctx)
                rows, terms = _dn_chunk_refs(cid, w_s, u_s, ke_s, ge_s, aqk_s, qg_s)
                state = sall_s[pl.ds(pl.multiple_of(cid * dh, dh), dh), :]
                _, vjp = jax.vjp(_dn_step, state, *terms)
                lat_rows = pl.ds(pl.multiple_of(jnp.minimum(cid, n_lat - 1) * CHUNK, CHUNK), CHUNK)
                do = jnp.where(cid < n_lat, do_ref[lat_rows, :], 0.0)
                dstate, dw, du, dke, dge, daqk, dqg = vjp((dstate, do))
                w_s[rows, :] = dw
                u_s[rows, :] = du
                ke_s[rows, :] = dke
                qg_s[rows, :] = dqg
                aqk_s[rows, :] = daqk
                ge_s[pl.ds(pl.multiple_of(cid * SUBLANES, SUBLANES), SUBLANES), :] = jnp.broadcast_to(
                    dge, (SUBLANES, dh))
                return dstate

            lax.fori_loop(0, n_all, bwd_step, jnp.zeros((dh, dh), F32))

            def intra_bwd(rows, ge_rows, nch):
                def load(ref, width=dh):
                    return ref[rows, :].astype(F32).reshape(nch, CHUNK, width)

                _, vjp = jax.vjp(functools.partial(_dn_intra, d=d), load(q_ref), load(k_ref), load(v_ref),
                                 load(bb_s), load(gc_s))
                dge = ge_s[ge_rows, :].reshape(nch, SUBLANES, dh)[:, 0:1]
                dq, dk, dv, dbb, dgc = vjp((load(w_s), load(u_s), load(ke_s), dge, load(aqk_s, CHUNK), load(qg_s)))
                flat = lambda x: x.reshape(nch * CHUNK, dh)
                if d == 0:
                    dq_ref[rows, :], dk_ref[rows, :], dv_ref[rows, :] = flat(dq), flat(dk), flat(dv)
                else:
                    dq_ref[rows, :] += flat(dq)
                    dk_ref[rows, :] += flat(dk)
                    dv_ref[rows, :] += flat(dv)
                bb_s[rows, :] = flat(dbb)
                gc_s[rows, :] = flat(dgc)

            _for_sub_batches(s, cl, intra_bwd, _DN_SUB_BWD)

            dbeta = jnp.sum(bb_s[...], axis=1, keepdims=True)
            dg = jnp.sum(_chunk_cumsum(gc_s[...], 1 - d), axis=1, keepdims=True)
            dbraw = dbeta * beta * (1.0 - beta)
            dpre = dg * a_neg * jax.nn.sigmoid(pre)
            dba_ref[...] += (jnp.where(lane == d * heads + h, dbraw, 0.0)
                             + jnp.where(lane == (2 + d) * heads + h, dpre, 0.0))
            dal_ref[d:d + 1, :] = jnp.broadcast_to(jnp.sum(dg * g, axis=0, keepdims=True), (1, LANES))
            ddt_ref[d:d + 1, :] = jnp.broadcast_to(jnp.sum(dpre, axis=0, keepdims=True), (1, LANES))
        dal_ref[2:SUBLANES, :] = jnp.zeros((SUBLANES - 2, LANES), F32)
        ddt_ref[2:SUBLANES, :] = jnp.zeros((SUBLANES - 2, LANES), F32)

    def col(j0, rows=m):
        return _once((rows, dh), lambda h: (0, j0 + h))

    small = pl.BlockSpec((SUBLANES, LANES), lambda h: (0, 0))
    tile_h = pl.BlockSpec((None, SUBLANES, LANES), lambda h: (h, 0, 0))
    tiles = jax.ShapeDtypeStruct((heads, SUBLANES, LANES), F32)
    res = pl.pallas_call(
        body, name=name, grid=(heads,),
        in_specs=[col(0), col(heads), col(2 * heads), _once((m, LANES), lambda h: (0, 0)), col(0, s), small, small]
        + side_in,
        out_specs=[_once((3, m, dh), lambda h: (0, 0, h)), _once((m, LANES), lambda h: (0, 0)), tile_h, tile_h]
        + side_out,
        out_shape=[jax.ShapeDtypeStruct((3, m, heads * dh), F32), jax.ShapeDtypeStruct((m, LANES), F32), tiles, tiles]
        + side_shapes,
        scratch_shapes=_dn_scratch(m) + [pltpu.VMEM((n_all * dh, dh), F32)] + side_sems,
        compiler_params=_params("arbitrary", vmem=BIG_KERNEL_VMEM))(qkv, qkv, qkv, ba, d_o, a_log, dt_bias, *side.srcs)
    return res[0], res[1], res[2], res[3], res[4:]


def _lin_scan(a, b, d):
    rows = a.shape[0]
    t = _iota((rows, 1), 0)
    step = 1
    while step < rows:
        if d == 0:
            ok, sa, sb = t >= step, pltpu.roll(a, step, 0), pltpu.roll(b, step, 0)
        else:
            ok, sa, sb = t < rows - step, pltpu.roll(a, rows - step, 0), pltpu.roll(b, rows - step, 0)
        b = b + a * jnp.where(ok, sb, 0.0)
        a = a * jnp.where(ok, sa, 1.0)
        step *= 2
    return b


def _lru_gates(xc, w_r, b_r, w_i, b_i, lam):
    r = jax.nn.sigmoid(_dot(xc, w_r, _NN) + b_r)
    i = jax.nn.sigmoid(_dot(xc, w_i, _NN) + b_i)
    log_a = -LRU_C * r * jax.nn.softplus(-lam)
    z = 2.0 * log_a
    series = -(z * (1.0 + z * (0.5 + z * (1.0 / 6.0))))
    one_minus = jnp.where(z > -0.01, series, 1.0 - jnp.exp(z))
    return jnp.exp(log_a), jnp.sqrt(one_minus) * (i * xc)


def _lru_states(a, b, d, s, cl):
    ac, bc = a[s:], b[s:]
    hc = _lin_scan(ac, bc, d)
    h0 = hc[cl - 1:cl] if d == 0 else hc[0:1]
    first = 0 if d == 0 else s - 1
    al = a[:s]
    bl = b[:s] + jnp.where(_iota((s, 1), 0) == first, al * h0, 0.0)
    return _lin_scan(al, bl, d), hc, h0


def _lru_specs(m, nb_dim):
    j_col = lambda rows: pl.BlockSpec((rows, LANES), lambda j: (0, j))
    w_blk = pl.BlockSpec((2, None, nb_dim, nb_dim), lambda j: (0, j, 0, 0))
    return j_col, w_blk


def lru_fwd(xcm, conv_w, conv_b, w_r, b_r, w_i, b_i, lam, *, s, cl, name):
    m, width = xcm.shape
    j_col, w_blk = _lru_specs(m, w_r.shape[-1])

    def body(x_ref, cw_ref, cb_ref, wr_ref, br_ref, wi_ref, bi_ref, lam_ref, h_ref):
        xc = _conv(x_ref[...], cw_ref[...], DN_CONV_OFFSETS, (s, cl)) + cb_ref[...]
        for d in (0, 1):
            a, b = _lru_gates(xc, wr_ref[d], br_ref[d:d + 1, :], wi_ref[d], bi_ref[d:d + 1, :], lam_ref[d:d + 1, :])
            h = _lru_states(a, b, d, s, cl)[0]
            if d == 0:
                h_ref[...] = h
            else:
                h_ref[...] += h

    return pl.pallas_call(
        body, name=name, grid=(width // LANES,),
        in_specs=[_once((m, LANES), lambda j: (0, j)), j_col(len(DN_CONV_OFFSETS)), j_col(1), w_blk, j_col(2), w_blk,
                  j_col(2), j_col(2)],
        out_specs=j_col(s), out_shape=jax.ShapeDtypeStruct((s, width), F32),
        compiler_params=_params("parallel", vmem=BIG_KERNEL_VMEM))(xcm, conv_w, conv_b, w_r, b_r, w_i, b_i, lam)


def lru_bwd(xcm, d_h, conv_w, conv_b, w_r, b_r, w_i, b_i, lam, *, s, cl, name):
    m, width = xcm.shape
    nb_dim = w_r.shape[-1]
    j_col, w_blk = _lru_specs(m, nb_dim)

    def body(x_ref, dh_ref, cw_ref, cb_ref, wr_ref, br_ref, wi_ref, bi_ref, lam_ref,
             dx_ref, dcw_ref, dcb_ref, dwr_ref, dbr_ref, dwi_ref, dbi_ref, dlam_ref):
        x, cw = x_ref[...], cw_ref[...]
        xc = _conv(x, cw, DN_CONV_OFFSETS, (s, cl)) + cb_ref[...]
        d_hl = dh_ref[...]
        d_xc = None
        for d in (0, 1):
            (a, b), vjp = jax.vjp(_lru_gates, xc, wr_ref[d], br_ref[d:d + 1, :], wi_ref[d], bi_ref[d:d + 1, :],
                                  lam_ref[d:d + 1, :])
            h, hc, h0 = _lru_states(a, b, d, s, cl)
            nxt = 1 if d == 0 else -1
            first = 0 if d == 0 else s - 1
            al, ac = a[:s], a[s:]
            lam_l = _lin_scan(_shift(al, nxt, (s, 0)), d_hl, 1 - d)
            h_prev = _shift(h, -nxt, (s, 0)) + jnp.where(_iota((s, 1), 0) == first, h0, 0.0)
            d_h0 = (al * lam_l)[first:first + 1]
            last_c = cl - 1 if d == 0 else 0
            d_hc = jnp.where(_iota((cl, 1), 0) == last_c, d_h0, 0.0)
            lam_c = _lin_scan(_shift(ac, nxt, (cl, 0)), d_hc, 1 - d)
            da = jnp.concatenate([lam_l * h_prev, lam_c * _shift(hc, -nxt, (cl, 0))], axis=0)
            db = jnp.concatenate([lam_l, lam_c], axis=0)
            g_xc, g_wr, g_br, g_wi, g_bi, g_lam = vjp((da, db))
            d_xc = g_xc if d_xc is None else d_xc + g_xc
            dwr_ref[d], dwi_ref[d] = g_wr, g_wi
            dbr_ref[d:d + 1, :], dbi_ref[d:d + 1, :], dlam_ref[d:d + 1, :] = g_br, g_bi, g_lam
        dx, dcw = _conv_bwd(x, cw, d_xc, DN_CONV_OFFSETS, (s, cl))
        dx_ref[...] = dx.astype(dx_ref.dtype)
        dcw_ref[...] = dcw
        dcb_ref[...] = jnp.sum(d_xc, axis=0, keepdims=True)

    kk = len(DN_CONV_OFFSETS)
    vec2 = jax.ShapeDtypeStruct((2, width), F32)
    return pl.pallas_call(
        body, name=name, grid=(width // LANES,),
        in_specs=[_once((m, LANES), lambda j: (0, j)), _once((s, LANES), lambda j: (0, j)), j_col(kk), j_col(1),
                  w_blk, j_col(2), w_blk, j_col(2), j_col(2)],
        out_specs=[j_col(m), j_col(kk), j_col(1), w_blk, j_col(2), w_blk, j_col(2), j_col(2)],
        out_shape=[jax.ShapeDtypeStruct((m, width), ACT_DTYPE), jax.ShapeDtypeStruct((kk, width), F32),
                   jax.ShapeDtypeStruct((1, width), F32), jax.ShapeDtypeStruct(w_r.shape, F32), vec2,
                   jax.ShapeDtypeStruct(w_i.shape, F32), vec2, vec2],
        compiler_params=_params("parallel", vmem=BIG_KERNEL_VMEM))(xcm, d_h, conv_w, conv_b, w_r, b_r, w_i, b_i, lam)


def lru_gate_fwd(h, proj, *, gate_col0, name):
    s, width = h.shape
    tr, tc = _tile(s, 512, 16), _tile(width, 512, LANES)
    c0 = gate_col0 * LANES // tc

    def body(h_ref, g_ref, y_ref):
        y_ref[...] = (h_ref[...] * _silu(g_ref[...])).astype(y_ref.dtype)

    blk = pl.BlockSpec((tr, tc), lambda i, j: (i, j))
    return pl.pallas_call(
        body, name=name, grid=(s // tr, width // tc),
        in_specs=[blk, pl.BlockSpec((tr, tc), lambda i, j: (i, c0 + j))], out_specs=blk,
        out_shape=jax.ShapeDtypeStruct((s, width), ACT_DTYPE),
        compiler_params=_params("parallel", "parallel"))(h, proj)


def lru_gate_bwd(h, proj, d_y, *, gate_col0, dy_col0, name):
    s, width = h.shape
    tr, tc = _tile(s, 512, 16), _tile(width, 512, LANES)
    c0, y0 = gate_col0 * LANES // tc, dy_col0 * LANES // tc

    def body(h_ref, g_ref, dy_ref, dh_ref, dg_ref):
        g, dy = g_ref[...], dy_ref[...]
        dh_ref[...] = dy * _silu(g)
        dg_ref[...] = (dy * h_ref[...] * _dsilu(g)).astype(dg_ref.dtype)

    blk = pl.BlockSpec((tr, tc), lambda i, j: (i, j))
    return pl.pallas_call(
        body, name=name, grid=(s // tr, width // tc),
        in_specs=[blk, pl.BlockSpec((tr, tc), lambda i, j: (i, c0 + j)), pl.BlockSpec((tr, tc), lambda i, j: (i, y0 + j))],
        out_specs=[blk, blk],
        out_shape=[jax.ShapeDtypeStruct((s, width), F32), jax.ShapeDtypeStruct((s, width), ACT_DTYPE)],
        compiler_params=_params("parallel", "parallel"))(h, proj, d_y)


def _sc_parts(p, width):
    return [p[:, k * width:(k + 1) * width] for k in range(4)]


def sc_mix_fwd(p, conv_w, *, name):
    s, width = p.shape[0], conv_w.shape[1]
    tr = 2 * GRID_W

    def body(p_ref, w_ref, y_ref):
        b_g, c_g, x_in, gate = _sc_parts(p_ref[...], width)
        z = _conv(c_g * x_in, w_ref[...], SC_CONV_OFFSETS, (GRID_W, 0))
        y_ref[...] = (b_g * z * _silu(gate)).astype(y_ref.dtype)

    return pl.pallas_call(
        body, name=name, grid=(s // tr,),
        in_specs=[pl.BlockSpec((tr, 4 * width), lambda i: (i, 0)), pl.BlockSpec(conv_w.shape, lambda i: (0, 0))],
        out_specs=pl.BlockSpec((tr, width), lambda i: (i, 0)),
        out_shape=jax.ShapeDtypeStruct((s, width), ACT_DTYPE),
        compiler_params=_params("parallel"))(p, conv_w)


def sc_mix_bwd(p, d_y, conv_w, *, name):
    s, width = p.shape[0], conv_w.shape[1]
    tr = 2 * GRID_W

    def body(p_ref, dy_ref, w_ref, dp_ref, dw_ref):
        @pl.when(pl.program_id(0) == 0)
        def _():
            dw_ref[...] = jnp.zeros_like(dw_ref)

        b_g, c_g, x_in, gate = _sc_parts(p_ref[...], width)
        w, dy = w_ref[...], dy_ref[...]
        u = c_g * x_in
        z = _conv(u, w, SC_CONV_OFFSETS, (GRID_W, 0))
        sg = _silu(gate)
        du, dw = _conv_bwd(u, w, dy * b_g * sg, SC_CONV_OFFSETS, (GRID_W, 0))
        dw_ref[...] += dw
        parts = (dy * z * sg, du * x_in, du * c_g, dy * b_g * z * _dsilu(gate))
        for k, part in enumerate(parts):
            dp_ref[:, k * width:(k + 1) * width] = part.astype(dp_ref.dtype)

    return pl.pallas_call(
        body, name=name, grid=(s // tr,),
        in_specs=[pl.BlockSpec((tr, 4 * width), lambda i: (i, 0)), pl.BlockSpec((tr, width), lambda i: (i, 0)),
                  pl.BlockSpec(conv_w.shape, lambda i: (0, 0))],
        out_specs=[pl.BlockSpec((tr, 4 * width), lambda i: (i, 0)), pl.BlockSpec(conv_w.shape, lambda i: (0, 0))],
        out_shape=[jax.ShapeDtypeStruct((s, 4 * width), ACT_DTYPE), jax.ShapeDtypeStruct(conv_w.shape, F32)],
        compiler_params=_params("arbitrary"))(p, d_y, conv_w)


MOD_ROWS = 16


def mod_fwd(cond, mod_w, mod_b, *, name):
    nl, d, ns = mod_w.shape
    tn = _tile(ns, 512, LANES)

    def body(c_ref, w_ref, b_ref, o_ref):
        o_ref[...] = _dot(_silu(c_ref[...]), w_ref[...], _NN) + b_ref[...]

    return pl.pallas_call(
        body, name=name, grid=(nl, ns // tn),
        in_specs=[pl.BlockSpec((MOD_ROWS, d), lambda l, j: (0, 0)), pl.BlockSpec((None, d, tn), lambda l, j: (l, 0, j)),
                  pl.BlockSpec((None, 1, tn), lambda l, j: (l, 0, j))],
        out_specs=pl.BlockSpec((None, MOD_ROWS, tn), lambda l, j: (l, 0, j)),
        out_shape=jax.ShapeDtypeStruct((nl, MOD_ROWS, ns), F32),
        compiler_params=_params("parallel", "parallel"))(cond, mod_w, mod_b)


def _adamw(w, g, m, v):
    m = ADAM_B1 * m + (1.0 - ADAM_B1) * g
    v = ADAM_B2 * v + (1.0 - ADAM_B2) * (g * g)
    m_hat = m / (1.0 - ADAM_B1 ** ADAM_STEP)
    v_hat = v / (1.0 - ADAM_B2 ** ADAM_STEP)
    return -ADAM_LR * (m_hat / (jnp.sqrt(v_hat) + ADAM_EPS) + ADAM_WD * w), m, v


def mod_adam(cond, d_mod, w, m, v, *, name):
    nl, d, ns = w.shape
    tr, tn = _tile(d, 256, SUBLANES), _tile(ns, 1024, LANES)

    def body(c_ref, dm_ref, w_ref, m_ref, v_ref, g_ref, dl_ref, nm_ref, nv_ref, ds_ref):
        @pl.when(pl.program_id(2) == 0)
        def _():
            ds_ref[...] = jnp.zeros_like(ds_ref)

        wv, dm = w_ref[...], dm_ref[...]
        g = _dot(_silu(c_ref[...]), dm, _TN)
        ds_ref[...] += _dot(dm, wv, _NT)
        g_ref[...] = g
        dl_ref[...], nm_ref[...], nv_ref[...] = _adamw(wv, g, m_ref[...], v_ref[...])

    blk = pl.BlockSpec((None, tr, tn), lambda l, i, j: (l, i, j))
    full = jax.ShapeDtypeStruct(w.shape, F32)
    return pl.pallas_call(
        body, name=name, grid=(nl, d // tr, ns // tn),
        in_specs=[pl.BlockSpec((MOD_ROWS, tr), lambda l, i, j: (0, i)),
                  pl.BlockSpec((None, MOD_ROWS, tn), lambda l, i, j: (l, 0, j)), blk, blk, blk],
        out_specs=[blk, blk, blk, blk, pl.BlockSpec((None, MOD_ROWS, tr), lambda l, i, j: (l, 0, i))],
        out_shape=[full, full, full, full, jax.ShapeDtypeStruct((nl, MOD_ROWS, d), F32)],
        compiler_params=_params("parallel", "parallel", "arbitrary"))(cond, d_mod, w, m, v)


def _row_tile(rows, cols, itemsize, mult):
    return _tile(rows, max(mult, (2 << 20) // (cols * itemsize)), mult)


def adam_update(w, g, m, v, *, name):
    r, c = w.shape
    tr = _row_tile(r, c, 4, SUBLANES)

    def body(w_ref, g_ref, m_ref, v_ref, dl_ref, nm_ref, nv_ref):
        dl_ref[...], nm_ref[...], nv_ref[...] = _adamw(w_ref[...], g_ref[...], m_ref[...], v_ref[...])

    blk = pl.BlockSpec((tr, c), lambda i: (i, 0))
    return pl.pallas_call(
        body, name=name, grid=(r // tr,), in_specs=[blk] * 4, out_specs=[blk] * 3,
        out_shape=[jax.ShapeDtypeStruct((r, c), F32)] * 3, compiler_params=_params("parallel"))(w, g, m, v)


def cast_rows(x, dtype, *, name):
    r, c = x.shape
    tr = _row_tile(r, c, 4, 16)

    def body(x_ref, o_ref):
        o_ref[...] = x_ref[...].astype(dtype)

    blk = pl.BlockSpec((tr, c), lambda i: (i, 0))
    return pl.pallas_call(body, name=name, grid=(r // tr,), in_specs=[blk], out_specs=blk,
                          out_shape=jax.ShapeDtypeStruct((r, c), dtype), compiler_params=_params("parallel"))(x)


def add_sibling_half(core, mine, other, *, name):
    a, _, r, c = mine.shape
    tr = _row_tile(r, c, 4, 16)

    def body(core_ref, x_ref, y_ref, o_ref):
        o_ref[...] = (x_ref[...].astype(F32) + y_ref[...].astype(F32)).astype(o_ref.dtype)

    grid_spec = pltpu.PrefetchScalarGridSpec(
        num_scalar_prefetch=1, grid=(a, r // tr),
        in_specs=[pl.BlockSpec((None, None, tr, c), lambda k, i, core_ref: (k, core_ref[0], i, 0)),
                  pl.BlockSpec((None, tr, c), lambda k, i, core_ref: (k, i, 0))],
        out_specs=pl.BlockSpec((None, tr, c), lambda k, i, core_ref: (k, i, 0)))
    return pl.pallas_call(body, name=name, grid_spec=grid_spec, out_shape=jax.ShapeDtypeStruct((a, r, c), WIRE_DTYPE),
                          compiler_params=_params("parallel", "parallel"))(core, mine, other)


def sum_slots(x, *, name):
    n, r, c = x.shape
    tr = _row_tile(r, c * n, 4, 16)

    def body(x_ref, o_ref):
        acc = x_ref[0].astype(F32)
        for k in range(1, n):
            acc = acc + x_ref[k].astype(F32)
        o_ref[...] = acc

    return pl.pallas_call(
        body, name=name, grid=(r // tr,), in_specs=[pl.BlockSpec((n, tr, c), lambda i: (0, i, 0))],
        out_specs=pl.BlockSpec((tr, c), lambda i: (i, 0)), out_shape=jax.ShapeDtypeStruct((r, c), F32),
        compiler_params=_params("parallel"))(x)


def ctx_cond_grad(parts, c_ctx, *, name):
    def body(p_ref, c_ref, o_ref):
        acc = p_ref[0]
        for k in range(1, N_CHIPS):
            acc = acc + p_ref[k]
        o_ref[...] = acc * _dsilu(c_ref[...])

    return pl.pallas_call(body, name=name, out_shape=jax.ShapeDtypeStruct(c_ctx.shape, F32))(parts, c_ctx)


PACK_ROWS = 256


def _tile_rows(shape):
    n = 1
    for dim in shape:
        n *= dim
    return -(-n // (SUBLANES * LANES)) * SUBLANES


def _pack(arrs):
    parts = []
    for a in arrs:
        flat = a.reshape(-1).astype(F32)
        rows = _tile_rows(a.shape)
        parts.append(jnp.pad(flat, (0, rows * LANES - flat.shape[0])).reshape(rows, LANES))
    total = sum(t.shape[0] for t in parts)
    parts.append(jnp.zeros((-total % PACK_ROWS, LANES), F32))
    return jnp.concatenate(parts, axis=0)


def _unpack(flat, shapes):
    lead = flat.shape[:-2]
    outs, r0 = [], 0
    for shape in shapes:
        rows = _tile_rows(shape)
        n = 1
        for dim in shape:
            n *= dim
        piece = flat[..., r0:r0 + rows, :].reshape(lead + (rows * LANES,))[..., :n]
        outs.append(piece.reshape(lead + tuple(shape)))
        r0 += rows
    return outs


WEIGHTS = ('c_ctx', 'mod_w', 'mod_b', 'norm_w', 'ab_w_in', 'ab_qkv_conv', 'ab_a_log', 'ab_dt_bias', 'ab_dn_norm',
           'ab_lru_conv_w', 'ab_lru_conv_b', 'ab_lru_w_r', 'ab_lru_b_r', 'ab_lru_w_i', 'ab_lru_b_i', 'ab_lru_lambda',
           'ab_w_out', 'sc_w_in', 'sc_conv', 'sc_w_out', 'final_norm_w')
BIG_WEIGHTS = ('ab_w_in', 'ab_w_out', 'sc_w_in', 'sc_w_out')


def kernel(x, c, ctx, c_ctx, mod_w, mod_b, norm_w, ab_w_in, ab_qkv_conv, ab_a_log, ab_dt_bias, ab_dn_norm, ab_lru_conv_w, ab_lru_conv_b, ab_lru_w_r, ab_lru_b_r, ab_lru_w_i, ab_lru_b_i, ab_lru_lambda, ab_w_out, sc_w_in, sc_conv, sc_w_out, final_norm_w, loss_target, m_c_ctx, m_mod_w, m_mod_b, m_norm_w, m_ab_w_in, m_ab_qkv_conv, m_ab_a_log, m_ab_dt_bias, m_ab_dn_norm, m_ab_lru_conv_w, m_ab_lru_conv_b, m_ab_lru_w_r, m_ab_lru_b_r, m_ab_lru_w_i, m_ab_lru_b_i, m_ab_lru_lambda, m_ab_w_out, m_sc_w_in, m_sc_conv, m_sc_w_out, m_final_norm_w, v_c_ctx, v_mod_w, v_mod_b, v_norm_w, v_ab_w_in, v_ab_qkv_conv, v_ab_a_log, v_ab_dt_bias, v_ab_dn_norm, v_ab_lru_conv_w, v_ab_lru_conv_b, v_ab_lru_w_r, v_ab_lru_b_r, v_ab_lru_w_i, v_ab_lru_b_i, v_ab_lru_lambda, v_ab_w_out, v_sc_w_in, v_sc_conv, v_sc_w_out, v_final_norm_w):
    given = dict(locals())
    weights = {n: given[n] for n in WEIGHTS}
    mom1 = {n: given['m_' + n] for n in WEIGHTS}
    mom2 = {n: given['v_' + n] for n in WEIGHTS}

    xi, yi, ci = lax.axis_index("x"), lax.axis_index("y"), lax.axis_index("c")
    chip = 2 * xi + yi
    dev = 2 * chip + ci
    x2d, ctx2d, target = x[0], ctx[0], loss_target[0]
    s, d = x2d.shape
    cl = ctx2d.shape[0]
    heads = ab_a_log.shape[-1]
    wdn = heads * LANES
    nb = ab_lru_w_r.shape[2]
    wl = nb * ab_lru_w_r.shape[3]
    sc = sc_w_in.shape[-1]
    ab_out = wdn + wl
    off_beta = 3 * wdn + wl
    ab_state = off_beta + 4 * heads
    ab_in = ab_state + wdn + wl
    ni = ab_in // N_CHIPS
    ns = mod_w.shape[-1]
    grid_rows = s // GRID_W

    def to_col_major(t):
        return t.reshape(grid_rows, GRID_W, t.shape[-1]).swapaxes(0, 1).reshape(s, t.shape[-1])

    def to_raster(t):
        return t.reshape(GRID_W, grid_rows, t.shape[-1]).swapaxes(0, 1).reshape(s, t.shape[-1])

    def from_chips(t):
        return jnp.moveaxis(t[0::N_CORES], 0, 1).reshape(t.shape[1], -1)

    def own_columns(t, width):
        return lax.dynamic_slice_in_dim(t, chip * width, width, axis=t.ndim - 1)

    small_shards = [c[0], ab_qkv_conv[0], ab_lru_conv_w[0], ab_lru_b_r[0], ab_lru_b_i[0], ab_lru_lambda[0], sc_conv[0]]
    gathered0 = all_gather_devices(_pack(small_shards), "ag_small_params")
    c_all, qkv_sh, lcw_sh, lbr_sh, lbi_sh, llam_sh, scv_sh = _unpack(gathered0, [t.shape for t in small_shards])
    qkv_conv, lru_conv_w, sc_conv_w = from_chips(qkv_sh), from_chips(lcw_sh), from_chips(scv_sh)
    lru_b_r, lru_b_i, lru_lam = from_chips(lbr_sh), from_chips(lbi_sh), from_chips(llam_sh)

    shards = [weights[n][0] for n in BIG_WEIGHTS]
    halves = [cast_rows(t, WIRE_DTYPE, name=f"cast_w{k}").reshape(N_CORES, t.shape[0] // N_CORES, t.shape[1])
              for k, t in enumerate(shards)]
    from_chips_half = chip_exchange(halves[:2], gather=True, name="ag_w_chips")
    from_sibling_half = sibling_exchange(from_chips_half, pick=False, name="ag_w_cores")
    full = [core_halves(ci, a, b) for a, b in zip(from_chips_half, from_sibling_half)]
    w_in_full = jnp.moveaxis(full[0].reshape(N_CHIPS, d, ni), 0, 1).reshape(d, ab_in)
    w_main = jnp.concatenate([w_in_full[:, :off_beta], w_in_full[:, ab_state:]], axis=1)
    w_ba = jnp.pad(w_in_full[:, off_beta:ab_state], ((0, 0), (0, LANES - 4 * heads)))
    w_ab_out = full[1].reshape(ab_out, d)

    cond = jnp.zeros((MOD_ROWS, d), F32).at[:N_DEV].set(c_all).at[N_DEV].set(c_ctx)
    mod_shard = mod_fwd(cond, mod_w, own_columns(mod_b, ns)[:, None, :], name="mod_fwd")
    gathered_mod = all_gather_devices(mod_shard.reshape(-1, LANES), "ag_mod")
    mod_all = jnp.moveaxis(gathered_mod[0::N_CORES].reshape(N_CHIPS, 2, MOD_ROWS, ns), 0, 2).reshape(2, MOD_ROWS, 3 * d)
    own_mod = lax.dynamic_index_in_dim(mod_all, dev, axis=1, keepdims=False)
    shift, scale, gate = own_mod[:, :d], own_mod[:, d:2 * d], own_mod[:, 2 * d:]
    shift_c, scale_c = mod_all[0, N_DEV, :d], mod_all[0, N_DEV, d:2 * d]

    def pair(a, b):
        return jnp.stack([a, b])[:, None, :]

    hn_all = norm_mod_fwd(x2d, ctx2d, norm_w[0:1], pair(scale[0], scale_c), pair(shift[0], shift_c), name="norm0_fwd")
    proj = mm_nn(hn_all, w_main, out_dtype=F32, name="proj0")
    ba = mm_nn(hn_all, w_ba, out_dtype=F32, name="proj0_ba")
    qkv = qkv_conv_fwd(proj, qkv_conv, s=s, cl=cl, heads=heads, name="qkv_conv_fwd")

    def pad_dh(t):
        return jnp.zeros((SUBLANES, LANES), F32).at[:2, :heads].set(t)

    a_log_t, dt_bias_t = pad_dh(ab_a_log[0]), pad_dh(ab_dt_bias[0])
    gate_dn0 = (3 * wdn + wl) // LANES
    gate_lru0 = gate_dn0 + wdn // LANES
    o_dn, sc_from_chips = dn_fwd(qkv, ba, a_log_t, dt_bias_t, SideExchange(halves[2:], gather=True), s=s, cl=cl,
                                 heads=heads, name="dn_fwd")
    sc_from_sibling = sibling_exchange(list(sc_from_chips), pick=False, name="ag_w_cores_sc")
    w_sc_in, w_sc_out = (core_halves(ci, a, b) for a, b in zip(sc_from_chips, sc_from_sibling))
    w_sc_in, w_sc_out = w_sc_in.reshape(N_CHIPS, d, sc), w_sc_out.reshape(sc, d)
    y_dn = dn_out_fwd(o_dn, proj, ab_dn_norm, heads=heads, gate_col0=gate_dn0, name="dn_out_fwd")
    lru_in = proj[:, 3 * wdn:3 * wdn + wl]
    xcm = jnp.concatenate([to_col_major(lru_in[:s]), lru_in[s:]], axis=0)
    lru_w = (lru_conv_w, ab_lru_conv_b, ab_lru_w_r[0], lru_b_r, ab_lru_w_i[0], lru_b_i, lru_lam)
    h_lru = to_raster(lru_fwd(xcm, *lru_w, s=s, cl=cl, name="lru_fwd"))
    y_lru = lru_gate_fwd(h_lru, proj, gate_col0=gate_lru0, name="lru_gate_fwd")
    y_ab = jnp.concatenate([y_dn, y_lru], axis=1)
    x1, yo0 = mm_nn(y_ab, w_ab_out, out_dtype=F32, name="out0", resid=x2d, gate=gate[0:1], aux_dtype=ACT_DTYPE,
                    tn_cap=512)

    hn1 = norm_mod_fwd(x1, None, norm_w[1:2], scale[1][None, None, :], shift[1][None, None, :], name="norm1_fwd")
    tm1, tn1, tk1 = _tile(s, 1024, 16), _tile(sc, 1024, LANES), _tile(d, MATMUL_TK, LANES)
    p = matmul(hn1, w_sc_in, m=s, n=4 * sc, k=d, tm=tm1, tn=tn1, tk=tk1, dims=_NN, out_dtype=F32, name="proj1",
               a_spec=pl.BlockSpec((tm1, tk1), lambda i, j, kk: (i, kk)),
               b_spec=pl.BlockSpec((None, tk1, tn1), lambda i, j, kk: (j // (sc // tn1), kk, j % (sc // tn1))))
    y_sc = sc_mix_fwd(p, sc_conv_w, name="sc_mix_fwd")
    x2, yo1 = mm_nn(y_sc, w_sc_out, out_dtype=F32, name="out1", resid=x1, gate=gate[1:2], aux_dtype=ACT_DTYPE,
                    tn_cap=512)
    loss_t, dx2, d_fnw = loss_head(x2, final_norm_w[None], target, name="loss_head")

    d_yo1, d_gate1 = gate_bwd(dx2, yo1, gate[1:2], name="gate1_bwd")
    d_ysc = mm_nt(d_yo1, w_sc_out, out_dtype=F32, name="out1_dx")
    dw_sc_out = mm_tn(y_sc, d_yo1, out_dtype=WIRE_DTYPE, name="out1_dw")
    d_p, d_sc_conv = sc_mix_bwd(p, d_ysc, sc_conv_w, name="sc_mix_bwd")
    tkp = _tile(sc, MATMUL_TK, LANES)
    tnd = _tile(d, 1024, LANES)
    d_hn1 = matmul(d_p, w_sc_in, m=s, n=d, k=4 * sc, tm=tm1, tn=tnd, tk=tkp, dims=_NT, out_dtype=F32, name="proj1_dx",
                   a_spec=pl.BlockSpec((tm1, tkp), lambda i, j, kk: (i, kk)),
                   b_spec=pl.BlockSpec((None, tnd, tkp), lambda i, j, kk: (kk // (sc // tkp), j, kk % (sc // tkp))))
    tks = _tile(s, MATMUL_TK_TOKENS, 16)
    dw_sc_in = matmul(hn1, d_p, m=d, n=4 * sc, k=s, tm=tnd, tn=tn1, tk=tks, dims=_TN, out_dtype=WIRE_DTYPE,
                      name="proj1_dw", a_spec=pl.BlockSpec((tks, tnd), lambda i, j, kk: (kk, i)),
                      b_spec=pl.BlockSpec((tks, tn1), lambda i, j, kk: (kk, j)),
                      out_spec=pl.BlockSpec((None, tnd, tn1), lambda i, j, kk: (j // (sc // tn1), i, j % (sc // tn1))),
                      out_shape=(N_CHIPS, d, sc))
    dx1, d_nw1, d_scale1, d_shift1 = norm_mod_bwd(x1, norm_w[1:2], scale[1:2], d_hn1, row0=0, resid=dx2, init=None,
                                                  name="norm1_bwd")

    d_yo0, d_gate0 = gate_bwd(dx1, yo0, gate[0:1], name="gate0_bwd")
    d_y = mm_nt(d_yo0, w_ab_out, out_dtype=F32, name="out0_dx")
    dw_ab_out = mm_tn(y_ab, d_yo0, out_dtype=WIRE_DTYPE, name="out0_dw")
    d_o, d_gate_dn, d_dn_norm = dn_out_bwd(o_dn, proj, d_y, ab_dn_norm, heads=heads, gate_col0=gate_dn0, name="dn_out_bwd")
    core = ci.astype(jnp.int32).reshape(1)

    def chip_core_slots(t, slot_rows):
        return t.reshape(N_CHIPS, N_CORES, slot_rows // N_CORES, t.shape[-1])

    sc_by_chip = [chip_core_slots(dw_sc_in, d), chip_core_slots(dw_sc_out, sc // N_CHIPS)]
    sc_from_sib = sibling_exchange(sc_by_chip, pick=True, name="rs_cores_sc")
    sc_halves_sum = [add_sibling_half(core, a, b, name=f"rs_add_sc{k}") for k, (a, b) in enumerate(zip(sc_by_chip, sc_from_sib))]
    d_qkv, d_ba, d_alog_t, d_dtb_t, sc_chips_sum = dn_bwd(qkv, ba, d_o, a_log_t, dt_bias_t,
                                                          SideExchange(sc_halves_sum, gather=False), s=s, cl=cl,
                                                          heads=heads, name="dn_bwd")
    d_qkv_raw, d_qkv_conv = qkv_conv_bwd(proj, qkv_conv, d_qkv, s=s, cl=cl, heads=heads, name="qkv_conv_bwd")
    d_h, d_gate_lru = lru_gate_bwd(h_lru, proj, d_y, gate_col0=gate_lru0, dy_col0=wdn // LANES, name="lru_gate_bwd")
    (d_xcm, d_lcw, d_lcb, d_wr, d_br, d_wi, d_bi, d_lam) = lru_bwd(xcm, to_col_major(d_h), *lru_w, s=s, cl=cl,
                                                                  name="lru_bwd")
    d_lru_in = jnp.concatenate([to_raster(d_xcm[:s]), d_xcm[s:]], axis=0)
    ctx_zeros = jnp.zeros((cl, wdn + wl), ACT_DTYPE)
    d_gates = jnp.concatenate([jnp.concatenate([d_gate_dn, d_gate_lru], axis=1), ctx_zeros], axis=0)
    d_proj = jnp.concatenate([d_qkv_raw, d_lru_in, d_gates], axis=1)
    d_hn_ba = mm_nt(d_ba, w_ba, out_dtype=F32, name="proj0_ba_dx")
    d_hn_all = mm_nt(d_proj, w_main, out_dtype=F32, name="proj0_dx", resid=d_hn_ba)
    dw_main = mm_tn(hn_all, d_proj, out_dtype=WIRE_DTYPE, name="proj0_dw")
    dw_ba = mm_tn(hn_all, d_ba, out_dtype=WIRE_DTYPE, name="proj0_ba_dw")
    _, d_nw0c, d_scale_c, d_shift_c = norm_mod_bwd(ctx2d, norm_w[0:1], scale_c[None], d_hn_all, row0=s, resid=None,
                                                   init=None, name="norm0_bwd_ctx")
    dx, d_nw0, d_scale0, d_shift0 = norm_mod_bwd(x2d, norm_w[0:1], scale[0:1], d_hn_all, row0=0, resid=dx1, init=d_nw0c,
                                                 name="norm0_bwd")

    dw_in_full = jnp.concatenate([dw_main[:, :off_beta], dw_ba[:, :4 * heads], dw_main[:, off_beta:]], axis=1)
    by_chip = [chip_core_slots(jnp.moveaxis(dw_in_full.reshape(d, N_CHIPS, ni), 1, 0), d),
               chip_core_slots(dw_ab_out, ab_out // N_CHIPS)]
    from_sibling = sibling_exchange(by_chip, pick=True, name="rs_cores")
    halves_sum = [add_sibling_half(core, a, b, name=f"rs_add{k}") for k, (a, b) in enumerate(zip(by_chip, from_sibling))]
    from_chips_sum = list(chip_exchange(halves_sum, gather=False, name="rs_chips")) + list(sc_chips_sum)
    reduced = [sum_slots(t, name=f"rs_sum{k}")[None] for k, t in enumerate(from_chips_sum)]
    both_halves = [core_halves(ci, a, b) for a, b in zip(reduced, sibling_exchange(reduced, pick=False, name="rs_back"))]
    grads = {n: t.reshape(weights[n].shape) for n, t in zip(BIG_WEIGHTS, both_halves)}

    d_mod_own = jnp.stack([jnp.concatenate([d_shift0[0], d_scale0[0], d_gate0[0]]),
                           jnp.concatenate([d_shift1[0], d_scale1[0], d_gate1[0]])])
    d_mod_ctx = jnp.concatenate([d_shift_c[0], d_scale_c[0], jnp.zeros((d,), F32)])
    summable = [
        loss_t[0, 0:1], d_mod_own.at[0].add(d_mod_ctx), d_mod_ctx, jnp.concatenate([d_nw0, d_nw1], axis=0), d_qkv_conv,
        d_alog_t[:, :2, 0].T, d_dtb_t[:, :2, 0].T, d_dn_norm[0], d_lcw, d_lcb[0], d_wr, d_br, d_wi, d_bi, d_lam,
        d_sc_conv, d_fnw[0]]
    sum_shapes = [t.shape for t in summable]
    gathered1 = all_gather_devices(_pack(summable + [d_mod_own]), "ag_small_grads")
    totals = _unpack(sum_slots(gathered1, name="sum_small_grads"), sum_shapes)
    (loss_sum, g_mod_b, d_mod_ctx_sum, g_norm_w, g_qkv_conv, g_a_log, g_dt_bias, g_dn_norm, g_lcw, g_lcb, g_wr, g_br,
     g_wi, g_bi, g_lam, g_sc_conv, g_fnw) = totals
    d_mod_rows = _unpack(gathered1, sum_shapes + [d_mod_own.shape])[-1]

    d_mod_all = jnp.zeros((2, MOD_ROWS, 3 * d), F32).at[:, :N_DEV].set(jnp.moveaxis(d_mod_rows, 0, 1))
    d_mod_all = d_mod_all.at[0, N_DEV].set(d_mod_ctx_sum)
    g_mod_w, dl_mod_w, nm_mod_w, nv_mod_w, d_silu = mod_adam(cond, own_columns(d_mod_all, ns), mod_w, m_mod_w, v_mod_w,
                                                             name="mod_adam")
    gathered2 = all_gather_devices(d_silu[0, N_DEV].reshape(-1, LANES), "ag_ctx_cond")
    g_c_ctx = ctx_cond_grad(gathered2[0::N_CORES], c_ctx.reshape(-1, LANES), name="ctx_cond_grad").reshape(d)

    grads.update({
        'c_ctx': g_c_ctx, 'mod_w': g_mod_w, 'mod_b': g_mod_b, 'norm_w': g_norm_w,
        'ab_qkv_conv': own_columns(g_qkv_conv, qkv_conv.shape[1] // N_CHIPS), 'ab_a_log': g_a_log, 'ab_dt_bias': g_dt_bias,
        'ab_dn_norm': g_dn_norm, 'ab_lru_conv_w': own_columns(g_lcw, wl // N_CHIPS), 'ab_lru_conv_b': g_lcb,
        'ab_lru_w_r': g_wr, 'ab_lru_b_r': own_columns(g_br, wl // N_CHIPS), 'ab_lru_w_i': g_wi,
        'ab_lru_b_i': own_columns(g_bi, wl // N_CHIPS), 'ab_lru_lambda': own_columns(g_lam, wl // N_CHIPS),
        'sc_conv': own_columns(g_sc_conv, sc // N_CHIPS), 'final_norm_w': g_fnw})
    grads = {n: grads[n].reshape(weights[n].shape) for n in WEIGHTS}

    delta, new_m, new_v = {'mod_w': dl_mod_w}, {'mod_w': nm_mod_w}, {'mod_w': nv_mod_w}
    for k, n in enumerate(BIG_WEIGHTS):
        as2d = lambda t: t.reshape(-1, t.shape[-1])
        upd = adam_update(as2d(weights[n]), as2d(grads[n]), as2d(mom1[n]), as2d(mom2[n]), name=f"adam_big{k}")
        delta[n], new_m[n], new_v[n] = (t.reshape(weights[n].shape) for t in upd)
    small = [n for n in WEIGHTS if n not in BIG_WEIGHTS and n != 'mod_w']
    small_shapes = [weights[n].shape for n in small]
    upd = adam_update(*[_pack([src[n] for n in small]) for src in (weights, grads, mom1, mom2)], name="adam_small")
    for out, flat in zip((delta, new_m, new_v), upd):
        out.update(dict(zip(small, _unpack(flat, small_shapes))))

    return (loss_sum[0], dx[None], *[grads[n] for n in WEIGHTS], *[delta[n] for n in WEIGHTS],
            *[new_m[n] for n in WEIGHTS], *[new_v[n] for n in WEIGHTS])
```

```python
import functools

import jax
import jax.numpy as jnp
from jax import lax
from jax.experimental import pallas as pl
from jax.experimental.pallas import tpu as pltpu

F32 = jnp.float32
BF16 = jnp.bfloat16
MXU_DTYPE = BF16
ACT_DTYPE = BF16
WIRE_DTYPE = BF16

EPS = 1e-6
GRID_W = 64
CHUNK = 64
DN_CONV_OFFSETS = (-2, -1, 0, 1)
SC_CONV_OFFSETS = (-1, 0, 1)
LRU_C = 8.0
ADAM_LR, ADAM_B1, ADAM_B2, ADAM_EPS, ADAM_WD, ADAM_STEP = 0.001, 0.9, 0.999, 1e-08, 0.01, 10

LANES = 128
SUBLANES = 8
N_CHIPS, N_CORES = 4, 2
N_DEV = N_CHIPS * N_CORES
INV_PRECISION = None
INV_BLOCK = 16

_MESH = pl.DeviceIdType.MESH
_ANY = pl.BlockSpec(memory_space=pl.ANY)
_NN = (((1,), (0,)), ((), ()))
_NT = (((1,), (1,)), ((), ()))
_TN = (((0,), (0,)), ((), ()))


def _tile(n, cap, mult):
    best = None
    for t in range(mult, min(n, cap) + 1, mult):
        if n % t == 0:
            best = t
    return n if best is None else best


def _iota(shape, dim):
    return lax.broadcasted_iota(jnp.int32, shape, dim)


def _dot(a, b, dims):
    return lax.dot_general(a.astype(MXU_DTYPE), b.astype(MXU_DTYPE), dims, preferred_element_type=F32)


def _silu(x):
    return x * jax.nn.sigmoid(x)


def _dsilu(x):
    s = jax.nn.sigmoid(x)
    return s * (1.0 + x * (1.0 - s))


V7X_VMEM_BYTES = 64 * 1024 * 1024
BIG_KERNEL_VMEM = V7X_VMEM_BYTES * 15 // 16
MATMUL_TK = 4096
MATMUL_TK_TOKENS = 8192
MATMUL_VMEM = V7X_VMEM_BYTES * 7 // 8


def _params(*sem, vmem=None):
    return pltpu.CompilerParams(dimension_semantics=sem, vmem_limit_bytes=vmem)


def _place():
    return lax.axis_index("x"), lax.axis_index("y"), lax.axis_index("c")


def all_gather_devices(block, name):
    def body(x_ref, out_ref, send_sems, recv_sems, local_sem):
        x, y, c = _place()
        me, sibling = (x, y, c), (x, y, 1 - c)
        chips = [(1 - x, y), (x, 1 - y), (1 - x, 1 - y)]

        def slot(px, py, pc):
            return out_ref.at[4 * px + 2 * py + pc]

        def copy(k, block_of, to, src=None):
            return pltpu.make_async_remote_copy(
                src_ref=slot(*block_of) if src is None else src, dst_ref=slot(*block_of),
                send_sem=send_sems.at[k], recv_sem=recv_sems.at[k], device_id=to, device_id_type=_MESH)

        mine = pltpu.make_async_copy(x_ref, slot(*me), local_sem)
        mine.start()
        first = [copy(0, me, sibling, src=x_ref)]
        first += [copy(1 + j, me, (*chip, c), src=x_ref) for j, chip in enumerate(chips)]
        for cp in first:
            cp.start()
        passed = [copy(4 + j, (*chip, c), sibling) for j, chip in enumerate(chips)]
        for j, chip in enumerate(chips):
            copy(1 + j, (*chip, c), me).wait_recv()
            passed[j].start()
        copy(0, sibling, me).wait_recv()
        for j, chip in enumerate(chips):
            copy(4 + j, (*chip, 1 - c), me).wait_recv()
        for cp in first + passed:
            cp.wait_send()
        mine.wait()

    return pl.pallas_call(
        body, name=name,
        out_shape=jax.ShapeDtypeStruct((N_DEV,) + block.shape, block.dtype),
        in_specs=[_ANY], out_specs=_ANY,
        scratch_shapes=[pltpu.SemaphoreType.DMA((7,)), pltpu.SemaphoreType.DMA((7,)), pltpu.SemaphoreType.DMA],
    )(block)


def chip_exchange(srcs, *, gather, name):
    n = len(srcs)

    def body(*refs):
        copies = _chip_copies(refs[:n], refs[n:2 * n], refs[2 * n:], gather)
        for cp in copies:
            cp.start()
        for cp in copies:
            cp.wait()

    return pl.pallas_call(
        body, name=name, out_shape=_chip_out_shapes(srcs), in_specs=[_ANY] * n, out_specs=[_ANY] * n,
        scratch_shapes=_chip_sems(n),
    )(*srcs)


def _chip_copies(src, out, sems, gather):
    send_sems, recv_sems, local_sems = sems
    x, y, c = _place()
    my = 2 * x + y
    peers = [(1 - x, y), (x, 1 - y), (1 - x, 1 - y)]
    copies = []
    for k in range(len(src)):
        own = src[k].at[c] if gather else src[k].at[my]
        copies.append(pltpu.make_async_copy(own, out[k].at[my], local_sems.at[k]))
        for j, (px, py) in enumerate(peers):
            copies.append(pltpu.make_async_remote_copy(
                src_ref=own if gather else src[k].at[2 * px + py], dst_ref=out[k].at[my],
                send_sem=send_sems.at[k, j], recv_sem=recv_sems.at[k, j],
                device_id=(px, py, c), device_id_type=_MESH))
    return copies


def _chip_out_shapes(srcs):
    return [jax.ShapeDtypeStruct((N_CHIPS,) + s.shape[1:], s.dtype) for s in srcs]


def _chip_sems(n):
    return [pltpu.SemaphoreType.DMA((n, 3)), pltpu.SemaphoreType.DMA((n, 3)), pltpu.SemaphoreType.DMA((n,))]


class SideExchange:
    def __init__(self, srcs, gather):
        self.srcs, self.gather, self.n = list(srcs), gather, len(srcs)

    def specs(self):
        return [_ANY] * self.n, _chip_out_shapes(self.srcs), [_ANY] * self.n, _chip_sems(self.n)

    def run(self, src_refs, out_refs, sem_refs, step, last_step, compute):
        @pl.when(step == 0)
        def _():
            for cp in _chip_copies(src_refs, out_refs, sem_refs, self.gather):
                cp.start()

        compute()

        @pl.when(step == last_step)
        def _():
            for cp in _chip_copies(src_refs, out_refs, sem_refs, self.gather):
                cp.wait()


def sibling_exchange(srcs, *, pick, name):
    n = len(srcs)

    def body(*refs):
        src, out = refs[:n], refs[n:2 * n]
        send_sems, recv_sems = refs[2 * n:]
        x, y, c = _place()
        copies = []
        for k in range(n):
            s_ref = src[k].at[pl.ds(0, src[k].shape[0]), 1 - c] if pick else src[k]
            cp = pltpu.make_async_remote_copy(
                src_ref=s_ref, dst_ref=out[k], send_sem=send_sems.at[k], recv_sem=recv_sems.at[k],
                device_id=(x, y, 1 - c), device_id_type=_MESH)
            cp.start()
            copies.append(cp)
        for cp in copies:
            cp.wait()

    outs = [jax.ShapeDtypeStruct(s.shape[:1] + s.shape[2:] if pick else s.shape, s.dtype) for s in srcs]
    return pl.pallas_call(
        body, name=name, out_shape=outs, in_specs=[_ANY] * n, out_specs=[_ANY] * n,
        scratch_shapes=[pltpu.SemaphoreType.DMA((n,)), pltpu.SemaphoreType.DMA((n,))],
    )(*srcs)


def core_halves(core, mine, other):
    return jnp.where(core == 0, jnp.stack([mine, other], axis=1), jnp.stack([other, mine], axis=1))


def matmul(a, b, *, m, n, k, tm, tn, tk, a_spec, b_spec, dims, out_dtype, name,
           out_spec=None, out_shape=None, resid=None, gate=None, aux_dtype=None, side=None):
    nk = k // tk
    o_spec = out_spec or pl.BlockSpec((tm, tn), lambda i, j, kk: (i, j))
    o_shape = out_shape or (m, n)
    n_in = 2 + (resid is not None) + (gate is not None)
    n_out = 1 + (aux_dtype is not None)
    n_side = 0 if side is None else side.n
    side_in, side_shapes, side_out, side_sems = ([], [], [], []) if side is None else side.specs()

    def body(*refs):
        if side is None:
            return compute(*refs)
        outs0 = n_in + n_side
        scratch = refs[outs0 + n_out + n_side:]
        main = refs[:n_in] + refs[outs0:outs0 + n_out] + scratch[:len(scratch) - len(side_sems)]
        step = (pl.program_id(0) * (n // tn) + pl.program_id(1)) * nk + pl.program_id(2)
        side.run(refs[n_in:outs0], refs[outs0 + n_out:outs0 + n_out + n_side], scratch[len(scratch) - len(side_sems):],
                 step, (m // tm) * (n // tn) * nk - 1, lambda: compute(*main))

    def compute(*refs):
        a_ref, b_ref = refs[0], refs[1]
        pos = 2
        r_ref = g_ref = aux_ref = None
        if resid is not None:
            r_ref, pos = refs[pos], pos + 1
        if gate is not None:
            g_ref, pos = refs[pos], pos + 1
        o_ref, pos = refs[pos], pos + 1
        if aux_dtype is not None:
            aux_ref, pos = refs[pos], pos + 1
        prod = _dot(a_ref[...], b_ref[...], dims)

        def finish(y):
            if aux_ref is not None:
                aux_ref[...] = y.astype(aux_dtype)
            if g_ref is not None:
                y = y * g_ref[...]
            if r_ref is not None:
                y = y + r_ref[...]
            o_ref[...] = y.astype(out_dtype)

        if nk == 1:
            finish(prod)
            return
        acc = refs[pos]
        kk = pl.program_id(2)

        @pl.when(kk == 0)
        def _():
            acc[...] = prod

        @pl.when((kk > 0) & (kk < nk - 1))
        def _():
            acc[...] += prod

        @pl.when(kk == nk - 1)
        def _():
            finish(acc[...] + prod)

    ins, in_specs = [a, b], [a_spec, b_spec]
    if resid is not None:
        ins.append(resid)
        in_specs.append(pl.BlockSpec((tm, tn), lambda i, j, kk: (i, j)))
    if gate is not None:
        ins.append(gate)
        in_specs.append(pl.BlockSpec((1, tn), lambda i, j, kk: (0, j)))
    outs, out_specs = [jax.ShapeDtypeStruct(o_shape, out_dtype)], [o_spec]
    if aux_dtype is not None:
        outs.append(jax.ShapeDtypeStruct((m, n), aux_dtype))
        out_specs.append(pl.BlockSpec((tm, tn), lambda i, j, kk: (i, j)))
    sem = ("parallel", "parallel", "arbitrary") if side is None else ("arbitrary",) * 3
    res = pl.pallas_call(
        body, name=name, grid=(m // tm, n // tn, nk), in_specs=in_specs + side_in, out_specs=out_specs + side_out,
        out_shape=outs + side_shapes, scratch_shapes=([pltpu.VMEM((tm, tn), F32)] if nk > 1 else []) + side_sems,
        compiler_params=_params(*sem, vmem=MATMUL_VMEM),
    )(*ins, *([] if side is None else side.srcs))
    main = res[:n_out] if aux_dtype is not None else res[0]
    return main if side is None else (main, res[n_out:])


def mm_nn(a, b, *, out_dtype, name, tm_cap=1088, tn_cap=1024, tk_cap=MATMUL_TK, **kw):
    m, k = a.shape
    n = b.shape[1]
    tm, tn, tk = _tile(m, tm_cap, 16), _tile(n, tn_cap, LANES), _tile(k, tk_cap, LANES)
    return matmul(a, b, m=m, n=n, k=k, tm=tm, tn=tn, tk=tk, dims=_NN, out_dtype=out_dtype, name=name,
                  a_spec=pl.BlockSpec((tm, tk), lambda i, j, kk: (i, kk)),
                  b_spec=pl.BlockSpec((tk, tn), lambda i, j, kk: (kk, j)), **kw)


def mm_nt(a, b, *, out_dtype, name, tm_cap=1088, tn_cap=1024, tk_cap=MATMUL_TK // 2, **kw):
    m, k = a.shape
    n = b.shape[0]
    tm, tn, tk = _tile(m, tm_cap, 16), _tile(n, tn_cap, LANES), _tile(k, tk_cap, LANES)
    return matmul(a, b, m=m, n=n, k=k, tm=tm, tn=tn, tk=tk, dims=_NT, out_dtype=out_dtype, name=name,
                  a_spec=pl.BlockSpec((tm, tk), lambda i, j, kk: (i, kk)),
                  b_spec=pl.BlockSpec((tn, tk), lambda i, j, kk: (j, kk)), **kw)


def mm_tn(a, b, *, out_dtype, name, tm_cap=1024, tn_cap=1024, tk_cap=MATMUL_TK_TOKENS, **kw):
    k, m = a.shape
    n = b.shape[1]
    tm, tn, tk = _tile(m, tm_cap, LANES), _tile(n, tn_cap, LANES), _tile(k, tk_cap, 16)
    return matmul(a, b, m=m, n=n, k=k, tm=tm, tn=tn, tk=tk, dims=_TN, out_dtype=out_dtype, name=name,
                  a_spec=pl.BlockSpec((tk, tm), lambda i, j, kk: (kk, i)),
                  b_spec=pl.BlockSpec((tk, tn), lambda i, j, kk: (kk, j)), **kw)


def _rms(x):
    r = lax.rsqrt(jnp.mean(x * x, axis=-1, keepdims=True) + EPS)
    return x * r, r


def norm_mod_fwd(x, ctx, nw, scale2, shift2, *, name):
    s, d = x.shape
    cl = 0 if ctx is None else ctx.shape[0]
    tr = _tile(s if ctx is None else cl, 256, 16)
    n_lat = s // tr

    def body(*refs):
        if ctx is None:
            x_ref, nw_ref, sc_ref, sh_ref, o_ref = refs
            v = x_ref[...]
        else:
            x_ref, c_ref, nw_ref, sc_ref, sh_ref, o_ref = refs
            v = jnp.where(pl.program_id(0) < n_lat, x_ref[...], c_ref[...])
        y = _rms(v)[0] * nw_ref[...]
        o_ref[...] = (y * (1.0 + sc_ref[...]) + sh_ref[...]).astype(o_ref.dtype)

    sel = pl.BlockSpec((None, 1, d), lambda i: (i // n_lat, 0, 0))
    ins = [x] if ctx is None else [x, ctx]
    specs = [pl.BlockSpec((tr, d), lambda i: (jnp.minimum(i, n_lat - 1), 0))]
    if ctx is not None:
        specs.append(pl.BlockSpec((tr, d), lambda i: (jnp.maximum(i - n_lat, 0), 0)))
    return pl.pallas_call(
        body, name=name, grid=((s + cl) // tr,),
        in_specs=specs + [pl.BlockSpec((1, d), lambda i: (0, 0)), sel, sel],
        out_specs=pl.BlockSpec((tr, d), lambda i: (i, 0)),
        out_shape=jax.ShapeDtypeStruct((s + cl, d), ACT_DTYPE),
        compiler_params=_params("parallel"),
    )(*ins, nw, scale2, shift2)


def norm_mod_bwd(x, nw, scale, d_hn, *, row0, resid, init, name):
    r, d = x.shape
    tr = _tile(r, 256, 16)
    off = row0 // tr
    want_dx = resid is not None

    def body(*refs):
        x_ref, nw_ref, sc_ref, dh_ref = refs[:4]
        pos = 4
        res_ref = init_ref = dx_ref = None
        if want_dx:
            res_ref, pos = refs[pos], pos + 1
        if init is not None:
            init_ref, pos = refs[pos], pos + 1
        if want_dx:
            dx_ref, pos = refs[pos], pos + 1
        dnw_ref, dsc_ref, dsh_ref = refs[pos:pos + 3]
        i = pl.program_id(0)

        @pl.when(i == 0)
        def _():
            dnw_ref[...] = jnp.zeros_like(dnw_ref) if init_ref is None else init_ref[...]
            dsc_ref[...] = jnp.zeros_like(dsc_ref)
            dsh_ref[...] = jnp.zeros_like(dsh_ref)

        nrm, rs = _rms(x_ref[...])
        w = nw_ref[...]
        dh = dh_ref[...].astype(F32)
        dsh_ref[...] += jnp.sum(dh, axis=0, keepdims=True)
        dsc_ref[...] += jnp.sum(dh * (nrm * w), axis=0, keepdims=True)
        dy = dh * (1.0 + sc_ref[...])
        dnw_ref[...] += jnp.sum(dy * nrm, axis=0, keepdims=True)
        if want_dx:
            dn = dy * w
            dx = rs * (dn - nrm * jnp.mean(dn * nrm, axis=-1, keepdims=True))
            dx_ref[...] = dx + res_ref[...]

    row = pl.BlockSpec((tr, d), lambda i: (i, 0))
    vec = pl.BlockSpec((1, d), lambda i: (0, 0))
    ins, specs = [x, nw, scale, d_hn], [row, vec, vec, pl.BlockSpec((tr, d), lambda i: (i + off, 0))]
    if want_dx:
        ins.append(resid)
        specs.append(row)
    if init is not None:
        ins.append(init)
        specs.append(vec)
    vshape = jax.ShapeDtypeStruct((1, d), F32)
    outs, ospecs = [vshape] * 3, [vec] * 3
    if want_dx:
        outs, ospecs = [jax.ShapeDtypeStruct((r, d), F32)] + outs, [row] + ospecs
    res = pl.pallas_call(body, name=name, grid=(r // tr,), in_specs=specs, out_specs=ospecs, out_shape=outs,
                         compiler_params=_params("arbitrary"))(*ins)
    return tuple(res) if want_dx else (None,) + tuple(res)


def gate_bwd(dx, yo, gate, *, name):
    s, d = dx.shape
    tr = _tile(s, 256, 16)

    def body(dx_ref, yo_ref, g_ref, dyo_ref, dg_ref):
        @pl.when(pl.program_id(0) == 0)
        def _():
            dg_ref[...] = jnp.zeros_like(dg_ref)

        g = dx_ref[...]
        dg_ref[...] += jnp.sum(g * yo_ref[...].astype(F32), axis=0, keepdims=True)
        dyo_ref[...] = (g * g_ref[...]).astype(dyo_ref.dtype)

    row = pl.BlockSpec((tr, d), lambda i: (i, 0))
    vec = pl.BlockSpec((1, d), lambda i: (0, 0))
    return pl.pallas_call(
        body, name=name, grid=(s // tr,), in_specs=[row, row, vec], out_specs=[row, vec],
        out_shape=[jax.ShapeDtypeStruct((s, d), ACT_DTYPE), jax.ShapeDtypeStruct((1, d), F32)],
        compiler_params=_params("arbitrary"))(dx, yo, gate)


def loss_head(x, fw, target, *, name):
    s, d = x.shape
    tr = _tile(s, 256, 16)

    def body(x_ref, w_ref, t_ref, loss_ref, dx_ref, dw_ref):
        @pl.when(pl.program_id(0) == 0)
        def _():
            loss_ref[...] = jnp.zeros_like(loss_ref)
            dw_ref[...] = jnp.zeros_like(dw_ref)

        nrm, rs = _rms(x_ref[...])
        w = w_ref[...]
        err = nrm * w - t_ref[...]
        loss_ref[...] += 0.5 * jnp.sum(jnp.mean(err * err, axis=-1, keepdims=True))
        d_out = err * (1.0 / d)
        dw_ref[...] += jnp.sum(d_out * nrm, axis=0, keepdims=True)
        dn = d_out * w
        dx_ref[...] = rs * (dn - nrm * jnp.mean(dn * nrm, axis=-1, keepdims=True))

    row = pl.BlockSpec((tr, d), lambda i: (i, 0))
    vec = pl.BlockSpec((1, d), lambda i: (0, 0))
    return pl.pallas_call(
        body, name=name, grid=(s // tr,), in_specs=[row, vec, row],
        out_specs=[pl.BlockSpec((SUBLANES, LANES), lambda i: (0, 0)), row, vec],
        out_shape=[jax.ShapeDtypeStruct((SUBLANES, LANES), F32), jax.ShapeDtypeStruct((s, d), F32),
                   jax.ShapeDtypeStruct((1, d), F32)],
        compiler_params=_params("arbitrary"))(x, fw, target)


def _segments(rows, seg_a, seg_b):
    t = _iota((rows, 1), 0)
    if seg_b == 0:
        return t % seg_a, seg_a
    return jnp.where(t < seg_a, t, t - seg_a), jnp.where(t < seg_a, seg_a, seg_b)


def _shift(x, o, seg):
    if o == 0:
        return x
    pos, length = _segments(x.shape[0], *seg)
    y = pltpu.roll(x, (-o) % x.shape[0], 0)
    return jnp.where((pos + o >= 0) & (pos + o < length), y, 0.0)


def _conv(x, w, offsets, seg):
    acc = None
    for j, o in enumerate(offsets):
        term = w[j:j + 1, :] * _shift(x, o, seg)
        acc = term if acc is None else acc + term
    return acc


def _conv_bwd(x, w, dy, offsets, seg):
    dx = None
    dw = jnp.zeros(w.shape, F32)
    row = _iota(w.shape, 0)
    for j, o in enumerate(offsets):
        term = w[j:j + 1, :] * _shift(dy, -o, seg)
        dx = term if dx is None else dx + term
        dwj = jnp.sum(dy * _shift(x, o, seg), axis=0, keepdims=True)
        dw = dw + jnp.where(row == j, dwj, 0.0)
    return dx, dw


def _qkv_post(y, group, scale):
    a = _silu(y)
    n = a * lax.rsqrt(jnp.sum(a * a, axis=-1, keepdims=True) + EPS)
    return jnp.where(group == 0, n * scale, jnp.where(group == 1, n, a))


def qkv_conv_fwd(proj, conv_w, *, s, cl, heads, name):
    m = s + cl
    dh = LANES
    scale = dh ** -0.5

    def body(x_ref, w_ref, o_ref):
        group = pl.program_id(0) // heads
        y = _conv(x_ref[...], w_ref[...], DN_CONV_OFFSETS, (s, cl))
        o_ref[...] = _qkv_post(y, group, scale).astype(o_ref.dtype)

    return pl.pallas_call(
        body, name=name, grid=(3 * heads,),
        in_specs=[pl.BlockSpec((m, dh), lambda j: (0, j)), pl.BlockSpec((len(DN_CONV_OFFSETS), dh), lambda j: (0, j))],
        out_specs=pl.BlockSpec((m, dh), lambda j: (0, j)),
        out_shape=jax.ShapeDtypeStruct((m, 3 * heads * dh), ACT_DTYPE),
        compiler_params=_params("parallel"))(proj, conv_w)


def qkv_conv_bwd(proj, conv_w, dqkv, *, s, cl, heads, name):
    m = s + cl
    dh = LANES
    scale = dh ** -0.5
    kk = len(DN_CONV_OFFSETS)

    def body(x_ref, w_ref, d_ref, dx_ref, dw_ref):
        group = pl.program_id(0) // heads
        x, w = x_ref[...], w_ref[...]
        y = _conv(x, w, DN_CONV_OFFSETS, (s, cl))
        a = _silu(y)
        rn = lax.rsqrt(jnp.sum(a * a, axis=-1, keepdims=True) + EPS)
        n = a * rn
        dout = d_ref[...] * jnp.where(group == 0, scale, 1.0)
        da_norm = rn * (dout - n * jnp.sum(dout * n, axis=-1, keepdims=True))
        dy = jnp.where(group == 2, dout, da_norm) * _dsilu(y)
        dx, dw = _conv_bwd(x, w, dy, DN_CONV_OFFSETS, (s, cl))
        dx_ref[...] = dx.astype(dx_ref.dtype)
        dw_ref[...] = dw

    col = pl.BlockSpec((m, dh), lambda j: (0, j))
    wspec = pl.BlockSpec((kk, dh), lambda j: (0, j))
    return pl.pallas_call(
        body, name=name, grid=(3 * heads,),
        in_specs=[col, wspec, pl.BlockSpec((None, m, dh), lambda j: (j // heads, 0, j % heads))],
        out_specs=[col, wspec],
        out_shape=[jax.ShapeDtypeStruct((m, 3 * heads * dh), ACT_DTYPE),
                   jax.ShapeDtypeStruct((kk, 3 * heads * dh), F32)],
        compiler_params=_params("parallel"))(proj, conv_w, dqkv)


def _scan_masks(d):
    t, s = _iota((CHUNK, CHUNK), 0), _iota((CHUNK, CHUNK), 1)
    return ((s <= t), (s < t)) if d == 0 else ((s >= t), (s > t))


def _bmm(spec, a, b, precision=None):
    if precision is None:
        a, b = a.astype(MXU_DTYPE), b.astype(MXU_DTYPE)
    return jnp.einsum(spec, a, b, precision=precision, preferred_element_type=F32)


def _unit_tri_inverse(a):
    mm = functools.partial(_bmm, 'nts,nsr->ntr', precision=INV_PRECISION)
    row, col = _iota((CHUNK, CHUNK), 0), _iota((CHUNK, CHUNK), 1)
    eye = (row == col).astype(F32)
    dg = jnp.where(row // INV_BLOCK == col // INV_BLOCK, a, 0.0)
    off = a - dg
    p = eye - dg
    pw = dg
    for _ in range(3):
        pw = mm(pw, pw)
        p = p + mm(p, pw)
    n = mm(p, off)
    r = eye - n
    return mm(r + mm(r, mm(n, n)), p)


def _dn_intra(q, k, v, beta_b, gc_b, d):
    dh = q.shape[-1]
    incl, strict = _scan_masks(d)
    gc64 = gc_b[:, :, :CHUNK]
    diff = gc64 - jnp.swapaxes(gc64, 1, 2)
    decay = jnp.where(incl, jnp.exp(jnp.where(incl, diff, 0.0)), 0.0)
    qk_kk = _bmm('ntd,nsd->nts', jnp.concatenate([q, k], axis=1), k)
    qk, kk = qk_kk[:, :CHUNK], qk_kk[:, CHUNK:]
    a = jnp.where(strict, beta_b[:, :, :CHUNK] * kk * decay, 0.0)
    tinv = _unit_tri_inverse(a)
    rhs = jnp.concatenate([beta_b * jnp.exp(gc_b) * k, beta_b * v], axis=-1)
    wu = _bmm('nts,nsd->ntd', tinv, rhs, INV_PRECISION)
    w, u = wu[:, :, :dh], wu[:, :, dh:]
    last = CHUNK - 1 if d == 0 else 0
    gl = gc_b[:, last:last + 1, :]
    ke = k * jnp.exp(gl - gc_b)
    ge = jnp.exp(gl)
    return w, u, ke, ge, qk * decay, q * jnp.exp(gc_b)


def _dn_step(s, w, u, ke, ge, aqk, qg):
    ws_qs = _dot(jnp.concatenate([w, qg], axis=0), s, _NN)
    u2 = u - ws_qs[:CHUNK]
    s_new = ge * s + _dot(ke, u2, _TN)
    o = ws_qs[CHUNK:] + _dot(aqk, u2, _NN)
    return s_new, o


def _chunk_cumsum(x, d):
    rows = x.shape[0]
    pos = _iota((rows, 1), 0) % CHUNK
    step = 1
    while step < CHUNK:
        if d == 0:
            x = x + jnp.where(pos >= step, pltpu.roll(x, step, 0), 0.0)
        else:
            x = x + jnp.where(pos < CHUNK - step, pltpu.roll(x, rows - step, 0), 0.0)
        step *= 2
    return x


def _pick_lane(x, j):
    return jnp.sum(jnp.where(_iota(x.shape, 1) == j, x, 0.0), axis=1, keepdims=True)


def _dn_gates(ba, a_log, dt_bias, d, h, heads):
    braw = _pick_lane(ba, d * heads + h)
    araw = _pick_lane(ba, (2 + d) * heads + h)
    a_neg = -jnp.exp(_pick_lane(a_log[d:d + 1, :], h))
    pre = araw + _pick_lane(dt_bias[d:d + 1, :], h)
    return jax.nn.sigmoid(braw), a_neg * jax.nn.softplus(pre), pre, a_neg


_DN_SUB_FWD = 16
_DN_SUB_BWD = 8


def _for_sub_batches(s, cl, fn, sub=_DN_SUB_FWD):
    for base, total in ((0, s), (s, cl)):
        nch = min(sub, total // CHUNK)
        rows_per = nch * CHUNK
        count = total // rows_per

        def run(i, carry, base=base, nch=nch, rows_per=rows_per):
            row0 = pl.multiple_of(base + i * rows_per, rows_per)
            ge0 = pl.multiple_of((base // CHUNK + i * nch) * SUBLANES, nch * SUBLANES)
            fn(pl.ds(row0, rows_per), pl.ds(ge0, nch * SUBLANES), nch)
            return carry

        if count == 1:
            fn(pl.ds(base, rows_per), pl.ds(base // CHUNK * SUBLANES, nch * SUBLANES), nch)
        else:
            lax.fori_loop(0, count, run, 0)


def _dn_chunk_order(t, d, n_lat, n_ctx):
    if d == 0:
        return jnp.where(t < n_ctx, n_lat + t, t - n_ctx)
    return n_lat + n_ctx - 1 - t


def _dn_fill_intra(q_ref, k_ref, v_ref, bb_s, gc_s, w_s, u_s, ke_s, ge_s, aqk_s, qg_s, d, s, cl):
    dh = LANES

    def fill(rows, ge_rows, nch):
        def load(ref):
            return ref[rows, :].astype(F32).reshape(nch, CHUNK, dh)

        w, u, ke, ge, aqk, qg = _dn_intra(load(q_ref), load(k_ref), load(v_ref), load(bb_s), load(gc_s), d)
        w_s[rows, :] = w.reshape(nch * CHUNK, dh)
        u_s[rows, :] = u.reshape(nch * CHUNK, dh)
        ke_s[rows, :] = ke.reshape(nch * CHUNK, dh)
        qg_s[rows, :] = qg.reshape(nch * CHUNK, dh)
        aqk_s[rows, :] = aqk.reshape(nch * CHUNK, CHUNK)
        ge_s[ge_rows, :] = jnp.broadcast_to(ge, (nch, SUBLANES, dh)).reshape(nch * SUBLANES, dh)

    _for_sub_batches(s, cl, fill)


def _dn_chunk_refs(cid, w_s, u_s, ke_s, ge_s, aqk_s, qg_s):
    rows = pl.ds(pl.multiple_of(cid * CHUNK, CHUNK), CHUNK)
    ge = ge_s[pl.ds(pl.multiple_of(cid * SUBLANES, SUBLANES), SUBLANES), :][0:1]
    return rows, (w_s[rows, :], u_s[rows, :], ke_s[rows, :], ge, aqk_s[rows, :], qg_s[rows, :])


def _dn_scratch(m):
    dh = LANES
    big = pltpu.VMEM((m, dh), F32)
    return [big, big, big, big, big, pltpu.VMEM((m // CHUNK * SUBLANES, dh), F32), pltpu.VMEM((m, CHUNK), F32), big]


def _once(shape, index_map):
    return pl.BlockSpec(shape, index_map, pipeline_mode=pl.Buffered(1))


def dn_fwd(qkv, ba, a_log, dt_bias, side, *, s, cl, heads, name):
    m = s + cl
    dh = LANES
    n_lat, n_ctx = s // CHUNK, cl // CHUNK
    side_in, side_shapes, side_out, side_sems = side.specs()

    def body(*refs):
        q_ref, k_ref, v_ref, ba_ref, al_ref, dt_ref = refs[:6]
        pos = 6 + side.n
        o_ref, st_ref = refs[pos], refs[pos + 1]
        scratch = refs[pos + 2 + side.n:]
        side.run(refs[6:pos], refs[pos + 2:pos + 2 + side.n], scratch[8:], pl.program_id(0), heads - 1,
                 lambda: compute(q_ref, k_ref, v_ref, ba_ref, al_ref, dt_ref, o_ref, st_ref, *scratch[:8]))

    def compute(q_ref, k_ref, v_ref, ba_ref, al_ref, dt_ref, o_ref, st_ref,
                bb_s, gc_s, w_s, u_s, ke_s, ge_s, aqk_s, qg_s):
        h = pl.program_id(0)
        for d in (0, 1):
            beta, g, _, _ = _dn_gates(ba_ref[...], al_ref[...], dt_ref[...], d, h, heads)
            bb_s[...] = jnp.broadcast_to(beta, (m, dh))
            gc_s[...] = _chunk_cumsum(jnp.broadcast_to(g, (m, dh)), d)
            _dn_fill_intra(q_ref, k_ref, v_ref, bb_s, gc_s, w_s, u_s, ke_s, ge_s, aqk_s, qg_s, d, s, cl)

            def step(t, state):
                cid = _dn_chunk_order(t, d, n_lat, n_ctx)
                rows, terms = _dn_chunk_refs(cid, w_s, u_s, ke_s, ge_s, aqk_s, qg_s)
                st_ref[d, pl.ds(pl.multiple_of(cid * dh, dh), dh), :] = state.astype(st_ref.dtype)
                state, o = _dn_step(state, *terms)

                @pl.when(cid < n_lat)
                def _():
                    if d == 0:
                        o_ref[rows, :] = o
                    else:
                        o_ref[rows, :] += o

                return state

            lax.fori_loop(0, n_lat + n_ctx, step, jnp.zeros((dh, dh), F32))

    def col(j0):
        return _once((m, dh), lambda h: (0, j0 + h))

    small = pl.BlockSpec((SUBLANES, LANES), lambda h: (0, 0))
    st_rows = (n_lat + n_ctx) * dh
    res = pl.pallas_call(
        body, name=name, grid=(heads,),
        in_specs=[col(0), col(heads), col(2 * heads), _once((m, LANES), lambda h: (0, 0)), small, small] + side_in,
        out_specs=[pl.BlockSpec((s, dh), lambda h: (0, h)), _once((None, 2, st_rows, dh), lambda h: (h, 0, 0, 0))]
        + side_out,
        out_shape=[jax.ShapeDtypeStruct((s, heads * dh), F32),
                   jax.ShapeDtypeStruct((heads, 2, st_rows, dh), ACT_DTYPE)] + side_shapes,
        scratch_shapes=_dn_scratch(m) + side_sems,
        compiler_params=_params("arbitrary", vmem=BIG_KERNEL_VMEM))(qkv, qkv, qkv, ba, a_log, dt_bias, *side.srcs)
    return res[0], res[1], res[2:]


def dn_out_fwd(o, proj, dn_norm, *, heads, gate_col0, name):
    s = o.shape[0]
    dh = LANES
    tr = _tile(s, 1024, 16)

    def body(o_ref, g_ref, nw_ref, y_ref):
        y_ref[...] = (_rms(o_ref[...])[0] * nw_ref[...] * _silu(g_ref[...])).astype(y_ref.dtype)

    blk = pl.BlockSpec((tr, dh), lambda i, h: (i, h))
    return pl.pallas_call(
        body, name=name, grid=(s // tr, heads),
        in_specs=[blk, pl.BlockSpec((tr, dh), lambda i, h: (i, gate_col0 + h)), pl.BlockSpec((1, dh), lambda i, h: (0, 0))],
        out_specs=blk, out_shape=jax.ShapeDtypeStruct((s, heads * dh), ACT_DTYPE),
        compiler_params=_params("parallel", "parallel"))(o, proj, dn_norm)


def dn_out_bwd(o, proj, d_y, dn_norm, *, heads, gate_col0, name):
    s = o.shape[0]
    dh = LANES
    tr = _tile(s, 1024, 16)

    def body(o_ref, g_ref, dy_ref, nw_ref, do_ref, dg_ref, dnw_ref):
        @pl.when((pl.program_id(0) == 0) & (pl.program_id(1) == 0))
        def _():
            dnw_ref[...] = jnp.zeros_like(dnw_ref)

        nrm, rs = _rms(o_ref[...])
        nw, gate, dy = nw_ref[...], g_ref[...], dy_ref[...]
        dg_ref[...] = (dy * (nrm * nw) * _dsilu(gate)).astype(dg_ref.dtype)
        dy0 = dy * _silu(gate)
        dnw_ref[0:1, :] += jnp.sum(dy0 * nrm, axis=0, keepdims=True)
        dn = dy0 * nw
        do_ref[...] = rs * (dn - nrm * jnp.mean(dn * nrm, axis=-1, keepdims=True))

    blk = pl.BlockSpec((tr, dh), lambda i, h: (i, h))
    return pl.pallas_call(
        body, name=name, grid=(s // tr, heads),
        in_specs=[blk, pl.BlockSpec((tr, dh), lambda i, h: (i, gate_col0 + h)), blk,
                  pl.BlockSpec((1, dh), lambda i, h: (0, 0))],
        out_specs=[blk, blk, pl.BlockSpec((SUBLANES, LANES), lambda i, h: (0, 0))],
        out_shape=[jax.ShapeDtypeStruct((s, heads * dh), F32), jax.ShapeDtypeStruct((s, heads * dh), ACT_DTYPE),
                   jax.ShapeDtypeStruct((SUBLANES, LANES), F32)],
        compiler_params=_params("arbitrary", "arbitrary"))(o, proj, d_y, dn_norm)


def dn_bwd(qkv, ba, d_o, states, a_log, dt_bias, side, *, s, cl, heads, name):
    m = s + cl
    dh = LANES
    n_lat, n_ctx = s // CHUNK, cl // CHUNK
    n_all = n_lat + n_ctx
    side_in, side_shapes, side_out, side_sems = side.specs()

    def body(*refs):
        ins, pos = refs[:8], 8 + side.n
        outs = refs[pos:pos + 4]
        scratch = refs[pos + 4 + side.n:]
        side.run(refs[8:pos], refs[pos + 4:pos + 4 + side.n], scratch[8:], pl.program_id(0), heads - 1,
                 lambda: compute(*ins, *outs, *scratch[:8]))

    def compute(q_ref, k_ref, v_ref, ba_ref, do_ref, st_ref, al_ref, dt_ref, dqkv_ref, dba_ref, dal_ref, ddt_ref,
                bb_s, gc_s, w_s, u_s, ke_s, ge_s, aqk_s, qg_s):
        h = pl.program_id(0)
        dq_ref, dk_ref, dv_ref = dqkv_ref.at[0], dqkv_ref.at[1], dqkv_ref.at[2]

        @pl.when(h == 0)
        def _():
            dba_ref[...] = jnp.zeros_like(dba_ref)

        lane = _iota((m, LANES), 1)
        for d in (0, 1):
            beta, g, pre, a_neg = _dn_gates(ba_ref[...], al_ref[...], dt_ref[...], d, h, heads)
            bb_s[...] = jnp.broadcast_to(beta, (m, dh))
            gc_s[...] = _chunk_cumsum(jnp.broadcast_to(g, (m, dh)), d)
            _dn_fill_intra(q_ref, k_ref, v_ref, bb_s, gc_s, w_s, u_s, ke_s, ge_s, aqk_s, qg_s, d, s, cl)

            def bwd_step(i, dstate):
                cid = _dn_chunk_order(n_all - 1 - i, d, n_lat, n_ctx)
                rows, terms = _dn_chunk_refs(cid, w_s, u_s, ke_s, ge_s, aqk_s, qg_s)
                state = st_ref[d, pl.ds(pl.multiple_of(cid * dh, dh), dh), :].astype(F32)
                _, vjp = jax.vjp(_dn_step, state, *terms)
                lat_rows = pl.ds(pl.multiple_of(jnp.minimum(cid, n_lat - 1) * CHUNK, CHUNK), CHUNK)
                do = jnp.where(cid < n_lat, do_ref[lat_rows, :], 0.0)
                dstate, dw, du, dke, dge, daqk, dqg = vjp((dstate, do))
                w_s[rows, :] = dw
                u_s[rows, :] = du
                ke_s[rows, :] = dke
                qg_s[rows, :] = dqg
                aqk_s[rows, :] = daqk
                ge_s[pl.ds(pl.multiple_of(cid * SUBLANES, SUBLANES), SUBLANES), :] = jnp.broadcast_to(
                    dge, (SUBLANES, dh))
                return dstate

            lax.fori_loop(0, n_all, bwd_step, jnp.zeros((dh, dh), F32))

            def intra_bwd(rows, ge_rows, nch):
                def load(ref, width=dh):
                    return ref[rows, :].astype(F32).reshape(nch, CHUNK, width)

                _, vjp = jax.vjp(functools.partial(_dn_intra, d=d), load(q_ref), load(k_ref), load(v_ref),
                                 load(bb_s), load(gc_s))
                dge = ge_s[ge_rows, :].reshape(nch, SUBLANES, dh)[:, 0:1]
                dq, dk, dv, dbb, dgc = vjp((load(w_s), load(u_s), load(ke_s), dge, load(aqk_s, CHUNK), load(qg_s)))
                flat = lambda x: x.reshape(nch * CHUNK, dh)
                if d == 0:
                    dq_ref[rows, :], dk_ref[rows, :], dv_ref[rows, :] = flat(dq), flat(dk), flat(dv)
                else:
                    dq_ref[rows, :] += flat(dq)
                    dk_ref[rows, :] += flat(dk)
                    dv_ref[rows, :] += flat(dv)
                bb_s[rows, :] = flat(dbb)
                gc_s[rows, :] = flat(dgc)

            _for_sub_batches(s, cl, intra_bwd, _DN_SUB_BWD)

            dbeta = jnp.sum(bb_s[...], axis=1, keepdims=True)
            dg = jnp.sum(_chunk_cumsum(gc_s[...], 1 - d), axis=1, keepdims=True)
            dbraw = dbeta * beta * (1.0 - beta)
            dpre = dg * a_neg * jax.nn.sigmoid(pre)
            dba_ref[...] += (jnp.where(lane == d * heads + h, dbraw, 0.0)
                             + jnp.where(lane == (2 + d) * heads + h, dpre, 0.0))
            dal_ref[d:d + 1, :] = jnp.broadcast_to(jnp.sum(dg * g, axis=0, keepdims=True), (1, LANES))
            ddt_ref[d:d + 1, :] = jnp.broadcast_to(jnp.sum(dpre, axis=0, keepdims=True), (1, LANES))
        dal_ref[2:SUBLANES, :] = jnp.zeros((SUBLANES - 2, LANES), F32)
        ddt_ref[2:SUBLANES, :] = jnp.zeros((SUBLANES - 2, LANES), F32)

    def col(j0, rows=m):
        return _once((rows, dh), lambda h: (0, j0 + h))

    small = pl.BlockSpec((SUBLANES, LANES), lambda h: (0, 0))
    tile_h = pl.BlockSpec((None, SUBLANES, LANES), lambda h: (h, 0, 0))
    tiles = jax.ShapeDtypeStruct((heads, SUBLANES, LANES), F32)
    res = pl.pallas_call(
        body, name=name, grid=(heads,),
        in_specs=[col(0), col(heads), col(2 * heads), _once((m, LANES), lambda h: (0, 0)), col(0, s),
                  _once((None, 2, n_all * dh, dh), lambda h: (h, 0, 0, 0)), small, small] + side_in,
        out_specs=[_once((3, m, dh), lambda h: (0, 0, h)), _once((m, LANES), lambda h: (0, 0)), tile_h, tile_h]
        + side_out,
        out_shape=[jax.ShapeDtypeStruct((3, m, heads * dh), F32), jax.ShapeDtypeStruct((m, LANES), F32), tiles, tiles]
        + side_shapes,
        scratch_shapes=_dn_scratch(m) + side_sems,
        compiler_params=_params("arbitrary", vmem=BIG_KERNEL_VMEM))(qkv, qkv, qkv, ba, d_o, states, a_log, dt_bias,
                                                                     *side.srcs)
    return res[0], res[1], res[2], res[3], res[4:]


def _lin_scan(a, b, d):
    rows = a.shape[0]
    t = _iota((rows, 1), 0)
    step = 1
    while step < rows:
        if d == 0:
            ok, sa, sb = t >= step, pltpu.roll(a, step, 0), pltpu.roll(b, step, 0)
        else:
            ok, sa, sb = t < rows - step, pltpu.roll(a, rows - step, 0), pltpu.roll(b, rows - step, 0)
        b = b + a * jnp.where(ok, sb, 0.0)
        a = a * jnp.where(ok, sa, 1.0)
        step *= 2
    return b


def _lru_gates(xc, w_r, b_r, w_i, b_i, lam):
    r = jax.nn.sigmoid(_dot(xc, w_r, _NN) + b_r)
    i = jax.nn.sigmoid(_dot(xc, w_i, _NN) + b_i)
    log_a = -LRU_C * r * jax.nn.softplus(-lam)
    z = 2.0 * log_a
    series = -(z * (1.0 + z * (0.5 + z * (1.0 / 6.0))))
    one_minus = jnp.where(z > -0.01, series, 1.0 - jnp.exp(z))
    return jnp.exp(log_a), jnp.sqrt(one_minus) * (i * xc)


def _lru_states(a, b, d, s, cl):
    ac, bc = a[s:], b[s:]
    hc = _lin_scan(ac, bc, d)
    h0 = hc[cl - 1:cl] if d == 0 else hc[0:1]
    first = 0 if d == 0 else s - 1
    al = a[:s]
    bl = b[:s] + jnp.where(_iota((s, 1), 0) == first, al * h0, 0.0)
    return _lin_scan(al, bl, d), hc, h0


def _lru_specs(m, nb_dim):
    j_col = lambda rows: pl.BlockSpec((rows, LANES), lambda j: (0, j))
    w_blk = pl.BlockSpec((2, None, nb_dim, nb_dim), lambda j: (0, j, 0, 0))
    return j_col, w_blk


def lru_fwd(xcm, conv_w, conv_b, w_r, b_r, w_i, b_i, lam, *, s, cl, name):
    m, width = xcm.shape
    j_col, w_blk = _lru_specs(m, w_r.shape[-1])

    def body(x_ref, cw_ref, cb_ref, wr_ref, br_ref, wi_ref, bi_ref, lam_ref, h_ref):
        xc = _conv(x_ref[...], cw_ref[...], DN_CONV_OFFSETS, (s, cl)) + cb_ref[...]
        for d in (0, 1):
            a, b = _lru_gates(xc, wr_ref[d], br_ref[d:d + 1, :], wi_ref[d], bi_ref[d:d + 1, :], lam_ref[d:d + 1, :])
            h = _lru_states(a, b, d, s, cl)[0]
            if d == 0:
                h_ref[...] = h
            else:
                h_ref[...] += h

    return pl.pallas_call(
        body, name=name, grid=(width // LANES,),
        in_specs=[_once((m, LANES), lambda j: (0, j)), j_col(len(DN_CONV_OFFSETS)), j_col(1), w_blk, j_col(2), w_blk,
                  j_col(2), j_col(2)],
        out_specs=j_col(s), out_shape=jax.ShapeDtypeStruct((s, width), F32),
        compiler_params=_params("parallel", vmem=BIG_KERNEL_VMEM))(xcm, conv_w, conv_b, w_r, b_r, w_i, b_i, lam)


def lru_bwd(xcm, d_h, conv_w, conv_b, w_r, b_r, w_i, b_i, lam, *, s, cl, name):
    m, width = xcm.shape
    nb_dim = w_r.shape[-1]
    j_col, w_blk = _lru_specs(m, nb_dim)

    def body(x_ref, dh_ref, cw_ref, cb_ref, wr_ref, br_ref, wi_ref, bi_ref, lam_ref,
             dx_ref, dcw_ref, dcb_ref, dwr_ref, dbr_ref, dwi_ref, dbi_ref, dlam_ref):
        x, cw = x_ref[...], cw_ref[...]
        xc = _conv(x, cw, DN_CONV_OFFSETS, (s, cl)) + cb_ref[...]
        d_hl = dh_ref[...]
        d_xc = None
        for d in (0, 1):
            (a, b), vjp = jax.vjp(_lru_gates, xc, wr_ref[d], br_ref[d:d + 1, :], wi_ref[d], bi_ref[d:d + 1, :],
                                  lam_ref[d:d + 1, :])
            h, hc, h0 = _lru_states(a, b, d, s, cl)
            nxt = 1 if d == 0 else -1
            first = 0 if d == 0 else s - 1
            al, ac = a[:s], a[s:]
            lam_l = _lin_scan(_shift(al, nxt, (s, 0)), d_hl, 1 - d)
            h_prev = _shift(h, -nxt, (s, 0)) + jnp.where(_iota((s, 1), 0) == first, h0, 0.0)
            d_h0 = (al * lam_l)[first:first + 1]
            last_c = cl - 1 if d == 0 else 0
            d_hc = jnp.where(_iota((cl, 1), 0) == last_c, d_h0, 0.0)
            lam_c = _lin_scan(_shift(ac, nxt, (cl, 0)), d_hc, 1 - d)
            da = jnp.concatenate([lam_l * h_prev, lam_c * _shift(hc, -nxt, (cl, 0))], axis=0)
            db = jnp.concatenate([lam_l, lam_c], axis=0)
            g_xc, g_wr, g_br, g_wi, g_bi, g_lam = vjp((da, db))
            d_xc = g_xc if d_xc is None else d_xc + g_xc
            dwr_ref[d], dwi_ref[d] = g_wr, g_wi
            dbr_ref[d:d + 1, :], dbi_ref[d:d + 1, :], dlam_ref[d:d + 1, :] = g_br, g_bi, g_lam
        dx, dcw = _conv_bwd(x, cw, d_xc, DN_CONV_OFFSETS, (s, cl))
        dx_ref[...] = dx.astype(dx_ref.dtype)
        dcw_ref[...] = dcw
        dcb_ref[...] = jnp.sum(d_xc, axis=0, keepdims=True)

    kk = len(DN_CONV_OFFSETS)
    vec2 = jax.ShapeDtypeStruct((2, width), F32)
    return pl.pallas_call(
        body, name=name, grid=(width // LANES,),
        in_specs=[_once((m, LANES), lambda j: (0, j)), _once((s, LANES), lambda j: (0, j)), j_col(kk), j_col(1),
                  w_blk, j_col(2), w_blk, j_col(2), j_col(2)],
        out_specs=[j_col(m), j_col(kk), j_col(1), w_blk, j_col(2), w_blk, j_col(2), j_col(2)],
        out_shape=[jax.ShapeDtypeStruct((m, width), ACT_DTYPE), jax.ShapeDtypeStruct((kk, width), F32),
                   jax.ShapeDtypeStruct((1, width), F32), jax.ShapeDtypeStruct(w_r.shape, F32), vec2,
                   jax.ShapeDtypeStruct(w_i.shape, F32), vec2, vec2],
        compiler_params=_params("parallel", vmem=BIG_KERNEL_VMEM))(xcm, d_h, conv_w, conv_b, w_r, b_r, w_i, b_i, lam)


def lru_gate_fwd(h, proj, *, gate_col0, name):
    s, width = h.shape
    tr, tc = _tile(s, 512, 16), _tile(width, 512, LANES)
    c0 = gate_col0 * LANES // tc

    def body(h_ref, g_ref, y_ref):
        y_ref[...] = (h_ref[...] * _silu(g_ref[...])).astype(y_ref.dtype)

    blk = pl.BlockSpec((tr, tc), lambda i, j: (i, j))
    return pl.pallas_call(
        body, name=name, grid=(s // tr, width // tc),
        in_specs=[blk, pl.BlockSpec((tr, tc), lambda i, j: (i, c0 + j))], out_specs=blk,
        out_shape=jax.ShapeDtypeStruct((s, width), ACT_DTYPE),
        compiler_params=_params("parallel", "parallel"))(h, proj)


def lru_gate_bwd(h, proj, d_y, *, gate_col0, dy_col0, name):
    s, width = h.shape
    tr, tc = _tile(s, 512, 16), _tile(width, 512, LANES)
    c0, y0 = gate_col0 * LANES // tc, dy_col0 * LANES // tc

    def body(h_ref, g_ref, dy_ref, dh_ref, dg_ref):
        g, dy = g_ref[...], dy_ref[...]
        dh_ref[...] = dy * _silu(g)
        dg_ref[...] = (dy * h_ref[...] * _dsilu(g)).astype(dg_ref.dtype)

    blk = pl.BlockSpec((tr, tc), lambda i, j: (i, j))
    return pl.pallas_call(
        body, name=name, grid=(s // tr, width // tc),
        in_specs=[blk, pl.BlockSpec((tr, tc), lambda i, j: (i, c0 + j)), pl.BlockSpec((tr, tc), lambda i, j: (i, y0 + j))],
        out_specs=[blk, blk],
        out_shape=[jax.ShapeDtypeStruct((s, width), F32), jax.ShapeDtypeStruct((s, width), ACT_DTYPE)],
        compiler_params=_params("parallel", "parallel"))(h, proj, d_y)


def _sc_parts(p, width):
    return [p[:, k * width:(k + 1) * width] for k in range(4)]


def sc_mix_fwd(p, conv_w, *, name):
    s, width = p.shape[0], conv_w.shape[1]
    tr = 2 * GRID_W

    def body(p_ref, w_ref, y_ref):
        b_g, c_g, x_in, gate = _sc_parts(p_ref[...], width)
        z = _conv(c_g * x_in, w_ref[...], SC_CONV_OFFSETS, (GRID_W, 0))
        y_ref[...] = (b_g * z * _silu(gate)).astype(y_ref.dtype)

    return pl.pallas_call(
        body, name=name, grid=(s // tr,),
        in_specs=[pl.BlockSpec((tr, 4 * width), lambda i: (i, 0)), pl.BlockSpec(conv_w.shape, lambda i: (0, 0))],
        out_specs=pl.BlockSpec((tr, width), lambda i: (i, 0)),
        out_shape=jax.ShapeDtypeStruct((s, width), ACT_DTYPE),
        compiler_params=_params("parallel"))(p, conv_w)


def sc_mix_bwd(p, d_y, conv_w, *, name):
    s, width = p.shape[0], conv_w.shape[1]
    tr = 2 * GRID_W

    def body(p_ref, dy_ref, w_ref, dp_ref, dw_ref):
        @pl.when(pl.program_id(0) == 0)
        def _():
            dw_ref[...] = jnp.zeros_like(dw_ref)

        b_g, c_g, x_in, gate = _sc_parts(p_ref[...], width)
        w, dy = w_ref[...], dy_ref[...]
        u = c_g * x_in
        z = _conv(u, w, SC_CONV_OFFSETS, (GRID_W, 0))
        sg = _silu(gate)
        du, dw = _conv_bwd(u, w, dy * b_g * sg, SC_CONV_OFFSETS, (GRID_W, 0))
        dw_ref[...] += dw
        parts = (dy * z * sg, du * x_in, du * c_g, dy * b_g * z * _dsilu(gate))
        for k, part in enumerate(parts):
            dp_ref[:, k * width:(k + 1) * width] = part.astype(dp_ref.dtype)

    return pl.pallas_call(
        body, name=name, grid=(s // tr,),
        in_specs=[pl.BlockSpec((tr, 4 * width), lambda i: (i, 0)), pl.BlockSpec((tr, width), lambda i: (i, 0)),
                  pl.BlockSpec(conv_w.shape, lambda i: (0, 0))],
        out_specs=[pl.BlockSpec((tr, 4 * width), lambda i: (i, 0)), pl.BlockSpec(conv_w.shape, lambda i: (0, 0))],
        out_shape=[jax.ShapeDtypeStruct((s, 4 * width), ACT_DTYPE), jax.ShapeDtypeStruct(conv_w.shape, F32)],
        compiler_params=_params("arbitrary"))(p, d_y, conv_w)


MOD_ROWS = 16


def mod_fwd(cond, mod_w, mod_b, *, name):
    nl, d, ns = mod_w.shape
    tn = _tile(ns, 512, LANES)

    def body(c_ref, w_ref, b_ref, o_ref):
        o_ref[...] = _dot(_silu(c_ref[...]), w_ref[...], _NN) + b_ref[...]

    return pl.pallas_call(
        body, name=name, grid=(nl, ns // tn),
        in_specs=[pl.BlockSpec((MOD_ROWS, d), lambda l, j: (0, 0)), pl.BlockSpec((None, d, tn), lambda l, j: (l, 0, j)),
                  pl.BlockSpec((None, 1, tn), lambda l, j: (l, 0, j))],
        out_specs=pl.BlockSpec((None, MOD_ROWS, tn), lambda l, j: (l, 0, j)),
        out_shape=jax.ShapeDtypeStruct((nl, MOD_ROWS, ns), F32),
        compiler_params=_params("parallel", "parallel"))(cond, mod_w, mod_b)


def _adamw(w, g, m, v):
    m = ADAM_B1 * m + (1.0 - ADAM_B1) * g
    v = ADAM_B2 * v + (1.0 - ADAM_B2) * (g * g)
    m_hat = m / (1.0 - ADAM_B1 ** ADAM_STEP)
    v_hat = v / (1.0 - ADAM_B2 ** ADAM_STEP)
    return -ADAM_LR * (m_hat / (jnp.sqrt(v_hat) + ADAM_EPS) + ADAM_WD * w), m, v


def mod_adam(cond, d_mod, w, m, v, *, name):
    nl, d, ns = w.shape
    tr, tn = _tile(d, 256, SUBLANES), _tile(ns, 1024, LANES)

    def body(c_ref, dm_ref, w_ref, m_ref, v_ref, g_ref, dl_ref, nm_ref, nv_ref, ds_ref):
        @pl.when(pl.program_id(2) == 0)
        def _():
            ds_ref[...] = jnp.zeros_like(ds_ref)

        wv, dm = w_ref[...], dm_ref[...]
        g = _dot(_silu(c_ref[...]), dm, _TN)
        ds_ref[...] += _dot(dm, wv, _NT)
        g_ref[...] = g
        dl_ref[...], nm_ref[...], nv_ref[...] = _adamw(wv, g, m_ref[...], v_ref[...])

    blk = pl.BlockSpec((None, tr, tn), lambda l, i, j: (l, i, j))
    full = jax.ShapeDtypeStruct(w.shape, F32)
    return pl.pallas_call(
        body, name=name, grid=(nl, d // tr, ns // tn),
        in_specs=[pl.BlockSpec((MOD_ROWS, tr), lambda l, i, j: (0, i)),
                  pl.BlockSpec((None, MOD_ROWS, tn), lambda l, i, j: (l, 0, j)), blk, blk, blk],
        out_specs=[blk, blk, blk, blk, pl.BlockSpec((None, MOD_ROWS, tr), lambda l, i, j: (l, 0, i))],
        out_shape=[full, full, full, full, jax.ShapeDtypeStruct((nl, MOD_ROWS, d), F32)],
        compiler_params=_params("parallel", "parallel", "arbitrary"))(cond, d_mod, w, m, v)


def _row_tile(rows, cols, itemsize, mult):
    return _tile(rows, max(mult, (2 << 20) // (cols * itemsize)), mult)


def adam_update(w, g, m, v, *, name):
    r, c = w.shape
    tr = _row_tile(r, c, 4, SUBLANES)

    def body(w_ref, g_ref, m_ref, v_ref, dl_ref, nm_ref, nv_ref):
        dl_ref[...], nm_ref[...], nv_ref[...] = _adamw(w_ref[...], g_ref[...], m_ref[...], v_ref[...])

    blk = pl.BlockSpec((tr, c), lambda i: (i, 0))
    return pl.pallas_call(
        body, name=name, grid=(r // tr,), in_specs=[blk] * 4, out_specs=[blk] * 3,
        out_shape=[jax.ShapeDtypeStruct((r, c), F32)] * 3, compiler_params=_params("parallel"))(w, g, m, v)


def cast_rows(x, dtype, *, name):
    r, c = x.shape
    tr = _row_tile(r, c, 4, 16)

    def body(x_ref, o_ref):
        o_ref[...] = x_ref[...].astype(dtype)

    blk = pl.BlockSpec((tr, c), lambda i: (i, 0))
    return pl.pallas_call(body, name=name, grid=(r // tr,), in_specs=[blk], out_specs=blk,
                          out_shape=jax.ShapeDtypeStruct((r, c), dtype), compiler_params=_params("parallel"))(x)


def add_sibling_half(core, mine, other, *, name):
    a, _, r, c = mine.shape
    tr = _row_tile(r, c, 4, 16)

    def body(core_ref, x_ref, y_ref, o_ref):
        o_ref[...] = (x_ref[...].astype(F32) + y_ref[...].astype(F32)).astype(o_ref.dtype)

    grid_spec = pltpu.PrefetchScalarGridSpec(
        num_scalar_prefetch=1, grid=(a, r // tr),
        in_specs=[pl.BlockSpec((None, None, tr, c), lambda k, i, core_ref: (k, core_ref[0], i, 0)),
                  pl.BlockSpec((None, tr, c), lambda k, i, core_ref: (k, i, 0))],
        out_specs=pl.BlockSpec((None, tr, c), lambda k, i, core_ref: (k, i, 0)))
    return pl.pallas_call(body, name=name, grid_spec=grid_spec, out_shape=jax.ShapeDtypeStruct((a, r, c), WIRE_DTYPE),
                          compiler_params=_params("parallel", "parallel"))(core, mine, other)


def sum_slots(x, *, name):
    n, r, c = x.shape
    tr = _row_tile(r, c * n, 4, 16)

    def body(x_ref, o_ref):
        acc = x_ref[0].astype(F32)
        for k in range(1, n):
            acc = acc + x_ref[k].astype(F32)
        o_ref[...] = acc

    return pl.pallas_call(
        body, name=name, grid=(r // tr,), in_specs=[pl.BlockSpec((n, tr, c), lambda i: (0, i, 0))],
        out_specs=pl.BlockSpec((tr, c), lambda i: (i, 0)), out_shape=jax.ShapeDtypeStruct((r, c), F32),
        compiler_params=_params("parallel"))(x)


def ctx_cond_grad(parts, c_ctx, *, name):
    def body(p_ref, c_ref, o_ref):
        acc = p_ref[0]
        for k in range(1, N_CHIPS):
            acc = acc + p_ref[k]
        o_ref[...] = acc * _dsilu(c_ref[...])

    return pl.pallas_call(body, name=name, out_shape=jax.ShapeDtypeStruct(c_ctx.shape, F32))(parts, c_ctx)


PACK_ROWS = 256


def _tile_rows(shape):
    n = 1
    for dim in shape:
        n *= dim
    return -(-n // (SUBLANES * LANES)) * SUBLANES


def _pack(arrs):
    parts = []
    for a in arrs:
        flat = a.reshape(-1).astype(F32)
        rows = _tile_rows(a.shape)
        parts.append(jnp.pad(flat, (0, rows * LANES - flat.shape[0])).reshape(rows, LANES))
    total = sum(t.shape[0] for t in parts)
    parts.append(jnp.zeros((-total % PACK_ROWS, LANES), F32))
    return jnp.concatenate(parts, axis=0)


def _unpack(flat, shapes):
    lead = flat.shape[:-2]
    outs, r0 = [], 0
    for shape in shapes:
        rows = _tile_rows(shape)
        n = 1
        for dim in shape:
            n *= dim
        piece = flat[..., r0:r0 + rows, :].reshape(lead + (rows * LANES,))[..., :n]
        outs.append(piece.reshape(lead + tuple(shape)))
        r0 += rows
    return outs


WEIGHTS = ('c_ctx', 'mod_w', 'mod_b', 'norm_w', 'ab_w_in', 'ab_qkv_conv', 'ab_a_log', 'ab_dt_bias', 'ab_dn_norm',
           'ab_lru_conv_w', 'ab_lru_conv_b', 'ab_lru_w_r', 'ab_lru_b_r', 'ab_lru_w_i', 'ab_lru_b_i', 'ab_lru_lambda',
           'ab_w_out', 'sc_w_in', 'sc_conv', 'sc_w_out', 'final_norm_w')
BIG_WEIGHTS = ('ab_w_in', 'ab_w_out', 'sc_w_in', 'sc_w_out')


def kernel(x, c, ctx, c_ctx, mod_w, mod_b, norm_w, ab_w_in, ab_qkv_conv, ab_a_log, ab_dt_bias, ab_dn_norm, ab_lru_conv_w, ab_lru_conv_b, ab_lru_w_r, ab_lru_b_r, ab_lru_w_i, ab_lru_b_i, ab_lru_lambda, ab_w_out, sc_w_in, sc_conv, sc_w_out, final_norm_w, loss_target, m_c_ctx, m_mod_w, m_mod_b, m_norm_w, m_ab_w_in, m_ab_qkv_conv, m_ab_a_log, m_ab_dt_bias, m_ab_dn_norm, m_ab_lru_conv_w, m_ab_lru_conv_b, m_ab_lru_w_r, m_ab_lru_b_r, m_ab_lru_w_i, m_ab_lru_b_i, m_ab_lru_lambda, m_ab_w_out, m_sc_w_in, m_sc_conv, m_sc_w_out, m_final_norm_w, v_c_ctx, v_mod_w, v_mod_b, v_norm_w, v_ab_w_in, v_ab_qkv_conv, v_ab_a_log, v_ab_dt_bias, v_ab_dn_norm, v_ab_lru_conv_w, v_ab_lru_conv_b, v_ab_lru_w_r, v_ab_lru_b_r, v_ab_lru_w_i, v_ab_lru_b_i, v_ab_lru_lambda, v_ab_w_out, v_sc_w_in, v_sc_conv, v_sc_w_out, v_final_norm_w):
    given = dict(locals())
    weights = {n: given[n] for n in WEIGHTS}
    mom1 = {n: given['m_' + n] for n in WEIGHTS}
    mom2 = {n: given['v_' + n] for n in WEIGHTS}

    xi, yi, ci = lax.axis_index("x"), lax.axis_index("y"), lax.axis_index("c")
    chip = 2 * xi + yi
    dev = 2 * chip + ci
    x2d, ctx2d, target = x[0], ctx[0], loss_target[0]
    s, d = x2d.shape
    cl = ctx2d.shape[0]
    heads = ab_a_log.shape[-1]
    wdn = heads * LANES
    nb = ab_lru_w_r.shape[2]
    wl = nb * ab_lru_w_r.shape[3]
    sc = sc_w_in.shape[-1]
    ab_out = wdn + wl
    off_beta = 3 * wdn + wl
    ab_state = off_beta + 4 * heads
    ab_in = ab_state + wdn + wl
    ni = ab_in // N_CHIPS
    ns = mod_w.shape[-1]
    grid_rows = s // GRID_W

    def to_col_major(t):
        return t.reshape(grid_rows, GRID_W, t.shape[-1]).swapaxes(0, 1).reshape(s, t.shape[-1])

    def to_raster(t):
        return t.reshape(GRID_W, grid_rows, t.shape[-1]).swapaxes(0, 1).reshape(s, t.shape[-1])

    def from_chips(t):
        return jnp.moveaxis(t[0::N_CORES], 0, 1).reshape(t.shape[1], -1)

    def own_columns(t, width):
        return lax.dynamic_slice_in_dim(t, chip * width, width, axis=t.ndim - 1)

    small_shards = [c[0], ab_qkv_conv[0], ab_lru_conv_w[0], ab_lru_b_r[0], ab_lru_b_i[0], ab_lru_lambda[0], sc_conv[0]]
    gathered0 = all_gather_devices(_pack(small_shards), "ag_small_params")
    c_all, qkv_sh, lcw_sh, lbr_sh, lbi_sh, llam_sh, scv_sh = _unpack(gathered0, [t.shape for t in small_shards])
    qkv_conv, lru_conv_w, sc_conv_w = from_chips(qkv_sh), from_chips(lcw_sh), from_chips(scv_sh)
    lru_b_r, lru_b_i, lru_lam = from_chips(lbr_sh), from_chips(lbi_sh), from_chips(llam_sh)

    shards = [weights[n][0] for n in BIG_WEIGHTS]
    halves = [cast_rows(t, WIRE_DTYPE, name=f"cast_w{k}").reshape(N_CORES, t.shape[0] // N_CORES, t.shape[1])
              for k, t in enumerate(shards)]
    from_chips_half = chip_exchange(halves[:1], gather=True, name="ag_w_chips")
    from_sibling_half = sibling_exchange(from_chips_half, pick=False, name="ag_w_cores")
    w_in_full = core_halves(ci, from_chips_half[0], from_sibling_half[0])
    w_in_full = jnp.moveaxis(w_in_full.reshape(N_CHIPS, d, ni), 0, 1).reshape(d, ab_in)
    w_main = jnp.concatenate([w_in_full[:, :off_beta], w_in_full[:, ab_state:]], axis=1)
    w_ba = jnp.pad(w_in_full[:, off_beta:ab_state], ((0, 0), (0, LANES - 4 * heads)))

    cond = jnp.zeros((MOD_ROWS, d), F32).at[:N_DEV].set(c_all).at[N_DEV].set(c_ctx)
    mod_shard = mod_fwd(cond, mod_w, own_columns(mod_b, ns)[:, None, :], name="mod_fwd")
    gathered_mod = all_gather_devices(mod_shard.reshape(-1, LANES), "ag_mod")
    mod_all = jnp.moveaxis(gathered_mod[0::N_CORES].reshape(N_CHIPS, 2, MOD_ROWS, ns), 0, 2).reshape(2, MOD_ROWS, 3 * d)
    own_mod = lax.dynamic_index_in_dim(mod_all, dev, axis=1, keepdims=False)
    shift, scale, gate = own_mod[:, :d], own_mod[:, d:2 * d], own_mod[:, 2 * d:]
    shift_c, scale_c = mod_all[0, N_DEV, :d], mod_all[0, N_DEV, d:2 * d]

    def pair(a, b):
        return jnp.stack([a, b])[:, None, :]

    hn_all = norm_mod_fwd(x2d, ctx2d, norm_w[0:1], pair(scale[0], scale_c), pair(shift[0], shift_c), name="norm0_fwd")
    proj = mm_nn(hn_all, w_main, out_dtype=F32, name="proj0")
    ba = mm_nn(hn_all, w_ba, out_dtype=F32, name="proj0_ba")
    qkv = qkv_conv_fwd(proj, qkv_conv, s=s, cl=cl, heads=heads, name="qkv_conv_fwd")

    def pad_dh(t):
        return jnp.zeros((SUBLANES, LANES), F32).at[:2, :heads].set(t)

    a_log_t, dt_bias_t = pad_dh(ab_a_log[0]), pad_dh(ab_dt_bias[0])
    gate_dn0 = (3 * wdn + wl) // LANES
    gate_lru0 = gate_dn0 + wdn // LANES
    o_dn, dn_states, late_from_chips = dn_fwd(qkv, ba, a_log_t, dt_bias_t, SideExchange(halves[1:], gather=True), s=s,
                                              cl=cl, heads=heads, name="dn_fwd")
    late_from_sibling = sibling_exchange(list(late_from_chips), pick=False, name="ag_w_cores_late")
    w_ab_out, w_sc_in, w_sc_out = (core_halves(ci, a, b) for a, b in zip(late_from_chips, late_from_sibling))
    w_ab_out, w_sc_in, w_sc_out = w_ab_out.reshape(ab_out, d), w_sc_in.reshape(N_CHIPS, d, sc), w_sc_out.reshape(sc, d)
    y_dn = dn_out_fwd(o_dn, proj, ab_dn_norm, heads=heads, gate_col0=gate_dn0, name="dn_out_fwd")
    lru_in = proj[:, 3 * wdn:3 * wdn + wl]
    xcm = jnp.concatenate([to_col_major(lru_in[:s]), lru_in[s:]], axis=0)
    lru_w = (lru_conv_w, ab_lru_conv_b, ab_lru_w_r[0], lru_b_r, ab_lru_w_i[0], lru_b_i, lru_lam)
    h_lru = to_raster(lru_fwd(xcm, *lru_w, s=s, cl=cl, name="lru_fwd"))
    y_lru = lru_gate_fwd(h_lru, proj, gate_col0=gate_lru0, name="lru_gate_fwd")
    y_ab = jnp.concatenate([y_dn, y_lru], axis=1)
    x1, yo0 = mm_nn(y_ab, w_ab_out, out_dtype=F32, name="out0", resid=x2d, gate=gate[0:1], aux_dtype=ACT_DTYPE,
                    tn_cap=512)

    hn1 = norm_mod_fwd(x1, None, norm_w[1:2], scale[1][None, None, :], shift[1][None, None, :], name="norm1_fwd")
    tm1, tn1, tk1 = _tile(s, 1024, 16), _tile(sc, 1024, LANES), _tile(d, MATMUL_TK, LANES)
    p = matmul(hn1, w_sc_in, m=s, n=4 * sc, k=d, tm=tm1, tn=tn1, tk=tk1, dims=_NN, out_dtype=F32, name="proj1",
               a_spec=pl.BlockSpec((tm1, tk1), lambda i, j, kk: (i, kk)),
               b_spec=pl.BlockSpec((None, tk1, tn1), lambda i, j, kk: (j // (sc // tn1), kk, j % (sc // tn1))))
    y_sc = sc_mix_fwd(p, sc_conv_w, name="sc_mix_fwd")
    x2, yo1 = mm_nn(y_sc, w_sc_out, out_dtype=F32, name="out1", resid=x1, gate=gate[1:2], aux_dtype=ACT_DTYPE,
                    tn_cap=512)
    loss_t, dx2, d_fnw = loss_head(x2, final_norm_w[None], target, name="loss_head")

    d_yo1, d_gate1 = gate_bwd(dx2, yo1, gate[1:2], name="gate1_bwd")
    d_ysc = mm_nt(d_yo1, w_sc_out, out_dtype=F32, name="out1_dx")
    dw_sc_out = mm_tn(y_sc, d_yo1, out_dtype=WIRE_DTYPE, name="out1_dw")
    d_p, d_sc_conv = sc_mix_bwd(p, d_ysc, sc_conv_w, name="sc_mix_bwd")
    tkp = _tile(sc, MATMUL_TK, LANES)
    tnd = _tile(d, 1024, LANES)
    d_hn1 = matmul(d_p, w_sc_in, m=s, n=d, k=4 * sc, tm=tm1, tn=tnd, tk=tkp, dims=_NT, out_dtype=F32, name="proj1_dx",
                   a_spec=pl.BlockSpec((tm1, tkp), lambda i, j, kk: (i, kk)),
                   b_spec=pl.BlockSpec((None, tnd, tkp), lambda i, j, kk: (kk // (sc // tkp), j, kk % (sc // tkp))))
    tks = _tile(s, MATMUL_TK_TOKENS, 16)
    dw_sc_in = matmul(hn1, d_p, m=d, n=4 * sc, k=s, tm=tnd, tn=tn1, tk=tks, dims=_TN, out_dtype=WIRE_DTYPE,
                      name="proj1_dw", a_spec=pl.BlockSpec((tks, tnd), lambda i, j, kk: (kk, i)),
                      b_spec=pl.BlockSpec((tks, tn1), lambda i, j, kk: (kk, j)),
                      out_spec=pl.BlockSpec((None, tnd, tn1), lambda i, j, kk: (j // (sc // tn1), i, j % (sc // tn1))),
                      out_shape=(N_CHIPS, d, sc))
    dx1, d_nw1, d_scale1, d_shift1 = norm_mod_bwd(x1, norm_w[1:2], scale[1:2], d_hn1, row0=0, resid=dx2, init=None,
                                                  name="norm1_bwd")

    d_yo0, d_gate0 = gate_bwd(dx1, yo0, gate[0:1], name="gate0_bwd")
    d_y = mm_nt(d_yo0, w_ab_out, out_dtype=F32, name="out0_dx")
    dw_ab_out = mm_tn(y_ab, d_yo0, out_dtype=WIRE_DTYPE, name="out0_dw")
    d_o, d_gate_dn, d_dn_norm = dn_out_bwd(o_dn, proj, d_y, ab_dn_norm, heads=heads, gate_col0=gate_dn0, name="dn_out_bwd")
    core = ci.astype(jnp.int32).reshape(1)

    def chip_core_slots(t, slot_rows):
        return t.reshape(N_CHIPS, N_CORES, slot_rows // N_CORES, t.shape[-1])

    early_by_chip = [chip_core_slots(dw_ab_out, ab_out // N_CHIPS), chip_core_slots(dw_sc_in, d),
                     chip_core_slots(dw_sc_out, sc // N_CHIPS)]
    early_from_sib = sibling_exchange(early_by_chip, pick=True, name="rs_cores_early")
    early_halves_sum = [add_sibling_half(core, a, b, name=f"rs_add_early{k}")
                        for k, (a, b) in enumerate(zip(early_by_chip, early_from_sib))]
    d_qkv, d_ba, d_alog_t, d_dtb_t, early_chips_sum = dn_bwd(qkv, ba, d_o, dn_states, a_log_t, dt_bias_t,
                                                             SideExchange(early_halves_sum, gather=False), s=s, cl=cl,
                                                             heads=heads, name="dn_bwd")
    d_qkv_raw, d_qkv_conv = qkv_conv_bwd(proj, qkv_conv, d_qkv, s=s, cl=cl, heads=heads, name="qkv_conv_bwd")
    d_h, d_gate_lru = lru_gate_bwd(h_lru, proj, d_y, gate_col0=gate_lru0, dy_col0=wdn // LANES, name="lru_gate_bwd")
    (d_xcm, d_lcw, d_lcb, d_wr, d_br, d_wi, d_bi, d_lam) = lru_bwd(xcm, to_col_major(d_h), *lru_w, s=s, cl=cl,
                                                                  name="lru_bwd")
    d_lru_in = jnp.concatenate([to_raster(d_xcm[:s]), d_xcm[s:]], axis=0)
    ctx_zeros = jnp.zeros((cl, wdn + wl), ACT_DTYPE)
    d_gates = jnp.concatenate([jnp.concatenate([d_gate_dn, d_gate_lru], axis=1), ctx_zeros], axis=0)
    d_proj = jnp.concatenate([d_qkv_raw, d_lru_in, d_gates], axis=1)
    dw_main = mm_tn(hn_all, d_proj, out_dtype=WIRE_DTYPE, name="proj0_dw")
    dw_ba = mm_tn(hn_all, d_ba, out_dtype=WIRE_DTYPE, name="proj0_ba_dw")
    dw_in_full = jnp.concatenate([dw_main[:, :off_beta], dw_ba[:, :4 * heads], dw_main[:, off_beta:]], axis=1)
    by_chip = [chip_core_slots(jnp.moveaxis(dw_in_full.reshape(d, N_CHIPS, ni), 1, 0), d)]
    from_sibling = sibling_exchange(by_chip, pick=True, name="rs_cores")
    halves_sum = [add_sibling_half(core, by_chip[0], from_sibling[0], name="rs_add")]
    d_hn_ba = mm_nt(d_ba, w_ba, out_dtype=F32, name="proj0_ba_dx")
    d_hn_all, last_chips_sum = mm_nt(d_proj, w_main, out_dtype=F32, name="proj0_dx", resid=d_hn_ba,
                                     side=SideExchange(halves_sum, gather=False))
    _, d_nw0c, d_scale_c, d_shift_c = norm_mod_bwd(ctx2d, norm_w[0:1], scale_c[None], d_hn_all, row0=s, resid=None,
                                                   init=None, name="norm0_bwd_ctx")
    dx, d_nw0, d_scale0, d_shift0 = norm_mod_bwd(x2d, norm_w[0:1], scale[0:1], d_hn_all, row0=0, resid=dx1, init=d_nw0c,
                                                 name="norm0_bwd")

    from_chips_sum = list(last_chips_sum) + list(early_chips_sum)
    reduced = [sum_slots(t, name=f"rs_sum{k}")[None] for k, t in enumerate(from_chips_sum)]
    both_halves = [core_halves(ci, a, b) for a, b in zip(reduced, sibling_exchange(reduced, pick=False, name="rs_back"))]
    grads = {n: t.reshape(weights[n].shape) for n, t in zip(BIG_WEIGHTS, both_halves)}

    d_mod_own = jnp.stack([jnp.concatenate([d_shift0[0], d_scale0[0], d_gate0[0]]),
                           jnp.concatenate([d_shift1[0], d_scale1[0], d_gate1[0]])])
    d_mod_ctx = jnp.concatenate([d_shift_c[0], d_scale_c[0], jnp.zeros((d,), F32)])
    summable = [
        loss_t[0, 0:1], d_mod_own.at[0].add(d_mod_ctx), d_mod_ctx, jnp.concatenate([d_nw0, d_nw1], axis=0), d_qkv_conv,
        d_alog_t[:, :2, 0].T, d_dtb_t[:, :2, 0].T, d_dn_norm[0], d_lcw, d_lcb[0], d_wr, d_br, d_wi, d_bi, d_lam,
        d_sc_conv, d_fnw[0]]
    sum_shapes = [t.shape for t in summable]
    gathered1 = all_gather_devices(_pack(summable + [d_mod_own]), "ag_small_grads")
    totals = _unpack(sum_slots(gathered1, name="sum_small_grads"), sum_shapes)
    (loss_sum, g_mod_b, d_mod_ctx_sum, g_norm_w, g_qkv_conv, g_a_log, g_dt_bias, g_dn_norm, g_lcw, g_lcb, g_wr, g_br,
     g_wi, g_bi, g_lam, g_sc_conv, g_fnw) = totals
    d_mod_rows = _unpack(gathered1, sum_shapes + [d_mod_own.shape])[-1]

    d_mod_all = jnp.zeros((2, MOD_ROWS, 3 * d), F32).at[:, :N_DEV].set(jnp.moveaxis(d_mod_rows, 0, 1))
    d_mod_all = d_mod_all.at[0, N_DEV].set(d_mod_ctx_sum)
    g_mod_w, dl_mod_w, nm_mod_w, nv_mod_w, d_silu = mod_adam(cond, own_columns(d_mod_all, ns), mod_w, m_mod_w, v_mod_w,
                                                             name="mod_adam")
    gathered2 = all_gather_devices(d_silu[0, N_DEV].reshape(-1, LANES), "ag_ctx_cond")
    g_c_ctx = ctx_cond_grad(gathered2[0::N_CORES], c_ctx.reshape(-1, LANES), name="ctx_cond_grad").reshape(d)

    grads.update({
        'c_ctx': g_c_ctx, 'mod_w': g_mod_w, 'mod_b': g_mod_b, 'norm_w': g_norm_w,
        'ab_qkv_conv': own_columns(g_qkv_conv, qkv_conv.shape[1] // N_CHIPS), 'ab_a_log': g_a_log, 'ab_dt_bias': g_dt_bias,
        'ab_dn_norm': g_dn_norm, 'ab_lru_conv_w': own_columns(g_lcw, wl // N_CHIPS), 'ab_lru_conv_b': g_lcb,
        'ab_lru_w_r': g_wr, 'ab_lru_b_r': own_columns(g_br, wl // N_CHIPS), 'ab_lru_w_i': g_wi,
        'ab_lru_b_i': own_columns(g_bi, wl // N_CHIPS), 'ab_lru_lambda': own_columns(g_lam, wl // N_CHIPS),
        'sc_conv': own_columns(g_sc_conv, sc // N_CHIPS), 'final_norm_w': g_fnw})
    grads = {n: grads[n].reshape(weights[n].shape) for n in WEIGHTS}

    delta, new_m, new_v = {'mod_w': dl_mod_w}, {'mod_w': nm_mod_w}, {'mod_w': nv_mod_w}
    for k, n in enumerate(BIG_WEIGHTS):
        as2d = lambda t: t.reshape(-1, t.shape[-1])
        upd = adam_update(as2d(weights[n]), as2d(grads[n]), as2d(mom1[n]), as2d(mom2[n]), name=f"adam_big{k}")
        delta[n], new_m[n], new_v[n] = (t.reshape(weights[n].shape) for t in upd)
    small = [n for n in WEIGHTS if n not in BIG_WEIGHTS and n != 'mod_w']
    small_shapes = [weights[n].shape for n in small]
    upd = adam_update(*[_pack([src[n] for n in small]) for src in (weights, grads, mom1, mom2)], name="adam_small")
    for out, flat in zip((delta, new_m, new_v), upd):
        out.update(dict(zip(small, _unpack(flat, small_shapes))))

    return (loss_sum[0], dx[None], *[grads[n] for n in WEIGHTS], *[delta[n] for n in WEIGHTS],
            *[new_m[n] for n in WEIGHTS], *[new_v[n] for n in WEIGHTS])
```

```python
import functools

import jax
import jax.numpy as jnp
from jax import lax
from jax.experimental import pallas as pl
from jax.experimental.pallas import tpu as pltpu

F32 = jnp.float32
BF16 = jnp.bfloat16
MXU_DTYPE = BF16
ACT_DTYPE = BF16
WIRE_DTYPE = BF16

EPS = 1e-6
GRID_W = 64
CHUNK = 64
DN_CONV_OFFSETS = (-2, -1, 0, 1)
SC_CONV_OFFSETS = (-1, 0, 1)
LRU_C = 8.0
ADAM_LR, ADAM_B1, ADAM_B2, ADAM_EPS, ADAM_WD, ADAM_STEP = 0.001, 0.9, 0.999, 1e-08, 0.01, 10

LANES = 128
SUBLANES = 8
N_CHIPS, N_CORES = 4, 2
N_DEV = N_CHIPS * N_CORES
INV_PRECISION = None
INV_BLOCK = 16

_MESH = pl.DeviceIdType.MESH
_ANY = pl.BlockSpec(memory_space=pl.ANY)
_NN = (((1,), (0,)), ((), ()))
_NT = (((1,), (1,)), ((), ()))
_TN = (((0,), (0,)), ((), ()))


def _tile(n, cap, mult):
    best = None
    for t in range(mult, min(n, cap) + 1, mult):
        if n % t == 0:
            best = t
    return n if best is None else best


def _iota(shape, dim):
    return lax.broadcasted_iota(jnp.int32, shape, dim)


def _dot(a, b, dims):
    return lax.dot_general(a.astype(MXU_DTYPE), b.astype(MXU_DTYPE), dims, preferred_element_type=F32)


def _silu(x):
    return x * jax.nn.sigmoid(x)


def _dsilu(x):
    s = jax.nn.sigmoid(x)
    return s * (1.0 + x * (1.0 - s))


V7X_VMEM_BYTES = 64 * 1024 * 1024
BIG_KERNEL_VMEM = V7X_VMEM_BYTES * 15 // 16
MATMUL_TK = 4096
MATMUL_TK_TOKENS = 8192
MATMUL_VMEM = V7X_VMEM_BYTES * 7 // 8


def _params(*sem, vmem=None):
    return pltpu.CompilerParams(dimension_semantics=sem, vmem_limit_bytes=vmem)


def _place():
    return lax.axis_index("x"), lax.axis_index("y"), lax.axis_index("c")


def all_gather_devices(block, name):
    def body(x_ref, out_ref, send_sems, recv_sems, local_sem):
        x, y, c = _place()
        me, sibling = (x, y, c), (x, y, 1 - c)
        chips = [(1 - x, y), (x, 1 - y), (1 - x, 1 - y)]

        def slot(px, py, pc):
            return out_ref.at[4 * px + 2 * py + pc]

        def copy(k, block_of, to, src=None):
            return pltpu.make_async_remote_copy(
                src_ref=slot(*block_of) if src is None else src, dst_ref=slot(*block_of),
                send_sem=send_sems.at[k], recv_sem=recv_sems.at[k], device_id=to, device_id_type=_MESH)

        mine = pltpu.make_async_copy(x_ref, slot(*me), local_sem)
        mine.start()
        first = [copy(0, me, sibling, src=x_ref)]
        first += [copy(1 + j, me, (*chip, c), src=x_ref) for j, chip in enumerate(chips)]
        for cp in first:
            cp.start()
        passed = [copy(4 + j, (*chip, c), sibling) for j, chip in enumerate(chips)]
        for j, chip in enumerate(chips):
            copy(1 + j, (*chip, c), me).wait_recv()
            passed[j].start()
        copy(0, sibling, me).wait_recv()
        for j, chip in enumerate(chips):
            copy(4 + j, (*chip, 1 - c), me).wait_recv()
        for cp in first + passed:
            cp.wait_send()
        mine.wait()

    return pl.pallas_call(
        body, name=name,
        out_shape=jax.ShapeDtypeStruct((N_DEV,) + block.shape, block.dtype),
        in_specs=[_ANY], out_specs=_ANY,
        scratch_shapes=[pltpu.SemaphoreType.DMA((7,)), pltpu.SemaphoreType.DMA((7,)), pltpu.SemaphoreType.DMA],
    )(block)


def chip_exchange(srcs, *, gather, name):
    n = len(srcs)

    def body(*refs):
        copies = _chip_copies(refs[:n], refs[n:2 * n], refs[2 * n:], gather)
        for cp in copies:
            cp.start()
        for cp in copies:
            cp.wait()

    return pl.pallas_call(
        body, name=name, out_shape=_chip_out_shapes(srcs), in_specs=[_ANY] * n, out_specs=[_ANY] * n,
        scratch_shapes=_chip_sems(n),
    )(*srcs)


def _chip_copies(src, out, sems, gather):
    send_sems, recv_sems, local_sems = sems
    x, y, c = _place()
    my = 2 * x + y
    peers = [(1 - x, y), (x, 1 - y), (1 - x, 1 - y)]
    copies = []
    for k in range(len(src)):
        own = src[k].at[c] if gather else src[k].at[my]
        copies.append(pltpu.make_async_copy(own, out[k].at[my], local_sems.at[k]))
        for j, (px, py) in enumerate(peers):
            copies.append(pltpu.make_async_remote_copy(
                src_ref=own if gather else src[k].at[2 * px + py], dst_ref=out[k].at[my],
                send_sem=send_sems.at[k, j], recv_sem=recv_sems.at[k, j],
                device_id=(px, py, c), device_id_type=_MESH))
    return copies


def _chip_out_shapes(srcs):
    return [jax.ShapeDtypeStruct((N_CHIPS,) + s.shape[1:], s.dtype) for s in srcs]


def _chip_sems(n):
    return [pltpu.SemaphoreType.DMA((n, 3)), pltpu.SemaphoreType.DMA((n, 3)), pltpu.SemaphoreType.DMA((n,))]


class SideExchange:
    def __init__(self, srcs, gather):
        self.srcs, self.gather, self.n = list(srcs), gather, len(srcs)

    def specs(self):
        return [_ANY] * self.n, _chip_out_shapes(self.srcs), [_ANY] * self.n, _chip_sems(self.n)

    def run(self, src_refs, out_refs, sem_refs, step, last_step, compute):
        @pl.when(step == 0)
        def _():
            for cp in _chip_copies(src_refs, out_refs, sem_refs, self.gather):
                cp.start()

        compute()

        @pl.when(step == last_step)
        def _():
            for cp in _chip_copies(src_refs, out_refs, sem_refs, self.gather):
                cp.wait()


def sibling_exchange(srcs, *, pick, name):
    n = len(srcs)

    def body(*refs):
        src, out = refs[:n], refs[n:2 * n]
        send_sems, recv_sems = refs[2 * n:]
        x, y, c = _place()
        copies = []
        for k in range(n):
            s_ref = src[k].at[pl.ds(0, src[k].shape[0]), 1 - c] if pick else src[k]
            cp = pltpu.make_async_remote_copy(
                src_ref=s_ref, dst_ref=out[k], send_sem=send_sems.at[k], recv_sem=recv_sems.at[k],
                device_id=(x, y, 1 - c), device_id_type=_MESH)
            cp.start()
            copies.append(cp)
        for cp in copies:
            cp.wait()

    outs = [jax.ShapeDtypeStruct(s.shape[:1] + s.shape[2:] if pick else s.shape, s.dtype) for s in srcs]
    return pl.pallas_call(
        body, name=name, out_shape=outs, in_specs=[_ANY] * n, out_specs=[_ANY] * n,
        scratch_shapes=[pltpu.SemaphoreType.DMA((n,)), pltpu.SemaphoreType.DMA((n,))],
    )(*srcs)


def core_halves(core, mine, other):
    return jnp.where(core == 0, jnp.stack([mine, other], axis=1), jnp.stack([other, mine], axis=1))


def matmul(a, b, *, m, n, k, tm, tn, tk, a_spec, b_spec, dims, out_dtype, name,
           out_spec=None, out_shape=None, resid=None, gate=None, aux_dtype=None, side=None):
    nk = k // tk
    o_spec = out_spec or pl.BlockSpec((tm, tn), lambda i, j, kk: (i, j))
    o_shape = out_shape or (m, n)
    n_in = 2 + (resid is not None) + (gate is not None)
    n_out = 1 + (aux_dtype is not None)
    n_side = 0 if side is None else side.n
    side_in, side_shapes, side_out, side_sems = ([], [], [], []) if side is None else side.specs()

    def body(*refs):
        if side is None:
            return compute(*refs)
        outs0 = n_in + n_side
        scratch = refs[outs0 + n_out + n_side:]
        main = refs[:n_in] + refs[outs0:outs0 + n_out] + scratch[:len(scratch) - len(side_sems)]
        step = (pl.program_id(0) * (n // tn) + pl.program_id(1)) * nk + pl.program_id(2)
        side.run(refs[n_in:outs0], refs[outs0 + n_out:outs0 + n_out + n_side], scratch[len(scratch) - len(side_sems):],
                 step, (m // tm) * (n // tn) * nk - 1, lambda: compute(*main))

    def compute(*refs):
        a_ref, b_ref = refs[0], refs[1]
        pos = 2
        r_ref = g_ref = aux_ref = None
        if resid is not None:
            r_ref, pos = refs[pos], pos + 1
        if gate is not None:
            g_ref, pos = refs[pos], pos + 1
        o_ref, pos = refs[pos], pos + 1
        if aux_dtype is not None:
            aux_ref, pos = refs[pos], pos + 1
        prod = _dot(a_ref[...], b_ref[...], dims)

        def finish(y):
            if aux_ref is not None:
                aux_ref[...] = y.astype(aux_dtype)
            if g_ref is not None:
                y = y * g_ref[...]
            if r_ref is not None:
                y = y + r_ref[...]
            o_ref[...] = y.astype(out_dtype)

        if nk == 1:
            finish(prod)
            return
        acc = refs[pos]
        kk = pl.program_id(2)

        @pl.when(kk == 0)
        def _():
            acc[...] = prod

        @pl.when((kk > 0) & (kk < nk - 1))
        def _():
            acc[...] += prod

        @pl.when(kk == nk - 1)
        def _():
            finish(acc[...] + prod)

    ins, in_specs = [a, b], [a_spec, b_spec]
    if resid is not None:
        ins.append(resid)
        in_specs.append(pl.BlockSpec((tm, tn), lambda i, j, kk: (i, j)))
    if gate is not None:
        ins.append(gate)
        in_specs.append(pl.BlockSpec((1, tn), lambda i, j, kk: (0, j)))
    outs, out_specs = [jax.ShapeDtypeStruct(o_shape, out_dtype)], [o_spec]
    if aux_dtype is not None:
        outs.append(jax.ShapeDtypeStruct((m, n), aux_dtype))
        out_specs.append(pl.BlockSpec((tm, tn), lambda i, j, kk: (i, j)))
    sem = ("parallel", "parallel", "arbitrary") if side is None else ("arbitrary",) * 3
    res = pl.pallas_call(
        body, name=name, grid=(m // tm, n // tn, nk), in_specs=in_specs + side_in, out_specs=out_specs + side_out,
        out_shape=outs + side_shapes, scratch_shapes=([pltpu.VMEM((tm, tn), F32)] if nk > 1 else []) + side_sems,
        compiler_params=_params(*sem, vmem=MATMUL_VMEM),
    )(*ins, *([] if side is None else side.srcs))
    main = res[:n_out] if aux_dtype is not None else res[0]
    return main if side is None else (main, res[n_out:])


def mm_nn(a, b, *, out_dtype, name, tm_cap=1088, tn_cap=1024, tk_cap=MATMUL_TK, **kw):
    m, k = a.shape
    n = b.shape[1]
    tm, tn, tk = _tile(m, tm_cap, 16), _tile(n, tn_cap, LANES), _tile(k, tk_cap, LANES)
    return matmul(a, b, m=m, n=n, k=k, tm=tm, tn=tn, tk=tk, dims=_NN, out_dtype=out_dtype, name=name,
                  a_spec=pl.BlockSpec((tm, tk), lambda i, j, kk: (i, kk)),
                  b_spec=pl.BlockSpec((tk, tn), lambda i, j, kk: (kk, j)), **kw)


def mm_nt(a, b, *, out_dtype, name, tm_cap=1088, tn_cap=1024, tk_cap=MATMUL_TK // 2, **kw):
    m, k = a.shape
    n = b.shape[0]
    tm, tn, tk = _tile(m, tm_cap, 16), _tile(n, tn_cap, LANES), _tile(k, tk_cap, LANES)
    return matmul(a, b, m=m, n=n, k=k, tm=tm, tn=tn, tk=tk, dims=_NT, out_dtype=out_dtype, name=name,
                  a_spec=pl.BlockSpec((tm, tk), lambda i, j, kk: (i, kk)),
                  b_spec=pl.BlockSpec((tn, tk), lambda i, j, kk: (j, kk)), **kw)


def mm_tn(a, b, *, out_dtype, name, tm_cap=1024, tn_cap=1024, tk_cap=MATMUL_TK_TOKENS, **kw):
    k, m = a.shape
    n = b.shape[1]
    tm, tn, tk = _tile(m, tm_cap, LANES), _tile(n, tn_cap, LANES), _tile(k, tk_cap, 16)
    return matmul(a, b, m=m, n=n, k=k, tm=tm, tn=tn, tk=tk, dims=_TN, out_dtype=out_dtype, name=name,
                  a_spec=pl.BlockSpec((tk, tm), lambda i, j, kk: (kk, i)),
                  b_spec=pl.BlockSpec((tk, tn), lambda i, j, kk: (kk, j)), **kw)


def _rms(x):
    r = lax.rsqrt(jnp.mean(x * x, axis=-1, keepdims=True) + EPS)
    return x * r, r


def norm_mod_fwd(x, ctx, nw, scale2, shift2, *, name):
    s, d = x.shape
    cl = 0 if ctx is None else ctx.shape[0]
    tr = _tile(s if ctx is None else cl, 256, 16)
    n_lat = s // tr

    def body(*refs):
        if ctx is None:
            x_ref, nw_ref, sc_ref, sh_ref, o_ref = refs
            v = x_ref[...]
        else:
            x_ref, c_ref, nw_ref, sc_ref, sh_ref, o_ref = refs
            v = jnp.where(pl.program_id(0) < n_lat, x_ref[...], c_ref[...])
        y = _rms(v)[0] * nw_ref[...]
        o_ref[...] = (y * (1.0 + sc_ref[...]) + sh_ref[...]).astype(o_ref.dtype)

    sel = pl.BlockSpec((None, 1, d), lambda i: (i // n_lat, 0, 0))
    ins = [x] if ctx is None else [x, ctx]
    specs = [pl.BlockSpec((tr, d), lambda i: (jnp.minimum(i, n_lat - 1), 0))]
    if ctx is not None:
        specs.append(pl.BlockSpec((tr, d), lambda i: (jnp.maximum(i - n_lat, 0), 0)))
    return pl.pallas_call(
        body, name=name, grid=((s + cl) // tr,),
        in_specs=specs + [pl.BlockSpec((1, d), lambda i: (0, 0)), sel, sel],
        out_specs=pl.BlockSpec((tr, d), lambda i: (i, 0)),
        out_shape=jax.ShapeDtypeStruct((s + cl, d), ACT_DTYPE),
        compiler_params=_params("parallel"),
    )(*ins, nw, scale2, shift2)


def norm_mod_bwd(x, nw, scale, d_hn, *, row0, resid, init, name):
    r, d = x.shape
    tr = _tile(r, 256, 16)
    off = row0 // tr
    want_dx = resid is not None

    def body(*refs):
        x_ref, nw_ref, sc_ref, dh_ref = refs[:4]
        pos = 4
        res_ref = init_ref = dx_ref = None
        if want_dx:
            res_ref, pos = refs[pos], pos + 1
        if init is not None:
            init_ref, pos = refs[pos], pos + 1
        if want_dx:
            dx_ref, pos = refs[pos], pos + 1
        dnw_ref, dsc_ref, dsh_ref = refs[pos:pos + 3]
        i = pl.program_id(0)

        @pl.when(i == 0)
        def _():
            dnw_ref[...] = jnp.zeros_like(dnw_ref) if init_ref is None else init_ref[...]
            dsc_ref[...] = jnp.zeros_like(dsc_ref)
            dsh_ref[...] = jnp.zeros_like(dsh_ref)

        nrm, rs = _rms(x_ref[...])
        w = nw_ref[...]
        dh = dh_ref[...].astype(F32)
        dsh_ref[...] += jnp.sum(dh, axis=0, keepdims=True)
        dsc_ref[...] += jnp.sum(dh * (nrm * w), axis=0, keepdims=True)
        dy = dh * (1.0 + sc_ref[...])
        dnw_ref[...] += jnp.sum(dy * nrm, axis=0, keepdims=True)
        if want_dx:
            dn = dy * w
            dx = rs * (dn - nrm * jnp.mean(dn * nrm, axis=-1, keepdims=True))
            dx_ref[...] = dx + res_ref[...]

    row = pl.BlockSpec((tr, d), lambda i: (i, 0))
    vec = pl.BlockSpec((1, d), lambda i: (0, 0))
    ins, specs = [x, nw, scale, d_hn], [row, vec, vec, pl.BlockSpec((tr, d), lambda i: (i + off, 0))]
    if want_dx:
        ins.append(resid)
        specs.append(row)
    if init is not None:
        ins.append(init)
        specs.append(vec)
    vshape = jax.ShapeDtypeStruct((1, d), F32)
    outs, ospecs = [vshape] * 3, [vec] * 3
    if want_dx:
        outs, ospecs = [jax.ShapeDtypeStruct((r, d), F32)] + outs, [row] + ospecs
    res = pl.pallas_call(body, name=name, grid=(r // tr,), in_specs=specs, out_specs=ospecs, out_shape=outs,
                         compiler_params=_params("arbitrary"))(*ins)
    return tuple(res) if want_dx else (None,) + tuple(res)


def gate_bwd(dx, yo, gate, *, name):
    s, d = dx.shape
    tr = _tile(s, 256, 16)

    def body(dx_ref, yo_ref, g_ref, dyo_ref, dg_ref):
        @pl.when(pl.program_id(0) == 0)
        def _():
            dg_ref[...] = jnp.zeros_like(dg_ref)

        g = dx_ref[...]
        dg_ref[...] += jnp.sum(g * yo_ref[...].astype(F32), axis=0, keepdims=True)
        dyo_ref[...] = (g * g_ref[...]).astype(dyo_ref.dtype)

    row = pl.BlockSpec((tr, d), lambda i: (i, 0))
    vec = pl.BlockSpec((1, d), lambda i: (0, 0))
    return pl.pallas_call(
        body, name=name, grid=(s // tr,), in_specs=[row, row, vec], out_specs=[row, vec],
        out_shape=[jax.ShapeDtypeStruct((s, d), ACT_DTYPE), jax.ShapeDtypeStruct((1, d), F32)],
        compiler_params=_params("arbitrary"))(dx, yo, gate)


def loss_head(x, fw, target, *, name):
    s, d = x.shape
    tr = _tile(s, 256, 16)

    def body(x_ref, w_ref, t_ref, loss_ref, dx_ref, dw_ref):
        @pl.when(pl.program_id(0) == 0)
        def _():
            loss_ref[...] = jnp.zeros_like(loss_ref)
            dw_ref[...] = jnp.zeros_like(dw_ref)

        nrm, rs = _rms(x_ref[...])
        w = w_ref[...]
        err = nrm * w - t_ref[...]
        loss_ref[...] += 0.5 * jnp.sum(jnp.mean(err * err, axis=-1, keepdims=True))
        d_out = err * (1.0 / d)
        dw_ref[...] += jnp.sum(d_out * nrm, axis=0, keepdims=True)
        dn = d_out * w
        dx_ref[...] = rs * (dn - nrm * jnp.mean(dn * nrm, axis=-1, keepdims=True))

    row = pl.BlockSpec((tr, d), lambda i: (i, 0))
    vec = pl.BlockSpec((1, d), lambda i: (0, 0))
    return pl.pallas_call(
        body, name=name, grid=(s // tr,), in_specs=[row, vec, row],
        out_specs=[pl.BlockSpec((SUBLANES, LANES), lambda i: (0, 0)), row, vec],
        out_shape=[jax.ShapeDtypeStruct((SUBLANES, LANES), F32), jax.ShapeDtypeStruct((s, d), F32),
                   jax.ShapeDtypeStruct((1, d), F32)],
        compiler_params=_params("arbitrary"))(x, fw, target)


def _segments(rows, seg_a, seg_b):
    t = _iota((rows, 1), 0)
    if seg_b == 0:
        return t % seg_a, seg_a
    return jnp.where(t < seg_a, t, t - seg_a), jnp.where(t < seg_a, seg_a, seg_b)


def _shift(x, o, seg):
    if o == 0:
        return x
    pos, length = _segments(x.shape[0], *seg)
    y = pltpu.roll(x, (-o) % x.shape[0], 0)
    return jnp.where((pos + o >= 0) & (pos + o < length), y, 0.0)


def _conv(x, w, offsets, seg):
    acc = None
    for j, o in enumerate(offsets):
        term = w[j:j + 1, :] * _shift(x, o, seg)
        acc = term if acc is None else acc + term
    return acc


def _conv_bwd(x, w, dy, offsets, seg):
    dx = None
    dw = jnp.zeros(w.shape, F32)
    row = _iota(w.shape, 0)
    for j, o in enumerate(offsets):
        term = w[j:j + 1, :] * _shift(dy, -o, seg)
        dx = term if dx is None else dx + term
        dwj = jnp.sum(dy * _shift(x, o, seg), axis=0, keepdims=True)
        dw = dw + jnp.where(row == j, dwj, 0.0)
    return dx, dw


def _qkv_post(y, group, scale):
    a = _silu(y)
    n = a * lax.rsqrt(jnp.sum(a * a, axis=-1, keepdims=True) + EPS)
    return jnp.where(group == 0, n * scale, jnp.where(group == 1, n, a))


def qkv_conv_fwd(proj, conv_w, *, s, cl, heads, name):
    m = s + cl
    dh = LANES
    scale = dh ** -0.5

    def body(x_ref, w_ref, o_ref):
        group = pl.program_id(0) // heads
        y = _conv(x_ref[...], w_ref[...], DN_CONV_OFFSETS, (s, cl))
        o_ref[...] = _qkv_post(y, group, scale).astype(o_ref.dtype)

    return pl.pallas_call(
        body, name=name, grid=(3 * heads,),
        in_specs=[pl.BlockSpec((m, dh), lambda j: (0, j)), pl.BlockSpec((len(DN_CONV_OFFSETS), dh), lambda j: (0, j))],
        out_specs=pl.BlockSpec((m, dh), lambda j: (0, j)),
        out_shape=jax.ShapeDtypeStruct((m, 3 * heads * dh), ACT_DTYPE),
        compiler_params=_params("parallel"))(proj, conv_w)


def qkv_conv_bwd(proj, conv_w, dqkv, *, s, cl, heads, name):
    m = s + cl
    dh = LANES
    scale = dh ** -0.5
    kk = len(DN_CONV_OFFSETS)

    def body(x_ref, w_ref, d_ref, dx_ref, dw_ref):
        group = pl.program_id(0) // heads
        x, w = x_ref[...], w_ref[...]
        y = _conv(x, w, DN_CONV_OFFSETS, (s, cl))
        a = _silu(y)
        rn = lax.rsqrt(jnp.sum(a * a, axis=-1, keepdims=True) + EPS)
        n = a * rn
        dout = d_ref[...] * jnp.where(group == 0, scale, 1.0)
        da_norm = rn * (dout - n * jnp.sum(dout * n, axis=-1, keepdims=True))
        dy = jnp.where(group == 2, dout, da_norm) * _dsilu(y)
        dx, dw = _conv_bwd(x, w, dy, DN_CONV_OFFSETS, (s, cl))
        dx_ref[...] = dx.astype(dx_ref.dtype)
        dw_ref[...] = dw

    col = pl.BlockSpec((m, dh), lambda j: (0, j))
    wspec = pl.BlockSpec((kk, dh), lambda j: (0, j))
    return pl.pallas_call(
        body, name=name, grid=(3 * heads,),
        in_specs=[col, wspec, pl.BlockSpec((None, m, dh), lambda j: (j // heads, 0, j % heads))],
        out_specs=[col, wspec],
        out_shape=[jax.ShapeDtypeStruct((m, 3 * heads * dh), ACT_DTYPE),
                   jax.ShapeDtypeStruct((kk, 3 * heads * dh), F32)],
        compiler_params=_params("parallel"))(proj, conv_w, dqkv)


def _scan_masks(d):
    t, s = _iota((CHUNK, CHUNK), 0), _iota((CHUNK, CHUNK), 1)
    return ((s <= t), (s < t)) if d == 0 else ((s >= t), (s > t))


def _bmm(spec, a, b, precision=None):
    if precision is None:
        a, b = a.astype(MXU_DTYPE), b.astype(MXU_DTYPE)
    return jnp.einsum(spec, a, b, precision=precision, preferred_element_type=F32)


def _unit_tri_inverse(a):
    mm = functools.partial(_bmm, 'nts,nsr->ntr', precision=INV_PRECISION)
    row, col = _iota((CHUNK, CHUNK), 0), _iota((CHUNK, CHUNK), 1)
    eye = (row == col).astype(F32)
    dg = jnp.where(row // INV_BLOCK == col // INV_BLOCK, a, 0.0)
    off = a - dg
    p = eye - dg
    pw = dg
    for _ in range(3):
        pw = mm(pw, pw)
        p = p + mm(p, pw)
    n = mm(p, off)
    r = eye - n
    return mm(r + mm(r, mm(n, n)), p)


def _dn_intra(q, k, v, beta_b, gc_b, d):
    dh = q.shape[-1]
    incl, strict = _scan_masks(d)
    gc64 = gc_b[:, :, :CHUNK]
    diff = gc64 - jnp.swapaxes(gc64, 1, 2)
    decay = jnp.where(incl, jnp.exp(jnp.where(incl, diff, 0.0)), 0.0)
    qk_kk = _bmm('ntd,nsd->nts', jnp.concatenate([q, k], axis=1), k)
    qk, kk = qk_kk[:, :CHUNK], qk_kk[:, CHUNK:]
    a = jnp.where(strict, beta_b[:, :, :CHUNK] * kk * decay, 0.0)
    tinv = _unit_tri_inverse(a)
    rhs = jnp.concatenate([beta_b * jnp.exp(gc_b) * k, beta_b * v], axis=-1)
    wu = _bmm('nts,nsd->ntd', tinv, rhs, INV_PRECISION)
    w, u = wu[:, :, :dh], wu[:, :, dh:]
    last = CHUNK - 1 if d == 0 else 0
    gl = gc_b[:, last:last + 1, :]
    ke = k * jnp.exp(gl - gc_b)
    ge = jnp.exp(gl)
    return w, u, ke, ge, qk * decay, q * jnp.exp(gc_b)


def _dn_step(s, w, u, ke, ge, aqk, qg):
    ws_qs = _dot(jnp.concatenate([w, qg], axis=0), s, _NN)
    u2 = u - ws_qs[:CHUNK]
    s_new = ge * s + _dot(ke, u2, _TN)
    o = ws_qs[CHUNK:] + _dot(aqk, u2, _NN)
    return s_new, o


def _chunk_cumsum(x, d):
    rows = x.shape[0]
    pos = _iota((rows, 1), 0) % CHUNK
    step = 1
    while step < CHUNK:
        if d == 0:
            x = x + jnp.where(pos >= step, pltpu.roll(x, step, 0), 0.0)
        else:
            x = x + jnp.where(pos < CHUNK - step, pltpu.roll(x, rows - step, 0), 0.0)
        step *= 2
    return x


def _pick_lane(x, j):
    return jnp.sum(jnp.where(_iota(x.shape, 1) == j, x, 0.0), axis=1, keepdims=True)


def _dn_gates(ba, a_log, dt_bias, d, h, heads):
    braw = _pick_lane(ba, d * heads + h)
    araw = _pick_lane(ba, (2 + d) * heads + h)
    a_neg = -jnp.exp(_pick_lane(a_log[d:d + 1, :], h))
    pre = araw + _pick_lane(dt_bias[d:d + 1, :], h)
    return jax.nn.sigmoid(braw), a_neg * jax.nn.softplus(pre), pre, a_neg


_DN_SUB_FWD = 16
_DN_SUB_BWD = 16


def _for_sub_batches(s, cl, fn, sub=_DN_SUB_FWD):
    for base, total in ((0, s), (s, cl)):
        nch = min(sub, total // CHUNK)
        rows_per = nch * CHUNK
        count = total // rows_per

        def run(i, carry, base=base, nch=nch, rows_per=rows_per):
            row0 = pl.multiple_of(base + i * rows_per, rows_per)
            ge0 = pl.multiple_of((base // CHUNK + i * nch) * SUBLANES, nch * SUBLANES)
            fn(pl.ds(row0, rows_per), pl.ds(ge0, nch * SUBLANES), nch)
            return carry

        if count == 1:
            fn(pl.ds(base, rows_per), pl.ds(base // CHUNK * SUBLANES, nch * SUBLANES), nch)
        else:
            lax.fori_loop(0, count, run, 0)


def _dn_chunk_order(t, d, n_lat, n_ctx):
    if d == 0:
        return jnp.where(t < n_ctx, n_lat + t, t - n_ctx)
    return n_lat + n_ctx - 1 - t


def _dn_fill_intra(q_ref, k_ref, v_ref, bb_s, gc_s, w_s, u_s, ke_s, ge_s, aqk_s, qg_s, d, s, cl):
    dh = LANES

    def fill(rows, ge_rows, nch):
        def load(ref):
            return ref[rows, :].astype(F32).reshape(nch, CHUNK, dh)

        w, u, ke, ge, aqk, qg = _dn_intra(load(q_ref), load(k_ref), load(v_ref), load(bb_s), load(gc_s), d)
        w_s[rows, :] = w.reshape(nch * CHUNK, dh)
        u_s[rows, :] = u.reshape(nch * CHUNK, dh)
        ke_s[rows, :] = ke.reshape(nch * CHUNK, dh)
        qg_s[rows, :] = qg.reshape(nch * CHUNK, dh)
        aqk_s[rows, :] = aqk.reshape(nch * CHUNK, CHUNK)
        ge_s[ge_rows, :] = jnp.broadcast_to(ge, (nch, SUBLANES, dh)).reshape(nch * SUBLANES, dh)

    _for_sub_batches(s, cl, fill)


def _dn_chunk_refs(cid, w_s, u_s, ke_s, ge_s, aqk_s, qg_s):
    rows = pl.ds(pl.multiple_of(cid * CHUNK, CHUNK), CHUNK)
    ge = ge_s[pl.ds(pl.multiple_of(cid * SUBLANES, SUBLANES), SUBLANES), :][0:1]
    return rows, (w_s[rows, :], u_s[rows, :], ke_s[rows, :], ge, aqk_s[rows, :], qg_s[rows, :])


def _dn_scratch(m):
    dh = LANES
    big = pltpu.VMEM((m, dh), F32)
    return [big, big, big, big, big, pltpu.VMEM((m // CHUNK * SUBLANES, dh), F32), pltpu.VMEM((m, CHUNK), F32), big]


def _once(shape, index_map):
    return pl.BlockSpec(shape, index_map, pipeline_mode=pl.Buffered(1))


def dn_fwd(qkv, ba, a_log, dt_bias, side, *, s, cl, heads, name):
    m = s + cl
    dh = LANES
    n_lat, n_ctx = s // CHUNK, cl // CHUNK
    side_in, side_shapes, side_out, side_sems = side.specs()

    def body(*refs):
        q_ref, k_ref, v_ref, ba_ref, al_ref, dt_ref = refs[:6]
        pos = 6 + side.n
        o_ref, st_ref = refs[pos], refs[pos + 1]
        scratch = refs[pos + 2 + side.n:]
        side.run(refs[6:pos], refs[pos + 2:pos + 2 + side.n], scratch[8:], pl.program_id(0), heads - 1,
                 lambda: compute(q_ref, k_ref, v_ref, ba_ref, al_ref, dt_ref, o_ref, st_ref, *scratch[:8]))

    def compute(q_ref, k_ref, v_ref, ba_ref, al_ref, dt_ref, o_ref, st_ref,
                bb_s, gc_s, w_s, u_s, ke_s, ge_s, aqk_s, qg_s):
        h = pl.program_id(0)
        for d in (0, 1):
            beta, g, _, _ = _dn_gates(ba_ref[...], al_ref[...], dt_ref[...], d, h, heads)
            bb_s[...] = jnp.broadcast_to(beta, (m, dh))
            gc_s[...] = _chunk_cumsum(jnp.broadcast_to(g, (m, dh)), d)
            _dn_fill_intra(q_ref, k_ref, v_ref, bb_s, gc_s, w_s, u_s, ke_s, ge_s, aqk_s, qg_s, d, s, cl)

            def step(t, state):
                cid = _dn_chunk_order(t, d, n_lat, n_ctx)
                rows, terms = _dn_chunk_refs(cid, w_s, u_s, ke_s, ge_s, aqk_s, qg_s)
                st_ref[d, pl.ds(pl.multiple_of(cid * dh, dh), dh), :] = state.astype(st_ref.dtype)
                state, o = _dn_step(state, *terms)

                @pl.when(cid < n_lat)
                def _():
                    if d == 0:
                        o_ref[rows, :] = o
                    else:
                        o_ref[rows, :] += o

                return state

            lax.fori_loop(0, n_lat + n_ctx, step, jnp.zeros((dh, dh), F32))

    def col(j0):
        return _once((m, dh), lambda h: (0, j0 + h))

    small = pl.BlockSpec((SUBLANES, LANES), lambda h: (0, 0))
    st_rows = (n_lat + n_ctx) * dh
    res = pl.pallas_call(
        body, name=name, grid=(heads,),
        in_specs=[col(0), col(heads), col(2 * heads), _once((m, LANES), lambda h: (0, 0)), small, small] + side_in,
        out_specs=[pl.BlockSpec((s, dh), lambda h: (0, h)), _once((None, 2, st_rows, dh), lambda h: (h, 0, 0, 0))]
        + side_out,
        out_shape=[jax.ShapeDtypeStruct((s, heads * dh), F32),
                   jax.ShapeDtypeStruct((heads, 2, st_rows, dh), ACT_DTYPE)] + side_shapes,
        scratch_shapes=_dn_scratch(m) + side_sems,
        compiler_params=_params("arbitrary", vmem=BIG_KERNEL_VMEM))(qkv, qkv, qkv, ba, a_log, dt_bias, *side.srcs)
    return res[0], res[1], res[2:]


def dn_out_fwd(o, proj, dn_norm, *, heads, gate_col0, name):
    s = o.shape[0]
    dh = LANES
    tr = _tile(s, 1024, 16)

    def body(o_ref, g_ref, nw_ref, y_ref):
        y_ref[...] = (_rms(o_ref[...])[0] * nw_ref[...] * _silu(g_ref[...])).astype(y_ref.dtype)

    blk = pl.BlockSpec((tr, dh), lambda i, h: (i, h))
    return pl.pallas_call(
        body, name=name, grid=(s // tr, heads),
        in_specs=[blk, pl.BlockSpec((tr, dh), lambda i, h: (i, gate_col0 + h)), pl.BlockSpec((1, dh), lambda i, h: (0, 0))],
        out_specs=blk, out_shape=jax.ShapeDtypeStruct((s, heads * dh), ACT_DTYPE),
        compiler_params=_params("parallel", "parallel"))(o, proj, dn_norm)


def dn_out_bwd(o, proj, d_y, dn_norm, *, heads, gate_col0, name):
    s = o.shape[0]
    dh = LANES
    tr = _tile(s, 1024, 16)

    def body(o_ref, g_ref, dy_ref, nw_ref, do_ref, dg_ref, dnw_ref):
        @pl.when((pl.program_id(0) == 0) & (pl.program_id(1) == 0))
        def _():
            dnw_ref[...] = jnp.zeros_like(dnw_ref)

        nrm, rs = _rms(o_ref[...])
        nw, gate, dy = nw_ref[...], g_ref[...], dy_ref[...]
        dg_ref[...] = (dy * (nrm * nw) * _dsilu(gate)).astype(dg_ref.dtype)
        dy0 = dy * _silu(gate)
        dnw_ref[0:1, :] += jnp.sum(dy0 * nrm, axis=0, keepdims=True)
        dn = dy0 * nw
        do_ref[...] = rs * (dn - nrm * jnp.mean(dn * nrm, axis=-1, keepdims=True))

    blk = pl.BlockSpec((tr, dh), lambda i, h: (i, h))
    return pl.pallas_call(
        body, name=name, grid=(s // tr, heads),
        in_specs=[blk, pl.BlockSpec((tr, dh), lambda i, h: (i, gate_col0 + h)), blk,
                  pl.BlockSpec((1, dh), lambda i, h: (0, 0))],
        out_specs=[blk, blk, pl.BlockSpec((SUBLANES, LANES), lambda i, h: (0, 0))],
        out_shape=[jax.ShapeDtypeStruct((s, heads * dh), F32), jax.ShapeDtypeStruct((s, heads * dh), ACT_DTYPE),
                   jax.ShapeDtypeStruct((SUBLANES, LANES), F32)],
        compiler_params=_params("arbitrary", "arbitrary"))(o, proj, d_y, dn_norm)


def dn_bwd(qkv, ba, d_o, states, a_log, dt_bias, side, *, s, cl, heads, name):
    m = s + cl
    dh = LANES
    n_lat, n_ctx = s // CHUNK, cl // CHUNK
    n_all = n_lat + n_ctx
    side_in, side_shapes, side_out, side_sems = side.specs()

    def body(*refs):
        ins, pos = refs[:8], 8 + side.n
        outs = refs[pos:pos + 4]
        scratch = refs[pos + 4 + side.n:]
        side.run(refs[8:pos], refs[pos + 4:pos + 4 + side.n], scratch[8:], pl.program_id(0), heads - 1,
                 lambda: compute(*ins, *outs, *scratch[:8]))

    def compute(q_ref, k_ref, v_ref, ba_ref, do_ref, st_ref, al_ref, dt_ref, dqkv_ref, dba_ref, dal_ref, ddt_ref,
                bb_s, gc_s, w_s, u_s, ke_s, ge_s, aqk_s, qg_s):
        h = pl.program_id(0)
        dq_ref, dk_ref, dv_ref = dqkv_ref.at[0], dqkv_ref.at[1], dqkv_ref.at[2]

        @pl.when(h == 0)
        def _():
            dba_ref[...] = jnp.zeros_like(dba_ref)

        lane = _iota((m, LANES), 1)
        for d in (0, 1):
            beta, g, pre, a_neg = _dn_gates(ba_ref[...], al_ref[...], dt_ref[...], d, h, heads)
            bb_s[...] = jnp.broadcast_to(beta, (m, dh))
            gc_s[...] = _chunk_cumsum(jnp.broadcast_to(g, (m, dh)), d)
            _dn_fill_intra(q_ref, k_ref, v_ref, bb_s, gc_s, w_s, u_s, ke_s, ge_s, aqk_s, qg_s, d, s, cl)

            def bwd_step(i, dstate):
                cid = _dn_chunk_order(n_all - 1 - i, d, n_lat, n_ctx)
                rows, terms = _dn_chunk_refs(cid, w_s, u_s, ke_s, ge_s, aqk_s, qg_s)
                state = st_ref[d, pl.ds(pl.multiple_of(cid * dh, dh), dh), :].astype(F32)
                _, vjp = jax.vjp(_dn_step, state, *terms)
                lat_rows = pl.ds(pl.multiple_of(jnp.minimum(cid, n_lat - 1) * CHUNK, CHUNK), CHUNK)
                do = jnp.where(cid < n_lat, do_ref[lat_rows, :], 0.0)
                dstate, dw, du, dke, dge, daqk, dqg = vjp((dstate, do))
                w_s[rows, :] = dw
                u_s[rows, :] = du
                ke_s[rows, :] = dke
                qg_s[rows, :] = dqg
                aqk_s[rows, :] = daqk
                ge_s[pl.ds(pl.multiple_of(cid * SUBLANES, SUBLANES), SUBLANES), :] = jnp.broadcast_to(
                    dge, (SUBLANES, dh))
                return dstate

            lax.fori_loop(0, n_all, bwd_step, jnp.zeros((dh, dh), F32))

            def intra_bwd(rows, ge_rows, nch):
                def load(ref, width=dh):
                    return ref[rows, :].astype(F32).reshape(nch, CHUNK, width)

                _, vjp = jax.vjp(functools.partial(_dn_intra, d=d), load(q_ref), load(k_ref), load(v_ref),
                                 load(bb_s), load(gc_s))
                dge = ge_s[ge_rows, :].reshape(nch, SUBLANES, dh)[:, 0:1]
                dq, dk, dv, dbb, dgc = vjp((load(w_s), load(u_s), load(ke_s), dge, load(aqk_s, CHUNK), load(qg_s)))
                flat = lambda x: x.reshape(nch * CHUNK, dh)
                if d == 0:
                    dq_ref[rows, :], dk_ref[rows, :], dv_ref[rows, :] = flat(dq), flat(dk), flat(dv)
                else:
                    dq_ref[rows, :] += flat(dq)
                    dk_ref[rows, :] += flat(dk)
                    dv_ref[rows, :] += flat(dv)
                bb_s[rows, :] = flat(dbb)
                gc_s[rows, :] = flat(dgc)

            _for_sub_batches(s, cl, intra_bwd, _DN_SUB_BWD)

            dbeta = jnp.sum(bb_s[...], axis=1, keepdims=True)
            dg = jnp.sum(_chunk_cumsum(gc_s[...], 1 - d), axis=1, keepdims=True)
            dbraw = dbeta * beta * (1.0 - beta)
            dpre = dg * a_neg * jax.nn.sigmoid(pre)
            dba_ref[...] += (jnp.where(lane == d * heads + h, dbraw, 0.0)
                             + jnp.where(lane == (2 + d) * heads + h, dpre, 0.0))
            dal_ref[d:d + 1, :] = jnp.broadcast_to(jnp.sum(dg * g, axis=0, keepdims=True), (1, LANES))
            ddt_ref[d:d + 1, :] = jnp.broadcast_to(jnp.sum(dpre, axis=0, keepdims=True), (1, LANES))
        dal_ref[2:SUBLANES, :] = jnp.zeros((SUBLANES - 2, LANES), F32)
        ddt_ref[2:SUBLANES, :] = jnp.zeros((SUBLANES - 2, LANES), F32)

    def col(j0, rows=m):
        return _once((rows, dh), lambda h: (0, j0 + h))

    small = pl.BlockSpec((SUBLANES, LANES), lambda h: (0, 0))
    tile_h = pl.BlockSpec((None, SUBLANES, LANES), lambda h: (h, 0, 0))
    tiles = jax.ShapeDtypeStruct((heads, SUBLANES, LANES), F32)
    res = pl.pallas_call(
        body, name=name, grid=(heads,),
        in_specs=[col(0), col(heads), col(2 * heads), _once((m, LANES), lambda h: (0, 0)), col(0, s),
                  _once((None, 2, n_all * dh, dh), lambda h: (h, 0, 0, 0)), small, small] + side_in,
        out_specs=[_once((3, m, dh), lambda h: (0, 0, h)), _once((m, LANES), lambda h: (0, 0)), tile_h, tile_h]
        + side_out,
        out_shape=[jax.ShapeDtypeStruct((3, m, heads * dh), F32), jax.ShapeDtypeStruct((m, LANES), F32), tiles, tiles]
        + side_shapes,
        scratch_shapes=_dn_scratch(m) + side_sems,
        compiler_params=_params("arbitrary", vmem=BIG_KERNEL_VMEM))(qkv, qkv, qkv, ba, d_o, states, a_log, dt_bias,
                                                                     *side.srcs)
    return res[0], res[1], res[2], res[3], res[4:]


def _lin_scan(a, b, d):
    rows = a.shape[0]
    t = _iota((rows, 1), 0)
    step = 1
    while step < rows:
        if d == 0:
            ok, sa, sb = t >= step, pltpu.roll(a, step, 0), pltpu.roll(b, step, 0)
        else:
            ok, sa, sb = t < rows - step, pltpu.roll(a, rows - step, 0), pltpu.roll(b, rows - step, 0)
        b = b + a * jnp.where(ok, sb, 0.0)
        a = a * jnp.where(ok, sa, 1.0)
        step *= 2
    return b


def _lru_gates(xc, w_r, b_r, w_i, b_i, lam):
    r = jax.nn.sigmoid(_dot(xc, w_r, _NN) + b_r)
    i = jax.nn.sigmoid(_dot(xc, w_i, _NN) + b_i)
    log_a = -LRU_C * r * jax.nn.softplus(-lam)
    z = 2.0 * log_a
    series = -(z * (1.0 + z * (0.5 + z * (1.0 / 6.0))))
    one_minus = jnp.where(z > -0.01, series, 1.0 - jnp.exp(z))
    return jnp.exp(log_a), jnp.sqrt(one_minus) * (i * xc)


def _lru_states(a, b, d, s, cl):
    ac, bc = a[s:], b[s:]
    hc = _lin_scan(ac, bc, d)
    h0 = hc[cl - 1:cl] if d == 0 else hc[0:1]
    first = 0 if d == 0 else s - 1
    al = a[:s]
    bl = b[:s] + jnp.where(_iota((s, 1), 0) == first, al * h0, 0.0)
    return _lin_scan(al, bl, d), hc, h0


def _lru_specs(m, nb_dim):
    j_col = lambda rows: pl.BlockSpec((rows, LANES), lambda j: (0, j))
    w_blk = pl.BlockSpec((2, None, nb_dim, nb_dim), lambda j: (0, j, 0, 0))
    return j_col, w_blk


def lru_fwd(xcm, conv_w, conv_b, w_r, b_r, w_i, b_i, lam, *, s, cl, name):
    m, width = xcm.shape
    j_col, w_blk = _lru_specs(m, w_r.shape[-1])

    def body(x_ref, cw_ref, cb_ref, wr_ref, br_ref, wi_ref, bi_ref, lam_ref, h_ref):
        xc = _conv(x_ref[...], cw_ref[...], DN_CONV_OFFSETS, (s, cl)) + cb_ref[...]
        for d in (0, 1):
            a, b = _lru_gates(xc, wr_ref[d], br_ref[d:d + 1, :], wi_ref[d], bi_ref[d:d + 1, :], lam_ref[d:d + 1, :])
            h = _lru_states(a, b, d, s, cl)[0]
            if d == 0:
                h_ref[...] = h
            else:
                h_ref[...] += h

    return pl.pallas_call(
        body, name=name, grid=(width // LANES,),
        in_specs=[_once((m, LANES), lambda j: (0, j)), j_col(len(DN_CONV_OFFSETS)), j_col(1), w_blk, j_col(2), w_blk,
                  j_col(2), j_col(2)],
        out_specs=j_col(s), out_shape=jax.ShapeDtypeStruct((s, width), F32),
        compiler_params=_params("parallel", vmem=BIG_KERNEL_VMEM))(xcm, conv_w, conv_b, w_r, b_r, w_i, b_i, lam)


def lru_bwd(xcm, d_h, conv_w, conv_b, w_r, b_r, w_i, b_i, lam, *, s, cl, name):
    m, width = xcm.shape
    nb_dim = w_r.shape[-1]
    j_col, w_blk = _lru_specs(m, nb_dim)

    def body(x_ref, dh_ref, cw_ref, cb_ref, wr_ref, br_ref, wi_ref, bi_ref, lam_ref,
             dx_ref, dcw_ref, dcb_ref, dwr_ref, dbr_ref, dwi_ref, dbi_ref, dlam_ref):
        x, cw = x_ref[...], cw_ref[...]
        xc = _conv(x, cw, DN_CONV_OFFSETS, (s, cl)) + cb_ref[...]
        d_hl = dh_ref[...]
        d_xc = None
        for d in (0, 1):
            (a, b), vjp = jax.vjp(_lru_gates, xc, wr_ref[d], br_ref[d:d + 1, :], wi_ref[d], bi_ref[d:d + 1, :],
                                  lam_ref[d:d + 1, :])
            h, hc, h0 = _lru_states(a, b, d, s, cl)
            nxt = 1 if d == 0 else -1
            first = 0 if d == 0 else s - 1
            al, ac = a[:s], a[s:]
            lam_l = _lin_scan(_shift(al, nxt, (s, 0)), d_hl, 1 - d)
            h_prev = _shift(h, -nxt, (s, 0)) + jnp.where(_iota((s, 1), 0) == first, h0, 0.0)
            d_h0 = (al * lam_l)[first:first + 1]
            last_c = cl - 1 if d == 0 else 0
            d_hc = jnp.where(_iota((cl, 1), 0) == last_c, d_h0, 0.0)
            lam_c = _lin_scan(_shift(ac, nxt, (cl, 0)), d_hc, 1 - d)
            da = jnp.concatenate([lam_l * h_prev, lam_c * _shift(hc, -nxt, (cl, 0))], axis=0)
            db = jnp.concatenate([lam_l, lam_c], axis=0)
            g_xc, g_wr, g_br, g_wi, g_bi, g_lam = vjp((da, db))
            d_xc = g_xc if d_xc is None else d_xc + g_xc
            dwr_ref[d], dwi_ref[d] = g_wr, g_wi
            dbr_ref[d:d + 1, :], dbi_ref[d:d + 1, :], dlam_ref[d:d + 1, :] = g_br, g_bi, g_lam
        dx, dcw = _conv_bwd(x, cw, d_xc, DN_CONV_OFFSETS, (s, cl))
        dx_ref[...] = dx.astype(dx_ref.dtype)
        dcw_ref[...] = dcw
        dcb_ref[...] = jnp.sum(d_xc, axis=0, keepdims=True)

    kk = len(DN_CONV_OFFSETS)
    vec2 = jax.ShapeDtypeStruct((2, width), F32)
    return pl.pallas_call(
        body, name=name, grid=(width // LANES,),
        in_specs=[_once((m, LANES), lambda j: (0, j)), _once((s, LANES), lambda j: (0, j)), j_col(kk), j_col(1),
                  w_blk, j_col(2), w_blk, j_col(2), j_col(2)],
        out_specs=[j_col(m), j_col(kk), j_col(1), w_blk, j_col(2), w_blk, j_col(2), j_col(2)],
        out_shape=[jax.ShapeDtypeStruct((m, width), ACT_DTYPE), jax.ShapeDtypeStruct((kk, width), F32),
                   jax.ShapeDtypeStruct((1, width), F32), jax.ShapeDtypeStruct(w_r.shape, F32), vec2,
                   jax.ShapeDtypeStruct(w_i.shape, F32), vec2, vec2],
        compiler_params=_params("parallel", vmem=BIG_KERNEL_VMEM))(xcm, d_h, conv_w, conv_b, w_r, b_r, w_i, b_i, lam)


def lru_gate_fwd(h, proj, *, gate_col0, name):
    s, width = h.shape
    tr, tc = _tile(s, 512, 16), _tile(width, 512, LANES)
    c0 = gate_col0 * LANES // tc

    def body(h_ref, g_ref, y_ref):
        y_ref[...] = (h_ref[...] * _silu(g_ref[...])).astype(y_ref.dtype)

    blk = pl.BlockSpec((tr, tc), lambda i, j: (i, j))
    return pl.pallas_call(
        body, name=name, grid=(s // tr, width // tc),
        in_specs=[blk, pl.BlockSpec((tr, tc), lambda i, j: (i, c0 + j))], out_specs=blk,
        out_shape=jax.ShapeDtypeStruct((s, width), ACT_DTYPE),
        compiler_params=_params("parallel", "parallel"))(h, proj)


def lru_gate_bwd(h, proj, d_y, *, gate_col0, dy_col0, name):
    s, width = h.shape
    tr, tc = _tile(s, 512, 16), _tile(width, 512, LANES)
    c0, y0 = gate_col0 * LANES // tc, dy_col0 * LANES // tc

    def body(h_ref, g_ref, dy_ref, dh_ref, dg_ref):
        g, dy = g_ref[...], dy_ref[...]
        dh_ref[...] = dy * _silu(g)
        dg_ref[...] = (dy * h_ref[...] * _dsilu(g)).astype(dg_ref.dtype)

    blk = pl.BlockSpec((tr, tc), lambda i, j: (i, j))
    return pl.pallas_call(
        body, name=name, grid=(s // tr, width // tc),
        in_specs=[blk, pl.BlockSpec((tr, tc), lambda i, j: (i, c0 + j)), pl.BlockSpec((tr, tc), lambda i, j: (i, y0 + j))],
        out_specs=[blk, blk],
        out_shape=[jax.ShapeDtypeStruct((s, width), F32), jax.ShapeDtypeStruct((s, width), ACT_DTYPE)],
        compiler_params=_params("parallel", "parallel"))(h, proj, d_y)


def _sc_parts(p, width):
    return [p[:, k * width:(k + 1) * width] for k in range(4)]


def sc_mix_fwd(p, conv_w, *, name):
    s, width = p.shape[0], conv_w.shape[1]
    tr = 2 * GRID_W

    def body(p_ref, w_ref, y_ref):
        b_g, c_g, x_in, gate = _sc_parts(p_ref[...], width)
        z = _conv(c_g * x_in, w_ref[...], SC_CONV_OFFSETS, (GRID_W, 0))
        y_ref[...] = (b_g * z * _silu(gate)).astype(y_ref.dtype)

    return pl.pallas_call(
        body, name=name, grid=(s // tr,),
        in_specs=[pl.BlockSpec((tr, 4 * width), lambda i: (i, 0)), pl.BlockSpec(conv_w.shape, lambda i: (0, 0))],
        out_specs=pl.BlockSpec((tr, width), lambda i: (i, 0)),
        out_shape=jax.ShapeDtypeStruct((s, width), ACT_DTYPE),
        compiler_params=_params("parallel"))(p, conv_w)


def sc_mix_bwd(p, d_y, conv_w, *, name):
    s, width = p.shape[0], conv_w.shape[1]
    tr = 2 * GRID_W

    def body(p_ref, dy_ref, w_ref, dp_ref, dw_ref):
        @pl.when(pl.program_id(0) == 0)
        def _():
            dw_ref[...] = jnp.zeros_like(dw_ref)

        b_g, c_g, x_in, gate = _sc_parts(p_ref[...], width)
        w, dy = w_ref[...], dy_ref[...]
        u = c_g * x_in
        z = _conv(u, w, SC_CONV_OFFSETS, (GRID_W, 0))
        sg = _silu(gate)
        du, dw = _conv_bwd(u, w, dy * b_g * sg, SC_CONV_OFFSETS, (GRID_W, 0))
        dw_ref[...] += dw
        parts = (dy * z * sg, du * x_in, du * c_g, dy * b_g * z * _dsilu(gate))
        for k, part in enumerate(parts):
            dp_ref[:, k * width:(k + 1) * width] = part.astype(dp_ref.dtype)

    return pl.pallas_call(
        body, name=name, grid=(s // tr,),
        in_specs=[pl.BlockSpec((tr, 4 * width), lambda i: (i, 0)), pl.BlockSpec((tr, width), lambda i: (i, 0)),
                  pl.BlockSpec(conv_w.shape, lambda i: (0, 0))],
        out_specs=[pl.BlockSpec((tr, 4 * width), lambda i: (i, 0)), pl.BlockSpec(conv_w.shape, lambda i: (0, 0))],
        out_shape=[jax.ShapeDtypeStruct((s, 4 * width), ACT_DTYPE), jax.ShapeDtypeStruct(conv_w.shape, F32)],
        compiler_params=_params("arbitrary"))(p, d_y, conv_w)


MOD_ROWS = 16


def mod_fwd(cond, mod_w, mod_b, *, name):
    nl, d, ns = mod_w.shape
    tn = _tile(ns, 512, LANES)

    def body(c_ref, w_ref, b_ref, o_ref):
        o_ref[...] = _dot(_silu(c_ref[...]), w_ref[...], _NN) + b_ref[...]

    return pl.pallas_call(
        body, name=name, grid=(nl, ns // tn),
        in_specs=[pl.BlockSpec((MOD_ROWS, d), lambda l, j: (0, 0)), pl.BlockSpec((None, d, tn), lambda l, j: (l, 0, j)),
                  pl.BlockSpec((None, 1, tn), lambda l, j: (l, 0, j))],
        out_specs=pl.BlockSpec((None, MOD_ROWS, tn), lambda l, j: (l, 0, j)),
        out_shape=jax.ShapeDtypeStruct((nl, MOD_ROWS, ns), F32),
        compiler_params=_params("parallel", "parallel"))(cond, mod_w, mod_b)


def _adamw(w, g, m, v):
    m = ADAM_B1 * m + (1.0 - ADAM_B1) * g
    v = ADAM_B2 * v + (1.0 - ADAM_B2) * (g * g)
    m_hat = m / (1.0 - ADAM_B1 ** ADAM_STEP)
    v_hat = v / (1.0 - ADAM_B2 ** ADAM_STEP)
    return -ADAM_LR * (m_hat / (jnp.sqrt(v_hat) + ADAM_EPS) + ADAM_WD * w), m, v


def mod_adam(cond, d_mod, w, m, v, *, name):
    nl, d, ns = w.shape
    tr, tn = _tile(d, 256, SUBLANES), _tile(ns, 1024, LANES)

    def body(c_ref, dm_ref, w_ref, m_ref, v_ref, g_ref, dl_ref, nm_ref, nv_ref, ds_ref):
        @pl.when(pl.program_id(2) == 0)
        def _():
            ds_ref[...] = jnp.zeros_like(ds_ref)

        wv, dm = w_ref[...], dm_ref[...]
        g = _dot(_silu(c_ref[...]), dm, _TN)
        ds_ref[...] += _dot(dm, wv, _NT)
        g_ref[...] = g
        dl_ref[...], nm_ref[...], nv_ref[...] = _adamw(wv, g, m_ref[...], v_ref[...])

    blk = pl.BlockSpec((None, tr, tn), lambda l, i, j: (l, i, j))
    full = jax.ShapeDtypeStruct(w.shape, F32)
    return pl.pallas_call(
        body, name=name, grid=(nl, d // tr, ns // tn),
        in_specs=[pl.BlockSpec((MOD_ROWS, tr), lambda l, i, j: (0, i)),
                  pl.BlockSpec((None, MOD_ROWS, tn), lambda l, i, j: (l, 0, j)), blk, blk, blk],
        out_specs=[blk, blk, blk, blk, pl.BlockSpec((None, MOD_ROWS, tr), lambda l, i, j: (l, 0, i))],
        out_shape=[full, full, full, full, jax.ShapeDtypeStruct((nl, MOD_ROWS, d), F32)],
        compiler_params=_params("parallel", "parallel", "arbitrary"))(cond, d_mod, w, m, v)


def _row_tile(rows, cols, itemsize, mult):
    return _tile(rows, max(mult, (2 << 20) // (cols * itemsize)), mult)


def adam_update(w, g, m, v, *, name):
    r, c = w.shape
    tr = _row_tile(r, c, 4, SUBLANES)

    def body(w_ref, g_ref, m_ref, v_ref, dl_ref, nm_ref, nv_ref):
        dl_ref[...], nm_ref[...], nv_ref[...] = _adamw(w_ref[...], g_ref[...], m_ref[...], v_ref[...])

    blk = pl.BlockSpec((tr, c), lambda i: (i, 0))
    return pl.pallas_call(
        body, name=name, grid=(r // tr,), in_specs=[blk] * 4, out_specs=[blk] * 3,
        out_shape=[jax.ShapeDtypeStruct((r, c), F32)] * 3, compiler_params=_params("parallel"))(w, g, m, v)


def cast_rows(x, dtype, *, name):
    r, c = x.shape
    tr = _row_tile(r, c, 4, 16)

    def body(x_ref, o_ref):
        o_ref[...] = x_ref[...].astype(dtype)

    blk = pl.BlockSpec((tr, c), lambda i: (i, 0))
    return pl.pallas_call(body, name=name, grid=(r // tr,), in_specs=[blk], out_specs=blk,
                          out_shape=jax.ShapeDtypeStruct((r, c), dtype), compiler_params=_params("parallel"))(x)


def add_sibling_half(core, mine, other, *, name):
    a, _, r, c = mine.shape
    tr = _row_tile(r, c, 4, 16)

    def body(core_ref, x_ref, y_ref, o_ref):
        o_ref[...] = (x_ref[...].astype(F32) + y_ref[...].astype(F32)).astype(o_ref.dtype)

    grid_spec = pltpu.PrefetchScalarGridSpec(
        num_scalar_prefetch=1, grid=(a, r // tr),
        in_specs=[pl.BlockSpec((None, None, tr, c), lambda k, i, core_ref: (k, core_ref[0], i, 0)),
                  pl.BlockSpec((None, tr, c), lambda k, i, core_ref: (k, i, 0))],
        out_specs=pl.BlockSpec((None, tr, c), lambda k, i, core_ref: (k, i, 0)))
    return pl.pallas_call(body, name=name, grid_spec=grid_spec, out_shape=jax.ShapeDtypeStruct((a, r, c), WIRE_DTYPE),
                          compiler_params=_params("parallel", "parallel"))(core, mine, other)


def sum_slots(x, *, name):
    n, r, c = x.shape
    tr = _row_tile(r, c * n, 4, 16)

    def body(x_ref, o_ref):
        acc = x_ref[0].astype(F32)
        for k in range(1, n):
            acc = acc + x_ref[k].astype(F32)
        o_ref[...] = acc

    return pl.pallas_call(
        body, name=name, grid=(r // tr,), in_specs=[pl.BlockSpec((n, tr, c), lambda i: (0, i, 0))],
        out_specs=pl.BlockSpec((tr, c), lambda i: (i, 0)), out_shape=jax.ShapeDtypeStruct((r, c), F32),
        compiler_params=_params("parallel"))(x)


def ctx_cond_grad(parts, c_ctx, *, name):
    def body(p_ref, c_ref, o_ref):
        acc = p_ref[0]
        for k in range(1, N_CHIPS):
            acc = acc + p_ref[k]
        o_ref[...] = acc * _dsilu(c_ref[...])

    return pl.pallas_call(body, name=name, out_shape=jax.ShapeDtypeStruct(c_ctx.shape, F32))(parts, c_ctx)


PACK_ROWS = 256


def _tile_rows(shape):
    n = 1
    for dim in shape:
        n *= dim
    return -(-n // (SUBLANES * LANES)) * SUBLANES


def _pack(arrs):
    parts = []
    for a in arrs:
        flat = a.reshape(-1).astype(F32)
        rows = _tile_rows(a.shape)
        parts.append(jnp.pad(flat, (0, rows * LANES - flat.shape[0])).reshape(rows, LANES))
    total = sum(t.shape[0] for t in parts)
    parts.append(jnp.zeros((-total % PACK_ROWS, LANES), F32))
    return jnp.concatenate(parts, axis=0)


def _unpack(flat, shapes):
    lead = flat.shape[:-2]
    outs, r0 = [], 0
    for shape in shapes:
        rows = _tile_rows(shape)
        n = 1
        for dim in shape:
            n *= dim
        piece = flat[..., r0:r0 + rows, :].reshape(lead + (rows * LANES,))[..., :n]
        outs.append(piece.reshape(lead + tuple(shape)))
        r0 += rows
    return outs


WEIGHTS = ('c_ctx', 'mod_w', 'mod_b', 'norm_w', 'ab_w_in', 'ab_qkv_conv', 'ab_a_log', 'ab_dt_bias', 'ab_dn_norm',
           'ab_lru_conv_w', 'ab_lru_conv_b', 'ab_lru_w_r', 'ab_lru_b_r', 'ab_lru_w_i', 'ab_lru_b_i', 'ab_lru_lambda',
           'ab_w_out', 'sc_w_in', 'sc_conv', 'sc_w_out', 'final_norm_w')
BIG_WEIGHTS = ('ab_w_in', 'ab_w_out', 'sc_w_in', 'sc_w_out')


def kernel(x, c, ctx, c_ctx, mod_w, mod_b, norm_w, ab_w_in, ab_qkv_conv, ab_a_log, ab_dt_bias, ab_dn_norm, ab_lru_conv_w, ab_lru_conv_b, ab_lru_w_r, ab_lru_b_r, ab_lru_w_i, ab_lru_b_i, ab_lru_lambda, ab_w_out, sc_w_in, sc_conv, sc_w_out, final_norm_w, loss_target, m_c_ctx, m_mod_w, m_mod_b, m_norm_w, m_ab_w_in, m_ab_qkv_conv, m_ab_a_log, m_ab_dt_bias, m_ab_dn_norm, m_ab_lru_conv_w, m_ab_lru_conv_b, m_ab_lru_w_r, m_ab_lru_b_r, m_ab_lru_w_i, m_ab_lru_b_i, m_ab_lru_lambda, m_ab_w_out, m_sc_w_in, m_sc_conv, m_sc_w_out, m_final_norm_w, v_c_ctx, v_mod_w, v_mod_b, v_norm_w, v_ab_w_in, v_ab_qkv_conv, v_ab_a_log, v_ab_dt_bias, v_ab_dn_norm, v_ab_lru_conv_w, v_ab_lru_conv_b, v_ab_lru_w_r, v_ab_lru_b_r, v_ab_lru_w_i, v_ab_lru_b_i, v_ab_lru_lambda, v_ab_w_out, v_sc_w_in, v_sc_conv, v_sc_w_out, v_final_norm_w):
    given = dict(locals())
    weights = {n: given[n] for n in WEIGHTS}
    mom1 = {n: given['m_' + n] for n in WEIGHTS}
    mom2 = {n: given['v_' + n] for n in WEIGHTS}

    xi, yi, ci = lax.axis_index("x"), lax.axis_index("y"), lax.axis_index("c")
    chip = 2 * xi + yi
    dev = 2 * chip + ci
    x2d, ctx2d, target = x[0], ctx[0], loss_target[0]
    s, d = x2d.shape
    cl = ctx2d.shape[0]
    heads = ab_a_log.shape[-1]
    wdn = heads * LANES
    nb = ab_lru_w_r.shape[2]
    wl = nb * ab_lru_w_r.shape[3]
    sc = sc_w_in.shape[-1]
    ab_out = wdn + wl
    off_beta = 3 * wdn + wl
    ab_state = off_beta + 4 * heads
    ab_in = ab_state + wdn + wl
    ni = ab_in // N_CHIPS
    ns = mod_w.shape[-1]
    grid_rows = s // GRID_W

    def to_col_major(t):
        return t.reshape(grid_rows, GRID_W, t.shape[-1]).swapaxes(0, 1).reshape(s, t.shape[-1])

    def to_raster(t):
        return t.reshape(GRID_W, grid_rows, t.shape[-1]).swapaxes(0, 1).reshape(s, t.shape[-1])

    def from_chips(t):
        return jnp.moveaxis(t[0::N_CORES], 0, 1).reshape(t.shape[1], -1)

    def own_columns(t, width):
        return lax.dynamic_slice_in_dim(t, chip * width, width, axis=t.ndim - 1)

    small_shards = [c[0], ab_qkv_conv[0], ab_lru_conv_w[0], ab_lru_b_r[0], ab_lru_b_i[0], ab_lru_lambda[0], sc_conv[0]]
    gathered0 = all_gather_devices(_pack(small_shards), "ag_small_params")
    c_all, qkv_sh, lcw_sh, lbr_sh, lbi_sh, llam_sh, scv_sh = _unpack(gathered0, [t.shape for t in small_shards])
    qkv_conv, lru_conv_w, sc_conv_w = from_chips(qkv_sh), from_chips(lcw_sh), from_chips(scv_sh)
    lru_b_r, lru_b_i, lru_lam = from_chips(lbr_sh), from_chips(lbi_sh), from_chips(llam_sh)

    shards = [weights[n][0] for n in BIG_WEIGHTS]
    halves = [cast_rows(t, WIRE_DTYPE, name=f"cast_w{k}").reshape(N_CORES, t.shape[0] // N_CORES, t.shape[1])
              for k, t in enumerate(shards)]
    from_chips_half = chip_exchange(halves[:1], gather=True, name="ag_w_chips")
    from_sibling_half = sibling_exchange(from_chips_half, pick=False, name="ag_w_cores")
    w_in_full = core_halves(ci, from_chips_half[0], from_sibling_half[0])
    w_in_full = jnp.moveaxis(w_in_full.reshape(N_CHIPS, d, ni), 0, 1).reshape(d, ab_in)
    w_main = jnp.concatenate([w_in_full[:, :off_beta], w_in_full[:, ab_state:]], axis=1)
    w_ba = jnp.pad(w_in_full[:, off_beta:ab_state], ((0, 0), (0, LANES - 4 * heads)))

    cond = jnp.zeros((MOD_ROWS, d), F32).at[:N_DEV].set(c_all).at[N_DEV].set(c_ctx)
    mod_shard = mod_fwd(cond, mod_w, own_columns(mod_b, ns)[:, None, :], name="mod_fwd")
    gathered_mod = all_gather_devices(mod_shard.reshape(-1, LANES), "ag_mod")
    mod_all = jnp.moveaxis(gathered_mod[0::N_CORES].reshape(N_CHIPS, 2, MOD_ROWS, ns), 0, 2).reshape(2, MOD_ROWS, 3 * d)
    own_mod = lax.dynamic_index_in_dim(mod_all, dev, axis=1, keepdims=False)
    shift, scale, gate = own_mod[:, :d], own_mod[:, d:2 * d], own_mod[:, 2 * d:]
    shift_c, scale_c = mod_all[0, N_DEV, :d], mod_all[0, N_DEV, d:2 * d]

    def pair(a, b):
        return jnp.stack([a, b])[:, None, :]

    hn_all = norm_mod_fwd(x2d, ctx2d, norm_w[0:1], pair(scale[0], scale_c), pair(shift[0], shift_c), name="norm0_fwd")
    proj = mm_nn(hn_all, w_main, out_dtype=F32, name="proj0")
    ba = mm_nn(hn_all, w_ba, out_dtype=F32, name="proj0_ba")
    qkv = qkv_conv_fwd(proj, qkv_conv, s=s, cl=cl, heads=heads, name="qkv_conv_fwd")

    def pad_dh(t):
        return jnp.zeros((SUBLANES, LANES), F32).at[:2, :heads].set(t)

    a_log_t, dt_bias_t = pad_dh(ab_a_log[0]), pad_dh(ab_dt_bias[0])
    gate_dn0 = (3 * wdn + wl) // LANES
    gate_lru0 = gate_dn0 + wdn // LANES
    o_dn, dn_states, late_from_chips = dn_fwd(qkv, ba, a_log_t, dt_bias_t, SideExchange(halves[1:], gather=True), s=s,
                                              cl=cl, heads=heads, name="dn_fwd")
    late_from_sibling = sibling_exchange(list(late_from_chips), pick=False, name="ag_w_cores_late")
    w_ab_out, w_sc_in, w_sc_out = (core_halves(ci, a, b) for a, b in zip(late_from_chips, late_from_sibling))
    w_ab_out, w_sc_in, w_sc_out = w_ab_out.reshape(ab_out, d), w_sc_in.reshape(N_CHIPS, d, sc), w_sc_out.reshape(sc, d)
    y_dn = dn_out_fwd(o_dn, proj, ab_dn_norm, heads=heads, gate_col0=gate_dn0, name="dn_out_fwd")
    lru_in = proj[:, 3 * wdn:3 * wdn + wl]
    xcm = jnp.concatenate([to_col_major(lru_in[:s]), lru_in[s:]], axis=0)
    lru_w = (lru_conv_w, ab_lru_conv_b, ab_lru_w_r[0], lru_b_r, ab_lru_w_i[0], lru_b_i, lru_lam)
    h_lru = to_raster(lru_fwd(xcm, *lru_w, s=s, cl=cl, name="lru_fwd"))
    y_lru = lru_gate_fwd(h_lru, proj, gate_col0=gate_lru0, name="lru_gate_fwd")
    y_ab = jnp.concatenate([y_dn, y_lru], axis=1)
    x1, yo0 = mm_nn(y_ab, w_ab_out, out_dtype=F32, name="out0", resid=x2d, gate=gate[0:1], aux_dtype=ACT_DTYPE,
                    tn_cap=512)

    hn1 = norm_mod_fwd(x1, None, norm_w[1:2], scale[1][None, None, :], shift[1][None, None, :], name="norm1_fwd")
    tm1, tn1, tk1 = _tile(s, 1024, 16), _tile(sc, 1024, LANES), _tile(d, MATMUL_TK, LANES)
    p = matmul(hn1, w_sc_in, m=s, n=4 * sc, k=d, tm=tm1, tn=tn1, tk=tk1, dims=_NN, out_dtype=F32, name="proj1",
               a_spec=pl.BlockSpec((tm1, tk1), lambda i, j, kk: (i, kk)),
               b_spec=pl.BlockSpec((None, tk1, tn1), lambda i, j, kk: (j // (sc // tn1), kk, j % (sc // tn1))))
    y_sc = sc_mix_fwd(p, sc_conv_w, name="sc_mix_fwd")
    x2, yo1 = mm_nn(y_sc, w_sc_out, out_dtype=F32, name="out1", resid=x1, gate=gate[1:2], aux_dtype=ACT_DTYPE,
                    tn_cap=512)
    loss_t, dx2, d_fnw = loss_head(x2, final_norm_w[None], target, name="loss_head")

    d_yo1, d_gate1 = gate_bwd(dx2, yo1, gate[1:2], name="gate1_bwd")
    d_ysc = mm_nt(d_yo1, w_sc_out, out_dtype=F32, name="out1_dx")
    dw_sc_out = mm_tn(y_sc, d_yo1, out_dtype=WIRE_DTYPE, name="out1_dw")
    d_p, d_sc_conv = sc_mix_bwd(p, d_ysc, sc_conv_w, name="sc_mix_bwd")
    tkp = _tile(sc, MATMUL_TK, LANES)
    tnd = _tile(d, 1024, LANES)
    d_hn1 = matmul(d_p, w_sc_in, m=s, n=d, k=4 * sc, tm=tm1, tn=tnd, tk=tkp, dims=_NT, out_dtype=F32, name="proj1_dx",
                   a_spec=pl.BlockSpec((tm1, tkp), lambda i, j, kk: (i, kk)),
                   b_spec=pl.BlockSpec((None, tnd, tkp), lambda i, j, kk: (kk // (sc // tkp), j, kk % (sc // tkp))))
    tks = _tile(s, MATMUL_TK_TOKENS, 16)
    dw_sc_in = matmul(hn1, d_p, m=d, n=4 * sc, k=s, tm=tnd, tn=tn1, tk=tks, dims=_TN, out_dtype=WIRE_DTYPE,
                      name="proj1_dw", a_spec=pl.BlockSpec((tks, tnd), lambda i, j, kk: (kk, i)),
                      b_spec=pl.BlockSpec((tks, tn1), lambda i, j, kk: (kk, j)),
                      out_spec=pl.BlockSpec((None, tnd, tn1), lambda i, j, kk: (j // (sc // tn1), i, j % (sc // tn1))),
                      out_shape=(N_CHIPS, d, sc))
    dx1, d_nw1, d_scale1, d_shift1 = norm_mod_bwd(x1, norm_w[1:2], scale[1:2], d_hn1, row0=0, resid=dx2, init=None,
                                                  name="norm1_bwd")

    d_yo0, d_gate0 = gate_bwd(dx1, yo0, gate[0:1], name="gate0_bwd")
    d_y = mm_nt(d_yo0, w_ab_out, out_dtype=F32, name="out0_dx")
    dw_ab_out = mm_tn(y_ab, d_yo0, out_dtype=WIRE_DTYPE, name="out0_dw")
    d_o, d_gate_dn, d_dn_norm = dn_out_bwd(o_dn, proj, d_y, ab_dn_norm, heads=heads, gate_col0=gate_dn0, name="dn_out_bwd")
    core = ci.astype(jnp.int32).reshape(1)

    def chip_core_slots(t, slot_rows):
        return t.reshape(N_CHIPS, N_CORES, slot_rows // N_CORES, t.shape[-1])

    early_by_chip = [chip_core_slots(dw_ab_out, ab_out // N_CHIPS), chip_core_slots(dw_sc_in, d),
                     chip_core_slots(dw_sc_out, sc // N_CHIPS)]
    early_from_sib = sibling_exchange(early_by_chip, pick=True, name="rs_cores_early")
    early_halves_sum = [add_sibling_half(core, a, b, name=f"rs_add_early{k}")
                        for k, (a, b) in enumerate(zip(early_by_chip, early_from_sib))]
    d_qkv, d_ba, d_alog_t, d_dtb_t, early_chips_sum = dn_bwd(qkv, ba, d_o, dn_states, a_log_t, dt_bias_t,
                                                             SideExchange(early_halves_sum, gather=False), s=s, cl=cl,
                                                             heads=heads, name="dn_bwd")
    d_qkv_raw, d_qkv_conv = qkv_conv_bwd(proj, qkv_conv, d_qkv, s=s, cl=cl, heads=heads, name="qkv_conv_bwd")
    d_h, d_gate_lru = lru_gate_bwd(h_lru, proj, d_y, gate_col0=gate_lru0, dy_col0=wdn // LANES, name="lru_gate_bwd")
    (d_xcm, d_lcw, d_lcb, d_wr, d_br, d_wi, d_bi, d_lam) = lru_bwd(xcm, to_col_major(d_h), *lru_w, s=s, cl=cl,
                                                                  name="lru_bwd")
    d_lru_in = jnp.concatenate([to_raster(d_xcm[:s]), d_xcm[s:]], axis=0)
    ctx_zeros = jnp.zeros((cl, wdn + wl), ACT_DTYPE)
    d_gates = jnp.concatenate([jnp.concatenate([d_gate_dn, d_gate_lru], axis=1), ctx_zeros], axis=0)
    d_proj = jnp.concatenate([d_qkv_raw, d_lru_in, d_gates], axis=1)
    dw_main = mm_tn(hn_all, d_proj, out_dtype=WIRE_DTYPE, name="proj0_dw")
    dw_ba = mm_tn(hn_all, d_ba, out_dtype=WIRE_DTYPE, name="proj0_ba_dw")
    dw_in_full = jnp.concatenate([dw_main[:, :off_beta], dw_ba[:, :4 * heads], dw_main[:, off_beta:]], axis=1)
    by_chip = [chip_core_slots(jnp.moveaxis(dw_in_full.reshape(d, N_CHIPS, ni), 1, 0), d)]
    from_sibling = sibling_exchange(by_chip, pick=True, name="rs_cores")
    halves_sum = [add_sibling_half(core, by_chip[0], from_sibling[0], name="rs_add")]
    d_hn_ba = mm_nt(d_ba, w_ba, out_dtype=F32, name="proj0_ba_dx")
    d_hn_all, last_chips_sum = mm_nt(d_proj, w_main, out_dtype=F32, name="proj0_dx", resid=d_hn_ba,
                                     side=SideExchange(halves_sum, gather=False))
    _, d_nw0c, d_scale_c, d_shift_c = norm_mod_bwd(ctx2d, norm_w[0:1], scale_c[None], d_hn_all, row0=s, resid=None,
                                                   init=None, name="norm0_bwd_ctx")
    dx, d_nw0, d_scale0, d_shift0 = norm_mod_bwd(x2d, norm_w[0:1], scale[0:1], d_hn_all, row0=0, resid=dx1, init=d_nw0c,
                                                 name="norm0_bwd")

    from_chips_sum = list(last_chips_sum) + list(early_chips_sum)
    reduced = [sum_slots(t, name=f"rs_sum{k}")[None] for k, t in enumerate(from_chips_sum)]
    both_halves = [core_halves(ci, a, b) for a, b in zip(reduced, sibling_exchange(reduced, pick=False, name="rs_back"))]
    grads = {n: t.reshape(weights[n].shape) for n, t in zip(BIG_WEIGHTS, both_halves)}

    d_mod_own = jnp.stack([jnp.concatenate([d_shift0[0], d_scale0[0], d_gate0[0]]),
                           jnp.concatenate([d_shift1[0], d_scale1[0], d_gate1[0]])])
    d_mod_ctx = jnp.concatenate([d_shift_c[0], d_scale_c[0], jnp.zeros((d,), F32)])
    summable = [
        loss_t[0, 0:1], d_mod_own.at[0].add(d_mod_ctx), d_mod_ctx, jnp.concatenate([d_nw0, d_nw1], axis=0), d_qkv_conv,
        d_alog_t[:, :2, 0].T, d_dtb_t[:, :2, 0].T, d_dn_norm[0], d_lcw, d_lcb[0], d_wr, d_br, d_wi, d_bi, d_lam,
        d_sc_conv, d_fnw[0]]
    sum_shapes = [t.shape for t in summable]
    gathered1 = all_gather_devices(_pack(summable + [d_mod_own]), "ag_small_grads")
    totals = _unpack(sum_slots(gathered1, name="sum_small_grads"), sum_shapes)
    (loss_sum, g_mod_b, d_mod_ctx_sum, g_norm_w, g_qkv_conv, g_a_log, g_dt_bias, g_dn_norm, g_lcw, g_lcb, g_wr, g_br,
     g_wi, g_bi, g_lam, g_sc_conv, g_fnw) = totals
    d_mod_rows = _unpack(gathered1, sum_shapes + [d_mod_own.shape])[-1]

    d_mod_all = jnp.zeros((2, MOD_ROWS, 3 * d), F32).at[:, :N_DEV].set(jnp.moveaxis(d_mod_rows, 0, 1))
    d_mod_all = d_mod_all.at[0, N_DEV].set(d_mod_ctx_sum)
    g_mod_w, dl_mod_w, nm_mod_w, nv_mod_w, d_silu = mod_adam(cond, own_columns(d_mod_all, ns), mod_w, m_mod_w, v_mod_w,
                                                             name="mod_adam")
    gathered2 = all_gather_devices(d_silu[0, N_DEV].reshape(-1, LANES), "ag_ctx_cond")
    g_c_ctx = ctx_cond_grad(gathered2[0::N_CORES], c_ctx.reshape(-1, LANES), name="ctx_cond_grad").reshape(d)

    grads.update({
        'c_ctx': g_c_ctx, 'mod_w': g_mod_w, 'mod_b': g_mod_b, 'norm_w': g_norm_w,
        'ab_qkv_conv': own_columns(g_qkv_conv, qkv_conv.shape[1] // N_CHIPS), 'ab_a_log': g_a_log, 'ab_dt_bias': g_dt_bias,
        'ab_dn_norm': g_dn_norm, 'ab_lru_conv_w': own_columns(g_lcw, wl // N_CHIPS), 'ab_lru_conv_b': g_lcb,
        'ab_lru_w_r': g_wr, 'ab_lru_b_r': own_columns(g_br, wl // N_CHIPS), 'ab_lru_w_i': g_wi,
        'ab_lru_b_i': own_columns(g_bi, wl // N_CHIPS), 'ab_lru_lambda': own_columns(g_lam, wl // N_CHIPS),
        'sc_conv': own_columns(g_sc_conv, sc // N_CHIPS), 'final_norm_w': g_fnw})
    grads = {n: grads[n].reshape(weights[n].shape) for n in WEIGHTS}

    delta, new_m, new_v = {'mod_w': dl_mod_w}, {'mod_w': nm_mod_w}, {'mod_w': nv_mod_w}
    for k, n in enumerate(BIG_WEIGHTS):
        as2d = lambda t: t.reshape(-1, t.shape[-1])
        upd = adam_update(as2d(weights[n]), as2d(grads[n]), as2d(mom1[n]), as2d(mom2[n]), name=f"adam_big{k}")
        delta[n], new_m[n], new_v[n] = (t.reshape(weights[n].shape) for t in upd)
    small = [n for n in WEIGHTS if n not in BIG_WEIGHTS and n != 'mod_w']
    small_shapes = [weights[n].shape for n in small]
    upd = adam_update(*[_pack([src[n] for n in small]) for src in (weights, grads, mom1, mom2)], name="adam_small")
    for out, flat in zip((delta, new_m, new_v), upd):
        out.update(dict(zip(small, _unpack(flat, small_shapes))))

    return (loss_sum[0], dx[None], *[grads[n] for n in WEIGHTS], *[delta[n] for n in WEIGHTS],
            *[new_m[n] for n in WEIGHTS], *[new_v[n] for n in WEIGHTS])
```

```python
import functools

import jax
import jax.numpy as jnp
from jax import lax
from jax.experimental import pallas as pl
from jax.experimental.pallas import tpu as pltpu

F32 = jnp.float32
BF16 = jnp.bfloat16
MXU_DTYPE = BF16
ACT_DTYPE = BF16
WIRE_DTYPE = BF16

EPS = 1e-6
GRID_W = 64
CHUNK = 64
DN_CONV_OFFSETS = (-2, -1, 0, 1)
SC_CONV_OFFSETS = (-1, 0, 1)
LRU_C = 8.0
ADAM_LR, ADAM_B1, ADAM_B2, ADAM_EPS, ADAM_WD, ADAM_STEP = 0.001, 0.9, 0.999, 1e-08, 0.01, 10

LANES = 128
SUBLANES = 8
N_CHIPS, N_CORES = 4, 2
N_DEV = N_CHIPS * N_CORES
INV_PRECISION = None
INV_BLOCK = 16

_MESH = pl.DeviceIdType.MESH
_ANY = pl.BlockSpec(memory_space=pl.ANY)
_NN = (((1,), (0,)), ((), ()))
_NT = (((1,), (1,)), ((), ()))
_TN = (((0,), (0,)), ((), ()))


def _tile(n, cap, mult):
    best = None
    for t in range(mult, min(n, cap) + 1, mult):
        if n % t == 0:
            best = t
    return n if best is None else best


def _iota(shape, dim):
    return lax.broadcasted_iota(jnp.int32, shape, dim)


def _dot(a, b, dims):
    return lax.dot_general(a.astype(MXU_DTYPE), b.astype(MXU_DTYPE), dims, preferred_element_type=F32)


def _silu(x):
    return x * jax.nn.sigmoid(x)


def _dsilu(x):
    s = jax.nn.sigmoid(x)
    return s * (1.0 + x * (1.0 - s))


V7X_VMEM_BYTES = 64 * 1024 * 1024
BIG_KERNEL_VMEM = V7X_VMEM_BYTES * 15 // 16
MATMUL_TK = 4096
MATMUL_TK_TOKENS = 8192
MATMUL_VMEM = V7X_VMEM_BYTES * 7 // 8


def _params(*sem, vmem=None):
    return pltpu.CompilerParams(dimension_semantics=sem, vmem_limit_bytes=vmem)


def _place():
    return lax.axis_index("x"), lax.axis_index("y"), lax.axis_index("c")


def all_gather_devices(block, name):
    def body(x_ref, out_ref, send_sems, recv_sems, local_sem):
        x, y, c = _place()
        me, sibling = (x, y, c), (x, y, 1 - c)
        chips = [(1 - x, y), (x, 1 - y), (1 - x, 1 - y)]

        def slot(px, py, pc):
            return out_ref.at[4 * px + 2 * py + pc]

        def copy(k, block_of, to, src=None):
            return pltpu.make_async_remote_copy(
                src_ref=slot(*block_of) if src is None else src, dst_ref=slot(*block_of),
                send_sem=send_sems.at[k], recv_sem=recv_sems.at[k], device_id=to, device_id_type=_MESH)

        mine = pltpu.make_async_copy(x_ref, slot(*me), local_sem)
        mine.start()
        first = [copy(0, me, sibling, src=x_ref)]
        first += [copy(1 + j, me, (*chip, c), src=x_ref) for j, chip in enumerate(chips)]
        for cp in first:
            cp.start()
        passed = [copy(4 + j, (*chip, c), sibling) for j, chip in enumerate(chips)]
        for j, chip in enumerate(chips):
            copy(1 + j, (*chip, c), me).wait_recv()
            passed[j].start()
        copy(0, sibling, me).wait_recv()
        for j, chip in enumerate(chips):
            copy(4 + j, (*chip, 1 - c), me).wait_recv()
        for cp in first + passed:
            cp.wait_send()
        mine.wait()

    return pl.pallas_call(
        body, name=name,
        out_shape=jax.ShapeDtypeStruct((N_DEV,) + block.shape, block.dtype),
        in_specs=[_ANY], out_specs=_ANY,
        scratch_shapes=[pltpu.SemaphoreType.DMA((7,)), pltpu.SemaphoreType.DMA((7,)), pltpu.SemaphoreType.DMA],
    )(block)


def chip_exchange(srcs, *, gather, name):
    n = len(srcs)

    def body(*refs):
        copies = _chip_copies(refs[:n], refs[n:2 * n], refs[2 * n:], gather)
        for cp in copies:
            cp.start()
        for cp in copies:
            cp.wait()

    return pl.pallas_call(
        body, name=name, out_shape=_chip_out_shapes(srcs), in_specs=[_ANY] * n, out_specs=[_ANY] * n,
        scratch_shapes=_chip_sems(n),
    )(*srcs)


def _chip_copies(src, out, sems, gather):
    send_sems, recv_sems, local_sems = sems
    x, y, c = _place()
    my = 2 * x + y
    peers = [(1 - x, y), (x, 1 - y), (1 - x, 1 - y)]
    copies = []
    for k in range(len(src)):
        own = src[k].at[c] if gather else src[k].at[my]
        copies.append(pltpu.make_async_copy(own, out[k].at[my], local_sems.at[k]))
        for j, (px, py) in enumerate(peers):
            copies.append(pltpu.make_async_remote_copy(
                src_ref=own if gather else src[k].at[2 * px + py], dst_ref=out[k].at[my],
                send_sem=send_sems.at[k, j], recv_sem=recv_sems.at[k, j],
                device_id=(px, py, c), device_id_type=_MESH))
    return copies


def _chip_out_shapes(srcs):
    return [jax.ShapeDtypeStruct((N_CHIPS,) + s.shape[1:], s.dtype) for s in srcs]


def _chip_sems(n):
    return [pltpu.SemaphoreType.DMA((n, 3)), pltpu.SemaphoreType.DMA((n, 3)), pltpu.SemaphoreType.DMA((n,))]


class SideExchange:
    def __init__(self, srcs, gather):
        self.srcs, self.gather, self.n = list(srcs), gather, len(srcs)

    def specs(self):
        return [_ANY] * self.n, _chip_out_shapes(self.srcs), [_ANY] * self.n, _chip_sems(self.n)

    def copies(self, src_refs, out_refs, sem_refs):
        return _chip_copies(src_refs, out_refs, sem_refs, self.gather)

    def run(self, src_refs, out_refs, sem_refs, step, last_step, compute):
        @pl.when(step == 0)
        def _():
            for cp in self.copies(src_refs, out_refs, sem_refs):
                cp.start()

        compute()

        @pl.when(step == last_step)
        def _():
            for cp in self.copies(src_refs, out_refs, sem_refs):
                cp.wait()


class SiblingSide(SideExchange):
    def __init__(self, srcs, pick):
        self.srcs, self.pick, self.n = list(srcs), pick, len(srcs)

    def specs(self):
        shapes = [jax.ShapeDtypeStruct(s.shape[:1] + s.shape[2:] if self.pick else s.shape, s.dtype) for s in self.srcs]
        return ([_ANY] * self.n, shapes, [_ANY] * self.n,
                [pltpu.SemaphoreType.DMA((self.n,)), pltpu.SemaphoreType.DMA((self.n,))])

    def copies(self, src_refs, out_refs, sem_refs):
        return _sibling_copies(src_refs, out_refs, sem_refs, self.pick)


def _sibling_copies(src, out, sems, pick):
    send_sems, recv_sems = sems
    x, y, c = _place()
    copies = []
    for k in range(len(src)):
        s_ref = src[k].at[pl.ds(0, src[k].shape[0]), 1 - c] if pick else src[k]
        copies.append(pltpu.make_async_remote_copy(
            src_ref=s_ref, dst_ref=out[k], send_sem=send_sems.at[k], recv_sem=recv_sems.at[k],
            device_id=(x, y, 1 - c), device_id_type=_MESH))
    return copies


def sibling_exchange(srcs, *, pick, name):
    n = len(srcs)

    def body(*refs):
        src, out = refs[:n], refs[n:2 * n]
        send_sems, recv_sems = refs[2 * n:]
        x, y, c = _place()
        copies = []
        for k in range(n):
            s_ref = src[k].at[pl.ds(0, src[k].shape[0]), 1 - c] if pick else src[k]
            cp = pltpu.make_async_remote_copy(
                src_ref=s_ref, dst_ref=out[k], send_sem=send_sems.at[k], recv_sem=recv_sems.at[k],
                device_id=(x, y, 1 - c), device_id_type=_MESH)
            cp.start()
            copies.append(cp)
        for cp in copies:
            cp.wait()

    outs = [jax.ShapeDtypeStruct(s.shape[:1] + s.shape[2:] if pick else s.shape, s.dtype) for s in srcs]
    return pl.pallas_call(
        body, name=name, out_shape=outs, in_specs=[_ANY] * n, out_specs=[_ANY] * n,
        scratch_shapes=[pltpu.SemaphoreType.DMA((n,)), pltpu.SemaphoreType.DMA((n,))],
    )(*srcs)


def core_halves(core, mine, other):
    return jnp.where(core == 0, jnp.stack([mine, other], axis=1), jnp.stack([other, mine], axis=1))


def matmul(a, b, *, m, n, k, tm, tn, tk, a_spec, b_spec, dims, out_dtype, name,
           out_spec=None, out_shape=None, resid=None, gate=None, aux_dtype=None, side=None):
    nk = k // tk
    o_spec = out_spec or pl.BlockSpec((tm, tn), lambda i, j, kk: (i, j))
    o_shape = out_shape or (m, n)
    n_in = 2 + (resid is not None) + (gate is not None)
    n_out = 1 + (aux_dtype is not None)
    n_side = 0 if side is None else side.n
    side_in, side_shapes, side_out, side_sems = ([], [], [], []) if side is None else side.specs()

    def body(*refs):
        if side is None:
            return compute(*refs)
        outs0 = n_in + n_side
        scratch = refs[outs0 + n_out + n_side:]
        main = refs[:n_in] + refs[outs0:outs0 + n_out] + scratch[:len(scratch) - len(side_sems)]
        step = (pl.program_id(0) * (n // tn) + pl.program_id(1)) * nk + pl.program_id(2)
        side.run(refs[n_in:outs0], refs[outs0 + n_out:outs0 + n_out + n_side], scratch[len(scratch) - len(side_sems):],
                 step, (m // tm) * (n // tn) * nk - 1, lambda: compute(*main))

    def compute(*refs):
        a_ref, b_ref = refs[0], refs[1]
        pos = 2
        r_ref = g_ref = aux_ref = None
        if resid is not None:
            r_ref, pos = refs[pos], pos + 1
        if gate is not None:
            g_ref, pos = refs[pos], pos + 1
        o_ref, pos = refs[pos], pos + 1
        if aux_dtype is not None:
            aux_ref, pos = refs[pos], pos + 1
        prod = _dot(a_ref[...], b_ref[...], dims)

        def finish(y):
            if aux_ref is not None:
                aux_ref[...] = y.astype(aux_dtype)
            if g_ref is not None:
                y = y * g_ref[...]
            if r_ref is not None:
                y = y + r_ref[...]
            o_ref[...] = y.astype(out_dtype)

        if nk == 1:
            finish(prod)
            return
        acc = refs[pos]
        kk = pl.program_id(2)

        @pl.when(kk == 0)
        def _():
            acc[...] = prod

        @pl.when((kk > 0) & (kk < nk - 1))
        def _():
            acc[...] += prod

        @pl.when(kk == nk - 1)
        def _():
            finish(acc[...] + prod)

    ins, in_specs = [a, b], [a_spec, b_spec]
    if resid is not None:
        ins.append(resid)
        in_specs.append(pl.BlockSpec((tm, tn), lambda i, j, kk: (i, j)))
    if gate is not None:
        ins.append(gate)
        in_specs.append(pl.BlockSpec((1, tn), lambda i, j, kk: (0, j)))
    outs, out_specs = [jax.ShapeDtypeStruct(o_shape, out_dtype)], [o_spec]
    if aux_dtype is not None:
        outs.append(jax.ShapeDtypeStruct((m, n), aux_dtype))
        out_specs.append(pl.BlockSpec((tm, tn), lambda i, j, kk: (i, j)))
    sem = ("parallel", "parallel", "arbitrary") if side is None else ("arbitrary",) * 3
    res = pl.pallas_call(
        body, name=name, grid=(m // tm, n // tn, nk), in_specs=in_specs + side_in, out_specs=out_specs + side_out,
        out_shape=outs + side_shapes, scratch_shapes=([pltpu.VMEM((tm, tn), F32)] if nk > 1 else []) + side_sems,
        compiler_params=_params(*sem, vmem=MATMUL_VMEM),
    )(*ins, *([] if side is None else side.srcs))
    main = res[:n_out] if aux_dtype is not None else res[0]
    return main if side is None else (main, res[n_out:])


def mm_nn(a, b, *, out_dtype, name, tm_cap=1088, tn_cap=1024, tk_cap=MATMUL_TK, **kw):
    m, k = a.shape
    n = b.shape[1]
    tm, tn, tk = _tile(m, tm_cap, 16), _tile(n, tn_cap, LANES), _tile(k, tk_cap, LANES)
    return matmul(a, b, m=m, n=n, k=k, tm=tm, tn=tn, tk=tk, dims=_NN, out_dtype=out_dtype, name=name,
                  a_spec=pl.BlockSpec((tm, tk), lambda i, j, kk: (i, kk)),
                  b_spec=pl.BlockSpec((tk, tn), lambda i, j, kk: (kk, j)), **kw)


def mm_nt(a, b, *, out_dtype, name, tm_cap=1088, tn_cap=1024, tk_cap=MATMUL_TK // 2, **kw):
    m, k = a.shape
    n = b.shape[0]
    tm, tn, tk = _tile(m, tm_cap, 16), _tile(n, tn_cap, LANES), _tile(k, tk_cap, LANES)
    return matmul(a, b, m=m, n=n, k=k, tm=tm, tn=tn, tk=tk, dims=_NT, out_dtype=out_dtype, name=name,
                  a_spec=pl.BlockSpec((tm, tk), lambda i, j, kk: (i, kk)),
                  b_spec=pl.BlockSpec((tn, tk), lambda i, j, kk: (j, kk)), **kw)


def mm_tn(a, b, *, out_dtype, name, tm_cap=1024, tn_cap=1024, tk_cap=MATMUL_TK_TOKENS, **kw):
    k, m = a.shape
    n = b.shape[1]
    tm, tn, tk = _tile(m, tm_cap, LANES), _tile(n, tn_cap, LANES), _tile(k, tk_cap, 16)
    return matmul(a, b, m=m, n=n, k=k, tm=tm, tn=tn, tk=tk, dims=_TN, out_dtype=out_dtype, name=name,
                  a_spec=pl.BlockSpec((tk, tm), lambda i, j, kk: (kk, i)),
                  b_spec=pl.BlockSpec((tk, tn), lambda i, j, kk: (kk, j)), **kw)


def _rms(x):
    r = lax.rsqrt(jnp.mean(x * x, axis=-1, keepdims=True) + EPS)
    return x * r, r


def norm_mod_fwd(x, ctx, nw, scale2, shift2, *, name):
    s, d = x.shape
    cl = 0 if ctx is None else ctx.shape[0]
    tr = _tile(s if ctx is None else cl, 256, 16)
    n_lat = s // tr

    def body(*refs):
        if ctx is None:
            x_ref, nw_ref, sc_ref, sh_ref, o_ref = refs
            v = x_ref[...]
        else:
            x_ref, c_ref, nw_ref, sc_ref, sh_ref, o_ref = refs
            v = jnp.where(pl.program_id(0) < n_lat, x_ref[...], c_ref[...])
        y = _rms(v)[0] * nw_ref[...]
        o_ref[...] = (y * (1.0 + sc_ref[...]) + sh_ref[...]).astype(o_ref.dtype)

    sel = pl.BlockSpec((None, 1, d), lambda i: (i // n_lat, 0, 0))
    ins = [x] if ctx is None else [x, ctx]
    specs = [pl.BlockSpec((tr, d), lambda i: (jnp.minimum(i, n_lat - 1), 0))]
    if ctx is not None:
        specs.append(pl.BlockSpec((tr, d), lambda i: (jnp.maximum(i - n_lat, 0), 0)))
    return pl.pallas_call(
        body, name=name, grid=((s + cl) // tr,),
        in_specs=specs + [pl.BlockSpec((1, d), lambda i: (0, 0)), sel, sel],
        out_specs=pl.BlockSpec((tr, d), lambda i: (i, 0)),
        out_shape=jax.ShapeDtypeStruct((s + cl, d), ACT_DTYPE),
        compiler_params=_params("parallel"),
    )(*ins, nw, scale2, shift2)


def norm_mod_bwd(x, nw, scale, d_hn, *, row0, resid, init, name):
    r, d = x.shape
    tr = _tile(r, 256, 16)
    off = row0 // tr
    want_dx = resid is not None

    def body(*refs):
        x_ref, nw_ref, sc_ref, dh_ref = refs[:4]
        pos = 4
        res_ref = init_ref = dx_ref = None
        if want_dx:
            res_ref, pos = refs[pos], pos + 1
        if init is not None:
            init_ref, pos = refs[pos], pos + 1
        if want_dx:
            dx_ref, pos = refs[pos], pos + 1
        dnw_ref, dsc_ref, dsh_ref = refs[pos:pos + 3]
        i = pl.program_id(0)

        @pl.when(i == 0)
        def _():
            dnw_ref[...] = jnp.zeros_like(dnw_ref) if init_ref is None else init_ref[...]
            dsc_ref[...] = jnp.zeros_like(dsc_ref)
            dsh_ref[...] = jnp.zeros_like(dsh_ref)

        nrm, rs = _rms(x_ref[...])
        w = nw_ref[...]
        dh = dh_ref[...].astype(F32)
        dsh_ref[...] += jnp.sum(dh, axis=0, keepdims=True)
        dsc_ref[...] += jnp.sum(dh * (nrm * w), axis=0, keepdims=True)
        dy = dh * (1.0 + sc_ref[...])
        dnw_ref[...] += jnp.sum(dy * nrm, axis=0, keepdims=True)
        if want_dx:
            dn = dy * w
            dx = rs * (dn - nrm * jnp.mean(dn * nrm, axis=-1, keepdims=True))
            dx_ref[...] = dx + res_ref[...]

    row = pl.BlockSpec((tr, d), lambda i: (i, 0))
    vec = pl.BlockSpec((1, d), lambda i: (0, 0))
    ins, specs = [x, nw, scale, d_hn], [row, vec, vec, pl.BlockSpec((tr, d), lambda i: (i + off, 0))]
    if want_dx:
        ins.append(resid)
        specs.append(row)
    if init is not None:
        ins.append(init)
        specs.append(vec)
    vshape = jax.ShapeDtypeStruct((1, d), F32)
    outs, ospecs = [vshape] * 3, [vec] * 3
    if want_dx:
        outs, ospecs = [jax.ShapeDtypeStruct((r, d), F32)] + outs, [row] + ospecs
    res = pl.pallas_call(body, name=name, grid=(r // tr,), in_specs=specs, out_specs=ospecs, out_shape=outs,
                         compiler_params=_params("arbitrary"))(*ins)
    return tuple(res) if want_dx else (None,) + tuple(res)


def gate_bwd(dx, yo, gate, *, name):
    s, d = dx.shape
    tr = _tile(s, 256, 16)

    def body(dx_ref, yo_ref, g_ref, dyo_ref, dg_ref):
        @pl.when(pl.program_id(0) == 0)
        def _():
            dg_ref[...] = jnp.zeros_like(dg_ref)

        g = dx_ref[...]
        dg_ref[...] += jnp.sum(g * yo_ref[...].astype(F32), axis=0, keepdims=True)
        dyo_ref[...] = (g * g_ref[...]).astype(dyo_ref.dtype)

    row = pl.BlockSpec((tr, d), lambda i: (i, 0))
    vec = pl.BlockSpec((1, d), lambda i: (0, 0))
    return pl.pallas_call(
        body, name=name, grid=(s // tr,), in_specs=[row, row, vec], out_specs=[row, vec],
        out_shape=[jax.ShapeDtypeStruct((s, d), ACT_DTYPE), jax.ShapeDtypeStruct((1, d), F32)],
        compiler_params=_params("arbitrary"))(dx, yo, gate)


def loss_head(x, fw, target, *, name):
    s, d = x.shape
    tr = _tile(s, 256, 16)

    def body(x_ref, w_ref, t_ref, loss_ref, dx_ref, dw_ref):
        @pl.when(pl.program_id(0) == 0)
        def _():
            loss_ref[...] = jnp.zeros_like(loss_ref)
            dw_ref[...] = jnp.zeros_like(dw_ref)

        nrm, rs = _rms(x_ref[...])
        w = w_ref[...]
        err = nrm * w - t_ref[...]
        loss_ref[...] += 0.5 * jnp.sum(jnp.mean(err * err, axis=-1, keepdims=True))
        d_out = err * (1.0 / d)
        dw_ref[...] += jnp.sum(d_out * nrm, axis=0, keepdims=True)
        dn = d_out * w
        dx_ref[...] = rs * (dn - nrm * jnp.mean(dn * nrm, axis=-1, keepdims=True))

    row = pl.BlockSpec((tr, d), lambda i: (i, 0))
    vec = pl.BlockSpec((1, d), lambda i: (0, 0))
    return pl.pallas_call(
        body, name=name, grid=(s // tr,), in_specs=[row, vec, row],
        out_specs=[pl.BlockSpec((SUBLANES, LANES), lambda i: (0, 0)), row, vec],
        out_shape=[jax.ShapeDtypeStruct((SUBLANES, LANES), F32), jax.ShapeDtypeStruct((s, d), F32),
                   jax.ShapeDtypeStruct((1, d), F32)],
        compiler_params=_params("arbitrary"))(x, fw, target)


def _segments(rows, seg_a, seg_b):
    t = _iota((rows, 1), 0)
    if seg_b == 0:
        return t % seg_a, seg_a
    return jnp.where(t < seg_a, t, t - seg_a), jnp.where(t < seg_a, seg_a, seg_b)


def _shift(x, o, seg):
    if o == 0:
        return x
    pos, length = _segments(x.shape[0], *seg)
    y = pltpu.roll(x, (-o) % x.shape[0], 0)
    return jnp.where((pos + o >= 0) & (pos + o < length), y, 0.0)


def _conv(x, w, offsets, seg):
    acc = None
    for j, o in enumerate(offsets):
        term = w[j:j + 1, :] * _shift(x, o, seg)
        acc = term if acc is None else acc + term
    return acc


def _conv_bwd(x, w, dy, offsets, seg):
    dx = None
    dw = jnp.zeros(w.shape, F32)
    row = _iota(w.shape, 0)
    for j, o in enumerate(offsets):
        term = w[j:j + 1, :] * _shift(dy, -o, seg)
        dx = term if dx is None else dx + term
        dwj = jnp.sum(dy * _shift(x, o, seg), axis=0, keepdims=True)
        dw = dw + jnp.where(row == j, dwj, 0.0)
    return dx, dw


def _qkv_post(y, group, scale):
    a = _silu(y)
    n = a * lax.rsqrt(jnp.sum(a * a, axis=-1, keepdims=True) + EPS)
    return jnp.where(group == 0, n * scale, jnp.where(group == 1, n, a))


def qkv_conv_fwd(proj, conv_w, *, s, cl, heads, name):
    m = s + cl
    dh = LANES
    scale = dh ** -0.5

    def body(x_ref, w_ref, o_ref):
        group = pl.program_id(0) // heads
        y = _conv(x_ref[...], w_ref[...], DN_CONV_OFFSETS, (s, cl))
        o_ref[...] = _qkv_post(y, group, scale).astype(o_ref.dtype)

    return pl.pallas_call(
        body, name=name, grid=(3 * heads,),
        in_specs=[pl.BlockSpec((m, dh), lambda j: (0, j)), pl.BlockSpec((len(DN_CONV_OFFSETS), dh), lambda j: (0, j))],
        out_specs=pl.BlockSpec((m, dh), lambda j: (0, j)),
        out_shape=jax.ShapeDtypeStruct((m, 3 * heads * dh), ACT_DTYPE),
        compiler_params=_params("parallel"))(proj, conv_w)


def qkv_conv_bwd(proj, conv_w, dqkv, *, s, cl, heads, name):
    m = s + cl
    dh = LANES
    scale = dh ** -0.5
    kk = len(DN_CONV_OFFSETS)

    def body(x_ref, w_ref, d_ref, dx_ref, dw_ref):
        group = pl.program_id(0) // heads
        x, w = x_ref[...], w_ref[...]
        y = _conv(x, w, DN_CONV_OFFSETS, (s, cl))
        a = _silu(y)
        rn = lax.rsqrt(jnp.sum(a * a, axis=-1, keepdims=True) + EPS)
        n = a * rn
        dout = d_ref[...] * jnp.where(group == 0, scale, 1.0)
        da_norm = rn * (dout - n * jnp.sum(dout * n, axis=-1, keepdims=True))
        dy = jnp.where(group == 2, dout, da_norm) * _dsilu(y)
        dx, dw = _conv_bwd(x, w, dy, DN_CONV_OFFSETS, (s, cl))
        dx_ref[...] = dx.astype(dx_ref.dtype)
        dw_ref[...] = dw

    col = pl.BlockSpec((m, dh), lambda j: (0, j))
    wspec = pl.BlockSpec((kk, dh), lambda j: (0, j))
    return pl.pallas_call(
        body, name=name, grid=(3 * heads,),
        in_specs=[col, wspec, pl.BlockSpec((None, m, dh), lambda j: (j // heads, 0, j % heads))],
        out_specs=[col, wspec],
        out_shape=[jax.ShapeDtypeStruct((m, 3 * heads * dh), ACT_DTYPE),
                   jax.ShapeDtypeStruct((kk, 3 * heads * dh), F32)],
        compiler_params=_params("parallel"))(proj, conv_w, dqkv)


def _scan_masks(d):
    t, s = _iota((CHUNK, CHUNK), 0), _iota((CHUNK, CHUNK), 1)
    return ((s <= t), (s < t)) if d == 0 else ((s >= t), (s > t))


def _bmm(spec, a, b, precision=None):
    if precision is None:
        a, b = a.astype(MXU_DTYPE), b.astype(MXU_DTYPE)
    return jnp.einsum(spec, a, b, precision=precision, preferred_element_type=F32)


def _unit_tri_inverse(a):
    mm = functools.partial(_bmm, 'nts,nsr->ntr', precision=INV_PRECISION)
    row, col = _iota((CHUNK, CHUNK), 0), _iota((CHUNK, CHUNK), 1)
    eye = (row == col).astype(F32)
    dg = jnp.where(row // INV_BLOCK == col // INV_BLOCK, a, 0.0)
    off = a - dg
    p = eye - dg
    pw = dg
    for _ in range(3):
        pw = mm(pw, pw)
        p = p + mm(p, pw)
    n = mm(p, off)
    r = eye - n
    return mm(r + mm(r, mm(n, n)), p)


def _dn_intra(q, k, v, beta_b, gc_b, d):
    dh = q.shape[-1]
    incl, strict = _scan_masks(d)
    gc64 = gc_b[:, :, :CHUNK]
    diff = gc64 - jnp.swapaxes(gc64, 1, 2)
    decay = jnp.where(incl, jnp.exp(jnp.where(incl, diff, 0.0)), 0.0)
    qk_kk = _bmm('ntd,nsd->nts', jnp.concatenate([q, k], axis=1), k)
    qk, kk = qk_kk[:, :CHUNK], qk_kk[:, CHUNK:]
    a = jnp.where(strict, beta_b[:, :, :CHUNK] * kk * decay, 0.0)
    tinv = _unit_tri_inverse(a)
    rhs = jnp.concatenate([beta_b * jnp.exp(gc_b) * k, beta_b * v], axis=-1)
    wu = _bmm('nts,nsd->ntd', tinv, rhs, INV_PRECISION)
    w, u = wu[:, :, :dh], wu[:, :, dh:]
    last = CHUNK - 1 if d == 0 else 0
    gl = gc_b[:, last:last + 1, :]
    ke = k * jnp.exp(gl - gc_b)
    ge = jnp.exp(gl)
    return w, u, ke, ge, qk * decay, q * jnp.exp(gc_b)


def _dn_step(s, w, u, ke, ge, aqk, qg):
    ws_qs = _dot(jnp.concatenate([w, qg], axis=0), s, _NN)
    u2 = u - ws_qs[:CHUNK]
    s_new = ge * s + _dot(ke, u2, _TN)
    o = ws_qs[CHUNK:] + _dot(aqk, u2, _NN)
    return s_new, o


def _chunk_cumsum(x, d):
    rows = x.shape[0]
    pos = _iota((rows, 1), 0) % CHUNK
    step = 1
    while step < CHUNK:
        if d == 0:
            x = x + jnp.where(pos >= step, pltpu.roll(x, step, 0), 0.0)
        else:
            x = x + jnp.where(pos < CHUNK - step, pltpu.roll(x, rows - step, 0), 0.0)
        step *= 2
    return x


def _pick_lane(x, j):
    return jnp.sum(jnp.where(_iota(x.shape, 1) == j, x, 0.0), axis=1, keepdims=True)


def _dn_gates(ba, a_log, dt_bias, d, h, heads):
    braw = _pick_lane(ba, d * heads + h)
    araw = _pick_lane(ba, (2 + d) * heads + h)
    a_neg = -jnp.exp(_pick_lane(a_log[d:d + 1, :], h))
    pre = araw + _pick_lane(dt_bias[d:d + 1, :], h)
    return jax.nn.sigmoid(braw), a_neg * jax.nn.softplus(pre), pre, a_neg


_DN_SUB_FWD = 16
_DN_SUB_BWD = 16


def _for_sub_batches(s, cl, fn, sub=_DN_SUB_FWD):
    for base, total in ((0, s), (s, cl)):
        nch = min(sub, total // CHUNK)
        rows_per = nch * CHUNK
        count = total // rows_per

        def run(i, carry, base=base, nch=nch, rows_per=rows_per):
            row0 = pl.multiple_of(base + i * rows_per, rows_per)
            ge0 = pl.multiple_of((base // CHUNK + i * nch) * SUBLANES, nch * SUBLANES)
            fn(pl.ds(row0, rows_per), pl.ds(ge0, nch * SUBLANES), nch)
            return carry

        if count == 1:
            fn(pl.ds(base, rows_per), pl.ds(base // CHUNK * SUBLANES, nch * SUBLANES), nch)
        else:
            lax.fori_loop(0, count, run, 0)


def _dn_chunk_order(t, d, n_lat, n_ctx):
    if d == 0:
        return jnp.where(t < n_ctx, n_lat + t, t - n_ctx)
    return n_lat + n_ctx - 1 - t


def _dn_fill_intra(q_ref, k_ref, v_ref, bb_s, gc_s, w_s, u_s, ke_s, ge_s, aqk_s, qg_s, d, s, cl):
    dh = LANES

    def fill(rows, ge_rows, nch):
        def load(ref):
            return ref[rows, :].astype(F32).reshape(nch, CHUNK, dh)

        w, u, ke, ge, aqk, qg = _dn_intra(load(q_ref), load(k_ref), load(v_ref), load(bb_s), load(gc_s), d)
        w_s[rows, :] = w.reshape(nch * CHUNK, dh)
        u_s[rows, :] = u.reshape(nch * CHUNK, dh)
        ke_s[rows, :] = ke.reshape(nch * CHUNK, dh)
        qg_s[rows, :] = qg.reshape(nch * CHUNK, dh)
        aqk_s[rows, :] = aqk.reshape(nch * CHUNK, CHUNK)
        ge_s[ge_rows, :] = jnp.broadcast_to(ge, (nch, SUBLANES, dh)).reshape(nch * SUBLANES, dh)

    _for_sub_batches(s, cl, fill)


def _dn_chunk_refs(cid, w_s, u_s, ke_s, ge_s, aqk_s, qg_s):
    rows = pl.ds(pl.multiple_of(cid * CHUNK, CHUNK), CHUNK)
    ge = ge_s[pl.ds(pl.multiple_of(cid * SUBLANES, SUBLANES), SUBLANES), :][0:1]
    return rows, (w_s[rows, :], u_s[rows, :], ke_s[rows, :], ge, aqk_s[rows, :], qg_s[rows, :])


def _dn_scratch(m):
    dh = LANES
    big = pltpu.VMEM((m, dh), F32)
    return [big, big, big, big, big, pltpu.VMEM((m // CHUNK * SUBLANES, dh), F32), pltpu.VMEM((m, CHUNK), F32), big]


def _once(shape, index_map):
    return pl.BlockSpec(shape, index_map, pipeline_mode=pl.Buffered(1))


def dn_fwd(qkv, ba, a_log, dt_bias, side, *, s, cl, heads, name):
    m = s + cl
    dh = LANES
    n_lat, n_ctx = s // CHUNK, cl // CHUNK
    side_in, side_shapes, side_out, side_sems = side.specs()

    def body(*refs):
        q_ref, k_ref, v_ref, ba_ref, al_ref, dt_ref = refs[:6]
        pos = 6 + side.n
        o_ref, st_ref = refs[pos], refs[pos + 1]
        scratch = refs[pos + 2 + side.n:]
        side.run(refs[6:pos], refs[pos + 2:pos + 2 + side.n], scratch[8:], pl.program_id(0), heads - 1,
                 lambda: compute(q_ref, k_ref, v_ref, ba_ref, al_ref, dt_ref, o_ref, st_ref, *scratch[:8]))

    def compute(q_ref, k_ref, v_ref, ba_ref, al_ref, dt_ref, o_ref, st_ref,
                bb_s, gc_s, w_s, u_s, ke_s, ge_s, aqk_s, qg_s):
        h = pl.program_id(0)
        for d in (0, 1):
            beta, g, _, _ = _dn_gates(ba_ref[...], al_ref[...], dt_ref[...], d, h, heads)
            bb_s[...] = jnp.broadcast_to(beta, (m, dh))
            gc_s[...] = _chunk_cumsum(jnp.broadcast_to(g, (m, dh)), d)
            _dn_fill_intra(q_ref, k_ref, v_ref, bb_s, gc_s, w_s, u_s, ke_s, ge_s, aqk_s, qg_s, d, s, cl)

            def step(t, state):
                cid = _dn_chunk_order(t, d, n_lat, n_ctx)
                rows, terms = _dn_chunk_refs(cid, w_s, u_s, ke_s, ge_s, aqk_s, qg_s)
                st_ref[d, pl.ds(pl.multiple_of(cid * dh, dh), dh), :] = state.astype(st_ref.dtype)
                state, o = _dn_step(state, *terms)

                @pl.when(cid < n_lat)
                def _():
                    if d == 0:
                        o_ref[rows, :] = o
                    else:
                        o_ref[rows, :] += o

                return state

            lax.fori_loop(0, n_lat + n_ctx, step, jnp.zeros((dh, dh), F32))

    def col(j0):
        return _once((m, dh), lambda h: (0, j0 + h))

    small = pl.BlockSpec((SUBLANES, LANES), lambda h: (0, 0))
    st_rows = (n_lat + n_ctx) * dh
    res = pl.pallas_call(
        body, name=name, grid=(heads,),
        in_specs=[col(0), col(heads), col(2 * heads), _once((m, LANES), lambda h: (0, 0)), small, small] + side_in,
        out_specs=[pl.BlockSpec((s, dh), lambda h: (0, h)), _once((None, 2, st_rows, dh), lambda h: (h, 0, 0, 0))]
        + side_out,
        out_shape=[jax.ShapeDtypeStruct((s, heads * dh), F32),
                   jax.ShapeDtypeStruct((heads, 2, st_rows, dh), ACT_DTYPE)] + side_shapes,
        scratch_shapes=_dn_scratch(m) + side_sems,
        compiler_params=_params("arbitrary", vmem=BIG_KERNEL_VMEM))(qkv, qkv, qkv, ba, a_log, dt_bias, *side.srcs)
    return res[0], res[1], res[2:]


def dn_out_fwd(o, proj, dn_norm, *, heads, gate_col0, name):
    s = o.shape[0]
    dh = LANES
    tr = _tile(s, 1024, 16)

    def body(o_ref, g_ref, nw_ref, y_ref):
        y_ref[...] = (_rms(o_ref[...])[0] * nw_ref[...] * _silu(g_ref[...])).astype(y_ref.dtype)

    blk = pl.BlockSpec((tr, dh), lambda i, h: (i, h))
    return pl.pallas_call(
        body, name=name, grid=(s // tr, heads),
        in_specs=[blk, pl.BlockSpec((tr, dh), lambda i, h: (i, gate_col0 + h)), pl.BlockSpec((1, dh), lambda i, h: (0, 0))],
        out_specs=blk, out_shape=jax.ShapeDtypeStruct((s, heads * dh), ACT_DTYPE),
        compiler_params=_params("parallel", "parallel"))(o, proj, dn_norm)


def dn_out_bwd(o, proj, d_y, dn_norm, *, heads, gate_col0, name):
    s = o.shape[0]
    dh = LANES
    tr = _tile(s, 1024, 16)

    def body(o_ref, g_ref, dy_ref, nw_ref, do_ref, dg_ref, dnw_ref):
        @pl.when((pl.program_id(0) == 0) & (pl.program_id(1) == 0))
        def _():
            dnw_ref[...] = jnp.zeros_like(dnw_ref)

        nrm, rs = _rms(o_ref[...])
        nw, gate, dy = nw_ref[...], g_ref[...], dy_ref[...]
        dg_ref[...] = (dy * (nrm * nw) * _dsilu(gate)).astype(dg_ref.dtype)
        dy0 = dy * _silu(gate)
        dnw_ref[0:1, :] += jnp.sum(dy0 * nrm, axis=0, keepdims=True)
        dn = dy0 * nw
        do_ref[...] = rs * (dn - nrm * jnp.mean(dn * nrm, axis=-1, keepdims=True))

    blk = pl.BlockSpec((tr, dh), lambda i, h: (i, h))
    return pl.pallas_call(
        body, name=name, grid=(s // tr, heads),
        in_specs=[blk, pl.BlockSpec((tr, dh), lambda i, h: (i, gate_col0 + h)), blk,
                  pl.BlockSpec((1, dh), lambda i, h: (0, 0))],
        out_specs=[blk, blk, pl.BlockSpec((SUBLANES, LANES), lambda i, h: (0, 0))],
        out_shape=[jax.ShapeDtypeStruct((s, heads * dh), F32), jax.ShapeDtypeStruct((s, heads * dh), ACT_DTYPE),
                   jax.ShapeDtypeStruct((SUBLANES, LANES), F32)],
        compiler_params=_params("arbitrary", "arbitrary"))(o, proj, d_y, dn_norm)


def dn_bwd(qkv, ba, d_o, states, a_log, dt_bias, side, *, s, cl, heads, name):
    m = s + cl
    dh = LANES
    n_lat, n_ctx = s // CHUNK, cl // CHUNK
    n_all = n_lat + n_ctx
    side_in, side_shapes, side_out, side_sems = side.specs()

    def body(*refs):
        ins, pos = refs[:8], 8 + side.n
        outs = refs[pos:pos + 4]
        scratch = refs[pos + 4 + side.n:]
        side.run(refs[8:pos], refs[pos + 4:pos + 4 + side.n], scratch[8:], pl.program_id(0), heads - 1,
                 lambda: compute(*ins, *outs, *scratch[:8]))

    def compute(q_ref, k_ref, v_ref, ba_ref, do_ref, st_ref, al_ref, dt_ref, dqkv_ref, dba_ref, dal_ref, ddt_ref,
                bb_s, gc_s, w_s, u_s, ke_s, ge_s, aqk_s, qg_s):
        h = pl.program_id(0)
        dq_ref, dk_ref, dv_ref = dqkv_ref.at[0], dqkv_ref.at[1], dqkv_ref.at[2]

        @pl.when(h == 0)
        def _():
            dba_ref[...] = jnp.zeros_like(dba_ref)

        lane = _iota((m, LANES), 1)
        for d in (0, 1):
            beta, g, pre, a_neg = _dn_gates(ba_ref[...], al_ref[...], dt_ref[...], d, h, heads)
            bb_s[...] = jnp.broadcast_to(beta, (m, dh))
            gc_s[...] = _chunk_cumsum(jnp.broadcast_to(g, (m, dh)), d)
            _dn_fill_intra(q_ref, k_ref, v_ref, bb_s, gc_s, w_s, u_s, ke_s, ge_s, aqk_s, qg_s, d, s, cl)

            def bwd_step(i, dstate):
                cid = _dn_chunk_order(n_all - 1 - i, d, n_lat, n_ctx)
                rows, terms = _dn_chunk_refs(cid, w_s, u_s, ke_s, ge_s, aqk_s, qg_s)
                state = st_ref[d, pl.ds(pl.multiple_of(cid * dh, dh), dh), :].astype(F32)
                _, vjp = jax.vjp(_dn_step, state, *terms)
                lat_rows = pl.ds(pl.multiple_of(jnp.minimum(cid, n_lat - 1) * CHUNK, CHUNK), CHUNK)
                do = jnp.where(cid < n_lat, do_ref[lat_rows, :], 0.0)
                dstate, dw, du, dke, dge, daqk, dqg = vjp((dstate, do))
                w_s[rows, :] = dw
                u_s[rows, :] = du
                ke_s[rows, :] = dke
                qg_s[rows, :] = dqg
                aqk_s[rows, :] = daqk
                ge_s[pl.ds(pl.multiple_of(cid * SUBLANES, SUBLANES), SUBLANES), :] = jnp.broadcast_to(
                    dge, (SUBLANES, dh))
                return dstate

            lax.fori_loop(0, n_all, bwd_step, jnp.zeros((dh, dh), F32))

            def intra_bwd(rows, ge_rows, nch):
                def load(ref, width=dh):
                    return ref[rows, :].astype(F32).reshape(nch, CHUNK, width)

                _, vjp = jax.vjp(functools.partial(_dn_intra, d=d), load(q_ref), load(k_ref), load(v_ref),
                                 load(bb_s), load(gc_s))
                dge = ge_s[ge_rows, :].reshape(nch, SUBLANES, dh)[:, 0:1]
                dq, dk, dv, dbb, dgc = vjp((load(w_s), load(u_s), load(ke_s), dge, load(aqk_s, CHUNK), load(qg_s)))
                flat = lambda x: x.reshape(nch * CHUNK, dh)
                if d == 0:
                    dq_ref[rows, :], dk_ref[rows, :], dv_ref[rows, :] = flat(dq), flat(dk), flat(dv)
                else:
                    dq_ref[rows, :] += flat(dq)
                    dk_ref[rows, :] += flat(dk)
                    dv_ref[rows, :] += flat(dv)
                bb_s[rows, :] = flat(dbb)
                gc_s[rows, :] = flat(dgc)

            _for_sub_batches(s, cl, intra_bwd, _DN_SUB_BWD)

            dbeta = jnp.sum(bb_s[...], axis=1, keepdims=True)
            dg = jnp.sum(_chunk_cumsum(gc_s[...], 1 - d), axis=1, keepdims=True)
            dbraw = dbeta * beta * (1.0 - beta)
            dpre = dg * a_neg * jax.nn.sigmoid(pre)
            dba_ref[...] += (jnp.where(lane == d * heads + h, dbraw, 0.0)
                             + jnp.where(lane == (2 + d) * heads + h, dpre, 0.0))
            dal_ref[d:d + 1, :] = jnp.broadcast_to(jnp.sum(dg * g, axis=0, keepdims=True), (1, LANES))
            ddt_ref[d:d + 1, :] = jnp.broadcast_to(jnp.sum(dpre, axis=0, keepdims=True), (1, LANES))
        dal_ref[2:SUBLANES, :] = jnp.zeros((SUBLANES - 2, LANES), F32)
        ddt_ref[2:SUBLANES, :] = jnp.zeros((SUBLANES - 2, LANES), F32)

    def col(j0, rows=m):
        return _once((rows, dh), lambda h: (0, j0 + h))

    small = pl.BlockSpec((SUBLANES, LANES), lambda h: (0, 0))
    tile_h = pl.BlockSpec((None, SUBLANES, LANES), lambda h: (h, 0, 0))
    tiles = jax.ShapeDtypeStruct((heads, SUBLANES, LANES), F32)
    res = pl.pallas_call(
        body, name=name, grid=(heads,),
        in_specs=[col(0), col(heads), col(2 * heads), _once((m, LANES), lambda h: (0, 0)), col(0, s),
                  _once((None, 2, n_all * dh, dh), lambda h: (h, 0, 0, 0)), small, small] + side_in,
        out_specs=[_once((3, m, dh), lambda h: (0, 0, h)), _once((m, LANES), lambda h: (0, 0)), tile_h, tile_h]
        + side_out,
        out_shape=[jax.ShapeDtypeStruct((3, m, heads * dh), F32), jax.ShapeDtypeStruct((m, LANES), F32), tiles, tiles]
        + side_shapes,
        scratch_shapes=_dn_scratch(m) + side_sems,
        compiler_params=_params("arbitrary", vmem=BIG_KERNEL_VMEM))(qkv, qkv, qkv, ba, d_o, states, a_log, dt_bias,
                                                                     *side.srcs)
    return res[0], res[1], res[2], res[3], res[4:]


def _lin_scan(a, b, d):
    rows = a.shape[0]
    t = _iota((rows, 1), 0)
    step = 1
    while step < rows:
        if d == 0:
            ok, sa, sb = t >= step, pltpu.roll(a, step, 0), pltpu.roll(b, step, 0)
        else:
            ok, sa, sb = t < rows - step, pltpu.roll(a, rows - step, 0), pltpu.roll(b, rows - step, 0)
        b = b + a * jnp.where(ok, sb, 0.0)
        a = a * jnp.where(ok, sa, 1.0)
        step *= 2
    return b


def _lru_gates(xc, w_r, b_r, w_i, b_i, lam):
    r = jax.nn.sigmoid(_dot(xc, w_r, _NN) + b_r)
    i = jax.nn.sigmoid(_dot(xc, w_i, _NN) + b_i)
    log_a = -LRU_C * r * jax.nn.softplus(-lam)
    z = 2.0 * log_a
    series = -(z * (1.0 + z * (0.5 + z * (1.0 / 6.0))))
    one_minus = jnp.where(z > -0.01, series, 1.0 - jnp.exp(z))
    return jnp.exp(log_a), jnp.sqrt(one_minus) * (i * xc)


def _lru_states(a, b, d, s, cl):
    ac, bc = a[s:], b[s:]
    hc = _lin_scan(ac, bc, d)
    h0 = hc[cl - 1:cl] if d == 0 else hc[0:1]
    first = 0 if d == 0 else s - 1
    al = a[:s]
    bl = b[:s] + jnp.where(_iota((s, 1), 0) == first, al * h0, 0.0)
    return _lin_scan(al, bl, d), hc, h0


def _lru_specs(m, nb_dim):
    j_col = lambda rows: pl.BlockSpec((rows, LANES), lambda j: (0, j))
    w_blk = pl.BlockSpec((2, None, nb_dim, nb_dim), lambda j: (0, j, 0, 0))
    return j_col, w_blk


def lru_fwd(xcm, conv_w, conv_b, w_r, b_r, w_i, b_i, lam, *, s, cl, name):
    m, width = xcm.shape
    j_col, w_blk = _lru_specs(m, w_r.shape[-1])

    def body(x_ref, cw_ref, cb_ref, wr_ref, br_ref, wi_ref, bi_ref, lam_ref, h_ref):
        xc = _conv(x_ref[...], cw_ref[...], DN_CONV_OFFSETS, (s, cl)) + cb_ref[...]
        for d in (0, 1):
            a, b = _lru_gates(xc, wr_ref[d], br_ref[d:d + 1, :], wi_ref[d], bi_ref[d:d + 1, :], lam_ref[d:d + 1, :])
            h = _lru_states(a, b, d, s, cl)[0]
            if d == 0:
                h_ref[...] = h
            else:
                h_ref[...] += h

    return pl.pallas_call(
        body, name=name, grid=(width // LANES,),
        in_specs=[_once((m, LANES), lambda j: (0, j)), j_col(len(DN_CONV_OFFSETS)), j_col(1), w_blk, j_col(2), w_blk,
                  j_col(2), j_col(2)],
        out_specs=j_col(s), out_shape=jax.ShapeDtypeStruct((s, width), F32),
        compiler_params=_params("parallel", vmem=BIG_KERNEL_VMEM))(xcm, conv_w, conv_b, w_r, b_r, w_i, b_i, lam)


def lru_bwd(xcm, d_h, conv_w, conv_b, w_r, b_r, w_i, b_i, lam, *, s, cl, name):
    m, width = xcm.shape
    nb_dim = w_r.shape[-1]
    j_col, w_blk = _lru_specs(m, nb_dim)

    def body(x_ref, dh_ref, cw_ref, cb_ref, wr_ref, br_ref, wi_ref, bi_ref, lam_ref,
             dx_ref, dcw_ref, dcb_ref, dwr_ref, dbr_ref, dwi_ref, dbi_ref, dlam_ref):
        x, cw = x_ref[...], cw_ref[...]
        xc = _conv(x, cw, DN_CONV_OFFSETS, (s, cl)) + cb_ref[...]
        d_hl = dh_ref[...]
        d_xc = None
        for d in (0, 1):
            (a, b), vjp = jax.vjp(_lru_gates, xc, wr_ref[d], br_ref[d:d + 1, :], wi_ref[d], bi_ref[d:d + 1, :],
                                  lam_ref[d:d + 1, :])
            h, hc, h0 = _lru_states(a, b, d, s, cl)
            nxt = 1 if d == 0 else -1
            first = 0 if d == 0 else s - 1
            al, ac = a[:s], a[s:]
            lam_l = _lin_scan(_shift(al, nxt, (s, 0)), d_hl, 1 - d)
            h_prev = _shift(h, -nxt, (s, 0)) + jnp.where(_iota((s, 1), 0) == first, h0, 0.0)
            d_h0 = (al * lam_l)[first:first + 1]
            last_c = cl - 1 if d == 0 else 0
            d_hc = jnp.where(_iota((cl, 1), 0) == last_c, d_h0, 0.0)
            lam_c = _lin_scan(_shift(ac, nxt, (cl, 0)), d_hc, 1 - d)
            da = jnp.concatenate([lam_l * h_prev, lam_c * _shift(hc, -nxt, (cl, 0))], axis=0)
            db = jnp.concatenate([lam_l, lam_c], axis=0)
            g_xc, g_wr, g_br, g_wi, g_bi, g_lam = vjp((da, db))
            d_xc = g_xc if d_xc is None else d_xc + g_xc
            dwr_ref[d], dwi_ref[d] = g_wr, g_wi
            dbr_ref[d:d + 1, :], dbi_ref[d:d + 1, :], dlam_ref[d:d + 1, :] = g_br, g_bi, g_lam
        dx, dcw = _conv_bwd(x, cw, d_xc, DN_CONV_OFFSETS, (s, cl))
        dx_ref[...] = dx.astype(dx_ref.dtype)
        dcw_ref[...] = dcw
        dcb_ref[...] = jnp.sum(d_xc, axis=0, keepdims=True)

    kk = len(DN_CONV_OFFSETS)
    vec2 = jax.ShapeDtypeStruct((2, width), F32)
    return pl.pallas_call(
        body, name=name, grid=(width // LANES,),
        in_specs=[_once((m, LANES), lambda j: (0, j)), _once((s, LANES), lambda j: (0, j)), j_col(kk), j_col(1),
                  w_blk, j_col(2), w_blk, j_col(2), j_col(2)],
        out_specs=[j_col(m), j_col(kk), j_col(1), w_blk, j_col(2), w_blk, j_col(2), j_col(2)],
        out_shape=[jax.ShapeDtypeStruct((m, width), ACT_DTYPE), jax.ShapeDtypeStruct((kk, width), F32),
                   jax.ShapeDtypeStruct((1, width), F32), jax.ShapeDtypeStruct(w_r.shape, F32), vec2,
                   jax.ShapeDtypeStruct(w_i.shape, F32), vec2, vec2],
        compiler_params=_params("parallel", vmem=BIG_KERNEL_VMEM))(xcm, d_h, conv_w, conv_b, w_r, b_r, w_i, b_i, lam)


def lru_gate_fwd(h, proj, *, gate_col0, name):
    s, width = h.shape
    tr, tc = _tile(s, 512, 16), _tile(width, 512, LANES)
    c0 = gate_col0 * LANES // tc

    def body(h_ref, g_ref, y_ref):
        y_ref[...] = (h_ref[...] * _silu(g_ref[...])).astype(y_ref.dtype)

    blk = pl.BlockSpec((tr, tc), lambda i, j: (i, j))
    return pl.pallas_call(
        body, name=name, grid=(s // tr, width // tc),
        in_specs=[blk, pl.BlockSpec((tr, tc), lambda i, j: (i, c0 + j))], out_specs=blk,
        out_shape=jax.ShapeDtypeStruct((s, width), ACT_DTYPE),
        compiler_params=_params("parallel", "parallel"))(h, proj)


def lru_gate_bwd(h, proj, d_y, *, gate_col0, dy_col0, name):
    s, width = h.shape
    tr, tc = _tile(s, 512, 16), _tile(width, 512, LANES)
    c0, y0 = gate_col0 * LANES // tc, dy_col0 * LANES // tc

    def body(h_ref, g_ref, dy_ref, dh_ref, dg_ref):
        g, dy = g_ref[...], dy_ref[...]
        dh_ref[...] = dy * _silu(g)
        dg_ref[...] = (dy * h_ref[...] * _dsilu(g)).astype(dg_ref.dtype)

    blk = pl.BlockSpec((tr, tc), lambda i, j: (i, j))
    return pl.pallas_call(
        body, name=name, grid=(s // tr, width // tc),
        in_specs=[blk, pl.BlockSpec((tr, tc), lambda i, j: (i, c0 + j)), pl.BlockSpec((tr, tc), lambda i, j: (i, y0 + j))],
        out_specs=[blk, blk],
        out_shape=[jax.ShapeDtypeStruct((s, width), F32), jax.ShapeDtypeStruct((s, width), ACT_DTYPE)],
        compiler_params=_params("parallel", "parallel"))(h, proj, d_y)


def _sc_parts(p, width):
    return [p[:, k * width:(k + 1) * width] for k in range(4)]


def sc_mix_fwd(p, conv_w, *, name):
    s, width = p.shape[0], conv_w.shape[1]
    tr = 2 * GRID_W

    def body(p_ref, w_ref, y_ref):
        b_g, c_g, x_in, gate = _sc_parts(p_ref[...], width)
        z = _conv(c_g * x_in, w_ref[...], SC_CONV_OFFSETS, (GRID_W, 0))
        y_ref[...] = (b_g * z * _silu(gate)).astype(y_ref.dtype)

    return pl.pallas_call(
        body, name=name, grid=(s // tr,),
        in_specs=[pl.BlockSpec((tr, 4 * width), lambda i: (i, 0)), pl.BlockSpec(conv_w.shape, lambda i: (0, 0))],
        out_specs=pl.BlockSpec((tr, width), lambda i: (i, 0)),
        out_shape=jax.ShapeDtypeStruct((s, width), ACT_DTYPE),
        compiler_params=_params("parallel"))(p, conv_w)


def sc_mix_bwd(p, d_y, conv_w, *, name):
    s, width = p.shape[0], conv_w.shape[1]
    tr = 2 * GRID_W

    def body(p_ref, dy_ref, w_ref, dp_ref, dw_ref):
        @pl.when(pl.program_id(0) == 0)
        def _():
            dw_ref[...] = jnp.zeros_like(dw_ref)

        b_g, c_g, x_in, gate = _sc_parts(p_ref[...], width)
        w, dy = w_ref[...], dy_ref[...]
        u = c_g * x_in
        z = _conv(u, w, SC_CONV_OFFSETS, (GRID_W, 0))
        sg = _silu(gate)
        du, dw = _conv_bwd(u, w, dy * b_g * sg, SC_CONV_OFFSETS, (GRID_W, 0))
        dw_ref[...] += dw
        parts = (dy * z * sg, du * x_in, du * c_g, dy * b_g * z * _dsilu(gate))
        for k, part in enumerate(parts):
            dp_ref[:, k * width:(k + 1) * width] = part.astype(dp_ref.dtype)

    return pl.pallas_call(
        body, name=name, grid=(s // tr,),
        in_specs=[pl.BlockSpec((tr, 4 * width), lambda i: (i, 0)), pl.BlockSpec((tr, width), lambda i: (i, 0)),
                  pl.BlockSpec(conv_w.shape, lambda i: (0, 0))],
        out_specs=[pl.BlockSpec((tr, 4 * width), lambda i: (i, 0)), pl.BlockSpec(conv_w.shape, lambda i: (0, 0))],
        out_shape=[jax.ShapeDtypeStruct((s, 4 * width), ACT_DTYPE), jax.ShapeDtypeStruct(conv_w.shape, F32)],
        compiler_params=_params("arbitrary"))(p, d_y, conv_w)


MOD_ROWS = 16


def mod_fwd(cond, mod_w, mod_b, *, name):
    nl, d, ns = mod_w.shape
    tn = _tile(ns, 512, LANES)

    def body(c_ref, w_ref, b_ref, o_ref):
        o_ref[...] = _dot(_silu(c_ref[...]), w_ref[...], _NN) + b_ref[...]

    return pl.pallas_call(
        body, name=name, grid=(nl, ns // tn),
        in_specs=[pl.BlockSpec((MOD_ROWS, d), lambda l, j: (0, 0)), pl.BlockSpec((None, d, tn), lambda l, j: (l, 0, j)),
                  pl.BlockSpec((None, 1, tn), lambda l, j: (l, 0, j))],
        out_specs=pl.BlockSpec((None, MOD_ROWS, tn), lambda l, j: (l, 0, j)),
        out_shape=jax.ShapeDtypeStruct((nl, MOD_ROWS, ns), F32),
        compiler_params=_params("parallel", "parallel"))(cond, mod_w, mod_b)


def _adamw(w, g, m, v):
    m = ADAM_B1 * m + (1.0 - ADAM_B1) * g
    v = ADAM_B2 * v + (1.0 - ADAM_B2) * (g * g)
    m_hat = m / (1.0 - ADAM_B1 ** ADAM_STEP)
    v_hat = v / (1.0 - ADAM_B2 ** ADAM_STEP)
    return -ADAM_LR * (m_hat / (jnp.sqrt(v_hat) + ADAM_EPS) + ADAM_WD * w), m, v


def mod_adam(cond, d_mod, w, m, v, *, name):
    nl, d, ns = w.shape
    tr, tn = _tile(d, 256, SUBLANES), _tile(ns, 1024, LANES)

    def body(c_ref, dm_ref, w_ref, m_ref, v_ref, g_ref, dl_ref, nm_ref, nv_ref, ds_ref):
        @pl.when(pl.program_id(2) == 0)
        def _():
            ds_ref[...] = jnp.zeros_like(ds_ref)

        wv, dm = w_ref[...], dm_ref[...]
        g = _dot(_silu(c_ref[...]), dm, _TN)
        ds_ref[...] += _dot(dm, wv, _NT)
        g_ref[...] = g
        dl_ref[...], nm_ref[...], nv_ref[...] = _adamw(wv, g, m_ref[...], v_ref[...])

    blk = pl.BlockSpec((None, tr, tn), lambda l, i, j: (l, i, j))
    full = jax.ShapeDtypeStruct(w.shape, F32)
    return pl.pallas_call(
        body, name=name, grid=(nl, d // tr, ns // tn),
        in_specs=[pl.BlockSpec((MOD_ROWS, tr), lambda l, i, j: (0, i)),
                  pl.BlockSpec((None, MOD_ROWS, tn), lambda l, i, j: (l, 0, j)), blk, blk, blk],
        out_specs=[blk, blk, blk, blk, pl.BlockSpec((None, MOD_ROWS, tr), lambda l, i, j: (l, 0, i))],
        out_shape=[full, full, full, full, jax.ShapeDtypeStruct((nl, MOD_ROWS, d), F32)],
        compiler_params=_params("parallel", "parallel", "arbitrary"))(cond, d_mod, w, m, v)


def _row_tile(rows, cols, itemsize, mult):
    return _tile(rows, max(mult, (2 << 20) // (cols * itemsize)), mult)


def adam_update(w, g, m, v, *, name):
    r, c = w.shape
    tr = _row_tile(r, c, 4, SUBLANES)

    def body(w_ref, g_ref, m_ref, v_ref, dl_ref, nm_ref, nv_ref):
        dl_ref[...], nm_ref[...], nv_ref[...] = _adamw(w_ref[...], g_ref[...], m_ref[...], v_ref[...])

    blk = pl.BlockSpec((tr, c), lambda i: (i, 0))
    return pl.pallas_call(
        body, name=name, grid=(r // tr,), in_specs=[blk] * 4, out_specs=[blk] * 3,
        out_shape=[jax.ShapeDtypeStruct((r, c), F32)] * 3, compiler_params=_params("parallel"))(w, g, m, v)


def cast_rows(x, dtype, *, name):
    r, c = x.shape
    tr = _row_tile(r, c, 4, 16)

    def body(x_ref, o_ref):
        o_ref[...] = x_ref[...].astype(dtype)

    blk = pl.BlockSpec((tr, c), lambda i: (i, 0))
    return pl.pallas_call(body, name=name, grid=(r // tr,), in_specs=[blk], out_specs=blk,
                          out_shape=jax.ShapeDtypeStruct((r, c), dtype), compiler_params=_params("parallel"))(x)


def add_sibling_half(core, mine, other, *, name):
    a, _, r, c = mine.shape
    tr = _row_tile(r, c, 4, 16)

    def body(core_ref, x_ref, y_ref, o_ref):
        o_ref[...] = (x_ref[...].astype(F32) + y_ref[...].astype(F32)).astype(o_ref.dtype)

    grid_spec = pltpu.PrefetchScalarGridSpec(
        num_scalar_prefetch=1, grid=(a, r // tr),
        in_specs=[pl.BlockSpec((None, None, tr, c), lambda k, i, core_ref: (k, core_ref[0], i, 0)),
                  pl.BlockSpec((None, tr, c), lambda k, i, core_ref: (k, i, 0))],
        out_specs=pl.BlockSpec((None, tr, c), lambda k, i, core_ref: (k, i, 0)))
    return pl.pallas_call(body, name=name, grid_spec=grid_spec, out_shape=jax.ShapeDtypeStruct((a, r, c), WIRE_DTYPE),
                          compiler_params=_params("parallel", "parallel"))(core, mine, other)


def sum_slots(x, *, name):
    n, r, c = x.shape
    tr = _row_tile(r, c * n, 4, 16)

    def body(x_ref, o_ref):
        acc = x_ref[0].astype(F32)
        for k in range(1, n):
            acc = acc + x_ref[k].astype(F32)
        o_ref[...] = acc

    return pl.pallas_call(
        body, name=name, grid=(r // tr,), in_specs=[pl.BlockSpec((n, tr, c), lambda i: (0, i, 0))],
        out_specs=pl.BlockSpec((tr, c), lambda i: (i, 0)), out_shape=jax.ShapeDtypeStruct((r, c), F32),
        compiler_params=_params("parallel"))(x)


def ctx_cond_grad(parts, c_ctx, *, name):
    def body(p_ref, c_ref, o_ref):
        acc = p_ref[0]
        for k in range(1, N_CHIPS):
            acc = acc + p_ref[k]
        o_ref[...] = acc * _dsilu(c_ref[...])

    return pl.pallas_call(body, name=name, out_shape=jax.ShapeDtypeStruct(c_ctx.shape, F32))(parts, c_ctx)


PACK_ROWS = 256


def _tile_rows(shape):
    n = 1
    for dim in shape:
        n *= dim
    return -(-n // (SUBLANES * LANES)) * SUBLANES


def _pack(arrs):
    parts = []
    for a in arrs:
        flat = a.reshape(-1).astype(F32)
        rows = _tile_rows(a.shape)
        parts.append(jnp.pad(flat, (0, rows * LANES - flat.shape[0])).reshape(rows, LANES))
    total = sum(t.shape[0] for t in parts)
    parts.append(jnp.zeros((-total % PACK_ROWS, LANES), F32))
    return jnp.concatenate(parts, axis=0)


def _unpack(flat, shapes):
    lead = flat.shape[:-2]
    outs, r0 = [], 0
    for shape in shapes:
        rows = _tile_rows(shape)
        n = 1
        for dim in shape:
            n *= dim
        piece = flat[..., r0:r0 + rows, :].reshape(lead + (rows * LANES,))[..., :n]
        outs.append(piece.reshape(lead + tuple(shape)))
        r0 += rows
    return outs


WEIGHTS = ('c_ctx', 'mod_w', 'mod_b', 'norm_w', 'ab_w_in', 'ab_qkv_conv', 'ab_a_log', 'ab_dt_bias', 'ab_dn_norm',
           'ab_lru_conv_w', 'ab_lru_conv_b', 'ab_lru_w_r', 'ab_lru_b_r', 'ab_lru_w_i', 'ab_lru_b_i', 'ab_lru_lambda',
           'ab_w_out', 'sc_w_in', 'sc_conv', 'sc_w_out', 'final_norm_w')
BIG_WEIGHTS = ('ab_w_in', 'ab_w_out', 'sc_w_in', 'sc_w_out')


def kernel(x, c, ctx, c_ctx, mod_w, mod_b, norm_w, ab_w_in, ab_qkv_conv, ab_a_log, ab_dt_bias, ab_dn_norm, ab_lru_conv_w, ab_lru_conv_b, ab_lru_w_r, ab_lru_b_r, ab_lru_w_i, ab_lru_b_i, ab_lru_lambda, ab_w_out, sc_w_in, sc_conv, sc_w_out, final_norm_w, loss_target, m_c_ctx, m_mod_w, m_mod_b, m_norm_w, m_ab_w_in, m_ab_qkv_conv, m_ab_a_log, m_ab_dt_bias, m_ab_dn_norm, m_ab_lru_conv_w, m_ab_lru_conv_b, m_ab_lru_w_r, m_ab_lru_b_r, m_ab_lru_w_i, m_ab_lru_b_i, m_ab_lru_lambda, m_ab_w_out, m_sc_w_in, m_sc_conv, m_sc_w_out, m_final_norm_w, v_c_ctx, v_mod_w, v_mod_b, v_norm_w, v_ab_w_in, v_ab_qkv_conv, v_ab_a_log, v_ab_dt_bias, v_ab_dn_norm, v_ab_lru_conv_w, v_ab_lru_conv_b, v_ab_lru_w_r, v_ab_lru_b_r, v_ab_lru_w_i, v_ab_lru_b_i, v_ab_lru_lambda, v_ab_w_out, v_sc_w_in, v_sc_conv, v_sc_w_out, v_final_norm_w):
    given = dict(locals())
    weights = {n: given[n] for n in WEIGHTS}
    mom1 = {n: given['m_' + n] for n in WEIGHTS}
    mom2 = {n: given['v_' + n] for n in WEIGHTS}

    xi, yi, ci = lax.axis_index("x"), lax.axis_index("y"), lax.axis_index("c")
    chip = 2 * xi + yi
    dev = 2 * chip + ci
    x2d, ctx2d, target = x[0], ctx[0], loss_target[0]
    s, d = x2d.shape
    cl = ctx2d.shape[0]
    heads = ab_a_log.shape[-1]
    wdn = heads * LANES
    nb = ab_lru_w_r.shape[2]
    wl = nb * ab_lru_w_r.shape[3]
    sc = sc_w_in.shape[-1]
    ab_out = wdn + wl
    off_beta = 3 * wdn + wl
    ab_state = off_beta + 4 * heads
    ab_in = ab_state + wdn + wl
    ni = ab_in // N_CHIPS
    ns = mod_w.shape[-1]
    grid_rows = s // GRID_W

    def to_col_major(t):
        return t.reshape(grid_rows, GRID_W, t.shape[-1]).swapaxes(0, 1).reshape(s, t.shape[-1])

    def to_raster(t):
        return t.reshape(GRID_W, grid_rows, t.shape[-1]).swapaxes(0, 1).reshape(s, t.shape[-1])

    def from_chips(t):
        return jnp.moveaxis(t[0::N_CORES], 0, 1).reshape(t.shape[1], -1)

    def own_columns(t, width):
        return lax.dynamic_slice_in_dim(t, chip * width, width, axis=t.ndim - 1)

    small_shards = [c[0], ab_qkv_conv[0], ab_lru_conv_w[0], ab_lru_b_r[0], ab_lru_b_i[0], ab_lru_lambda[0], sc_conv[0]]
    gathered0 = all_gather_devices(_pack(small_shards), "ag_small_params")
    c_all, qkv_sh, lcw_sh, lbr_sh, lbi_sh, llam_sh, scv_sh = _unpack(gathered0, [t.shape for t in small_shards])
    qkv_conv, lru_conv_w, sc_conv_w = from_chips(qkv_sh), from_chips(lcw_sh), from_chips(scv_sh)
    lru_b_r, lru_b_i, lru_lam = from_chips(lbr_sh), from_chips(lbi_sh), from_chips(llam_sh)

    shards = [weights[n][0] for n in BIG_WEIGHTS]
    halves = [cast_rows(t, WIRE_DTYPE, name=f"cast_w{k}").reshape(N_CORES, t.shape[0] // N_CORES, t.shape[1])
              for k, t in enumerate(shards)]
    from_chips_half = chip_exchange(halves[:1], gather=True, name="ag_w_chips")
    from_sibling_half = sibling_exchange(from_chips_half, pick=False, name="ag_w_cores")
    w_in_full = core_halves(ci, from_chips_half[0], from_sibling_half[0])
    w_in_full = jnp.moveaxis(w_in_full.reshape(N_CHIPS, d, ni), 0, 1).reshape(d, ab_in)
    w_main = jnp.concatenate([w_in_full[:, :off_beta], w_in_full[:, ab_state:]], axis=1)
    w_ba = jnp.pad(w_in_full[:, off_beta:ab_state], ((0, 0), (0, LANES - 4 * heads)))

    cond = jnp.zeros((MOD_ROWS, d), F32).at[:N_DEV].set(c_all).at[N_DEV].set(c_ctx)
    mod_shard = mod_fwd(cond, mod_w, own_columns(mod_b, ns)[:, None, :], name="mod_fwd")
    gathered_mod = all_gather_devices(mod_shard.reshape(-1, LANES), "ag_mod")
    mod_all = jnp.moveaxis(gathered_mod[0::N_CORES].reshape(N_CHIPS, 2, MOD_ROWS, ns), 0, 2).reshape(2, MOD_ROWS, 3 * d)
    own_mod = lax.dynamic_index_in_dim(mod_all, dev, axis=1, keepdims=False)
    shift, scale, gate = own_mod[:, :d], own_mod[:, d:2 * d], own_mod[:, 2 * d:]
    shift_c, scale_c = mod_all[0, N_DEV, :d], mod_all[0, N_DEV, d:2 * d]

    def pair(a, b):
        return jnp.stack([a, b])[:, None, :]

    hn_all = norm_mod_fwd(x2d, ctx2d, norm_w[0:1], pair(scale[0], scale_c), pair(shift[0], shift_c), name="norm0_fwd")
    proj = mm_nn(hn_all, w_main, out_dtype=F32, name="proj0")
    ba = mm_nn(hn_all, w_ba, out_dtype=F32, name="proj0_ba")
    qkv = qkv_conv_fwd(proj, qkv_conv, s=s, cl=cl, heads=heads, name="qkv_conv_fwd")

    def pad_dh(t):
        return jnp.zeros((SUBLANES, LANES), F32).at[:2, :heads].set(t)

    a_log_t, dt_bias_t = pad_dh(ab_a_log[0]), pad_dh(ab_dt_bias[0])
    gate_dn0 = (3 * wdn + wl) // LANES
    gate_lru0 = gate_dn0 + wdn // LANES
    o_dn, dn_states, late_from_chips = dn_fwd(qkv, ba, a_log_t, dt_bias_t, SideExchange(halves[1:], gather=True), s=s,
                                              cl=cl, heads=heads, name="dn_fwd")
    ab_out_sibling = sibling_exchange(list(late_from_chips[:1]), pick=False, name="ag_w_cores_late")
    w_ab_out = core_halves(ci, late_from_chips[0], ab_out_sibling[0]).reshape(ab_out, d)
    y_dn = dn_out_fwd(o_dn, proj, ab_dn_norm, heads=heads, gate_col0=gate_dn0, name="dn_out_fwd")
    lru_in = proj[:, 3 * wdn:3 * wdn + wl]
    xcm = jnp.concatenate([to_col_major(lru_in[:s]), lru_in[s:]], axis=0)
    lru_w = (lru_conv_w, ab_lru_conv_b, ab_lru_w_r[0], lru_b_r, ab_lru_w_i[0], lru_b_i, lru_lam)
    h_lru = to_raster(lru_fwd(xcm, *lru_w, s=s, cl=cl, name="lru_fwd"))
    y_lru = lru_gate_fwd(h_lru, proj, gate_col0=gate_lru0, name="lru_gate_fwd")
    y_ab = jnp.concatenate([y_dn, y_lru], axis=1)
    (x1, yo0), sc_sibling = mm_nn(y_ab, w_ab_out, out_dtype=F32, name="out0", resid=x2d, gate=gate[0:1],
                                  aux_dtype=ACT_DTYPE, tn_cap=512, side=SiblingSide(late_from_chips[1:], pick=False))
    w_sc_in, w_sc_out = (core_halves(ci, a, b) for a, b in zip(late_from_chips[1:], sc_sibling))
    w_sc_in, w_sc_out = w_sc_in.reshape(N_CHIPS, d, sc), w_sc_out.reshape(sc, d)

    hn1 = norm_mod_fwd(x1, None, norm_w[1:2], scale[1][None, None, :], shift[1][None, None, :], name="norm1_fwd")
    tm1, tn1, tk1 = _tile(s, 1024, 16), _tile(sc, 1024, LANES), _tile(d, MATMUL_TK, LANES)
    p = matmul(hn1, w_sc_in, m=s, n=4 * sc, k=d, tm=tm1, tn=tn1, tk=tk1, dims=_NN, out_dtype=F32, name="proj1",
               a_spec=pl.BlockSpec((tm1, tk1), lambda i, j, kk: (i, kk)),
               b_spec=pl.BlockSpec((None, tk1, tn1), lambda i, j, kk: (j // (sc // tn1), kk, j % (sc // tn1))))
    y_sc = sc_mix_fwd(p, sc_conv_w, name="sc_mix_fwd")
    x2, yo1 = mm_nn(y_sc, w_sc_out, out_dtype=F32, name="out1", resid=x1, gate=gate[1:2], aux_dtype=ACT_DTYPE,
                    tn_cap=512)
    loss_t, dx2, d_fnw = loss_head(x2, final_norm_w[None], target, name="loss_head")

    d_yo1, d_gate1 = gate_bwd(dx2, yo1, gate[1:2], name="gate1_bwd")
    d_ysc = mm_nt(d_yo1, w_sc_out, out_dtype=F32, name="out1_dx")
    dw_sc_out = mm_tn(y_sc, d_yo1, out_dtype=WIRE_DTYPE, name="out1_dw")
    d_p, d_sc_conv = sc_mix_bwd(p, d_ysc, sc_conv_w, name="sc_mix_bwd")
    tkp = _tile(sc, MATMUL_TK, LANES)
    tnd = _tile(d, 1024, LANES)
    d_hn1 = matmul(d_p, w_sc_in, m=s, n=d, k=4 * sc, tm=tm1, tn=tnd, tk=tkp, dims=_NT, out_dtype=F32, name="proj1_dx",
                   a_spec=pl.BlockSpec((tm1, tkp), lambda i, j, kk: (i, kk)),
                   b_spec=pl.BlockSpec((None, tnd, tkp), lambda i, j, kk: (kk // (sc // tkp), j, kk % (sc // tkp))))
    tks = _tile(s, MATMUL_TK_TOKENS, 16)
    dw_sc_in = matmul(hn1, d_p, m=d, n=4 * sc, k=s, tm=tnd, tn=tn1, tk=tks, dims=_TN, out_dtype=WIRE_DTYPE,
                      name="proj1_dw", a_spec=pl.BlockSpec((tks, tnd), lambda i, j, kk: (kk, i)),
                      b_spec=pl.BlockSpec((tks, tn1), lambda i, j, kk: (kk, j)),
                      out_spec=pl.BlockSpec((None, tnd, tn1), lambda i, j, kk: (j // (sc // tn1), i, j % (sc // tn1))),
                      out_shape=(N_CHIPS, d, sc))
    dx1, d_nw1, d_scale1, d_shift1 = norm_mod_bwd(x1, norm_w[1:2], scale[1:2], d_hn1, row0=0, resid=dx2, init=None,
                                                  name="norm1_bwd")

    d_yo0, d_gate0 = gate_bwd(dx1, yo0, gate[0:1], name="gate0_bwd")
    core = ci.astype(jnp.int32).reshape(1)

    def chip_core_slots(t, slot_rows):
        return t.reshape(N_CHIPS, N_CORES, slot_rows // N_CORES, t.shape[-1])

    sc_by_chip = [chip_core_slots(dw_sc_in, d), chip_core_slots(dw_sc_out, sc // N_CHIPS)]
    d_y, sc_from_sib = mm_nt(d_yo0, w_ab_out, out_dtype=F32, name="out0_dx", side=SiblingSide(sc_by_chip, pick=True))
    dw_ab_out = mm_tn(y_ab, d_yo0, out_dtype=WIRE_DTYPE, name="out0_dw")
    d_o, d_gate_dn, d_dn_norm = dn_out_bwd(o_dn, proj, d_y, ab_dn_norm, heads=heads, gate_col0=gate_dn0, name="dn_out_bwd")
    early_by_chip = [chip_core_slots(dw_ab_out, ab_out // N_CHIPS)] + sc_by_chip
    ab_out_from_sib = sibling_exchange(early_by_chip[:1], pick=True, name="rs_cores_early")
    early_from_sib = list(ab_out_from_sib) + list(sc_from_sib)
    early_halves_sum = [add_sibling_half(core, a, b, name=f"rs_add_early{k}")
                        for k, (a, b) in enumerate(zip(early_by_chip, early_from_sib))]
    d_qkv, d_ba, d_alog_t, d_dtb_t, early_chips_sum = dn_bwd(qkv, ba, d_o, dn_states, a_log_t, dt_bias_t,
                                                             SideExchange(early_halves_sum, gather=False), s=s, cl=cl,
                                                             heads=heads, name="dn_bwd")
    d_qkv_raw, d_qkv_conv = qkv_conv_bwd(proj, qkv_conv, d_qkv, s=s, cl=cl, heads=heads, name="qkv_conv_bwd")
    d_h, d_gate_lru = lru_gate_bwd(h_lru, proj, d_y, gate_col0=gate_lru0, dy_col0=wdn // LANES, name="lru_gate_bwd")
    (d_xcm, d_lcw, d_lcb, d_wr, d_br, d_wi, d_bi, d_lam) = lru_bwd(xcm, to_col_major(d_h), *lru_w, s=s, cl=cl,
                                                                  name="lru_bwd")
    d_lru_in = jnp.concatenate([to_raster(d_xcm[:s]), d_xcm[s:]], axis=0)
    ctx_zeros = jnp.zeros((cl, wdn + wl), ACT_DTYPE)
    d_gates = jnp.concatenate([jnp.concatenate([d_gate_dn, d_gate_lru], axis=1), ctx_zeros], axis=0)
    d_proj = jnp.concatenate([d_qkv_raw, d_lru_in, d_gates], axis=1)
    dw_main = mm_tn(hn_all, d_proj, out_dtype=WIRE_DTYPE, name="proj0_dw")
    dw_ba = mm_tn(hn_all, d_ba, out_dtype=WIRE_DTYPE, name="proj0_ba_dw")
    dw_in_full = jnp.concatenate([dw_main[:, :off_beta], dw_ba[:, :4 * heads], dw_main[:, off_beta:]], axis=1)
    by_chip = [chip_core_slots(jnp.moveaxis(dw_in_full.reshape(d, N_CHIPS, ni), 1, 0), d)]
    from_sibling = sibling_exchange(by_chip, pick=True, name="rs_cores")
    halves_sum = [add_sibling_half(core, by_chip[0], from_sibling[0], name="rs_add")]
    d_hn_ba = mm_nt(d_ba, w_ba, out_dtype=F32, name="proj0_ba_dx")
    d_hn_all, last_chips_sum = mm_nt(d_proj, w_main, out_dtype=F32, name="proj0_dx", resid=d_hn_ba,
                                     side=SideExchange(halves_sum, gather=False))
    _, d_nw0c, d_scale_c, d_shift_c = norm_mod_bwd(ctx2d, norm_w[0:1], scale_c[None], d_hn_all, row0=s, resid=None,
                                                   init=None, name="norm0_bwd_ctx")
    dx, d_nw0, d_scale0, d_shift0 = norm_mod_bwd(x2d, norm_w[0:1], scale[0:1], d_hn_all, row0=0, resid=dx1, init=d_nw0c,
                                                 name="norm0_bwd")

    from_chips_sum = list(last_chips_sum) + list(early_chips_sum)
    reduced = [sum_slots(t, name=f"rs_sum{k}")[None] for k, t in enumerate(from_chips_sum)]
    both_halves = [core_halves(ci, a, b) for a, b in zip(reduced, sibling_exchange(reduced, pick=False, name="rs_back"))]
    grads = {n: t.reshape(weights[n].shape) for n, t in zip(BIG_WEIGHTS, both_halves)}

    d_mod_own = jnp.stack([jnp.concatenate([d_shift0[0], d_scale0[0], d_gate0[0]]),
                           jnp.concatenate([d_shift1[0], d_scale1[0], d_gate1[0]])])
    d_mod_ctx = jnp.concatenate([d_shift_c[0], d_scale_c[0], jnp.zeros((d,), F32)])
    summable = [
        loss_t[0, 0:1], d_mod_own.at[0].add(d_mod_ctx), d_mod_ctx, jnp.concatenate([d_nw0, d_nw1], axis=0), d_qkv_conv,
        d_alog_t[:, :2, 0].T, d_dtb_t[:, :2, 0].T, d_dn_norm[0], d_lcw, d_lcb[0], d_wr, d_br, d_wi, d_bi, d_lam,
        d_sc_conv, d_fnw[0]]
    sum_shapes = [t.shape for t in summable]
    gathered1 = all_gather_devices(_pack(summable + [d_mod_own]), "ag_small_grads")
    totals = _unpack(sum_slots(gathered1, name="sum_small_grads"), sum_shapes)
    (loss_sum, g_mod_b, d_mod_ctx_sum, g_norm_w, g_qkv_conv, g_a_log, g_dt_bias, g_dn_norm, g_lcw, g_lcb, g_wr, g_br,
     g_wi, g_bi, g_lam, g_sc_conv, g_fnw) = totals
    d_mod_rows = _unpack(gathered1, sum_shapes + [d_mod_own.shape])[-1]

    d_mod_all = jnp.zeros((2, MOD_ROWS, 3 * d), F32).at[:, :N_DEV].set(jnp.moveaxis(d_mod_rows, 0, 1))
    d_mod_all = d_mod_all.at[0, N_DEV].set(d_mod_ctx_sum)
    g_mod_w, dl_mod_w, nm_mod_w, nv_mod_w, d_silu = mod_adam(cond, own_columns(d_mod_all, ns), mod_w, m_mod_w, v_mod_w,
                                                             name="mod_adam")
    gathered2 = all_gather_devices(d_silu[0, N_DEV].reshape(-1, LANES), "ag_ctx_cond")
    g_c_ctx = ctx_cond_grad(gathered2[0::N_CORES], c_ctx.reshape(-1, LANES), name="ctx_cond_grad").reshape(d)

    grads.update({
        'c_ctx': g_c_ctx, 'mod_w': g_mod_w, 'mod_b': g_mod_b, 'norm_w': g_norm_w,
        'ab_qkv_conv': own_columns(g_qkv_conv, qkv_conv.shape[1] // N_CHIPS), 'ab_a_log': g_a_log, 'ab_dt_bias': g_dt_bias,
        'ab_dn_norm': g_dn_norm, 'ab_lru_conv_w': own_columns(g_lcw, wl // N_CHIPS), 'ab_lru_conv_b': g_lcb,
        'ab_lru_w_r': g_wr, 'ab_lru_b_r': own_columns(g_br, wl // N_CHIPS), 'ab_lru_w_i': g_wi,
        'ab_lru_b_i': own_columns(g_bi, wl // N_CHIPS), 'ab_lru_lambda': own_columns(g_lam, wl // N_CHIPS),
        'sc_conv': own_columns(g_sc_conv, sc // N_CHIPS), 'final_norm_w': g_fnw})
    grads = {n: grads[n].reshape(weights[n].shape) for n in WEIGHTS}

    delta, new_m, new_v = {'mod_w': dl_mod_w}, {'mod_w': nm_mod_w}, {'mod_w': nv_mod_w}
    for k, n in enumerate(BIG_WEIGHTS):
        as2d = lambda t: t.reshape(-1, t.shape[-1])
        upd = adam_update(as2d(weights[n]), as2d(grads[n]), as2d(mom1[n]), as2d(mom2[n]), name=f"adam_big{k}")
        delta[n], new_m[n], new_v[n] = (t.reshape(weights[n].shape) for t in upd)
    small = [n for n in WEIGHTS if n not in BIG_WEIGHTS and n != 'mod_w']
    small_shapes = [weights[n].shape for n in small]
    upd = adam_update(*[_pack([src[n] for n in small]) for src in (weights, grads, mom1, mom2)], name="adam_small")
    for out, flat in zip((delta, new_m, new_v), upd):
        out.update(dict(zip(small, _unpack(flat, small_shapes))))

    return (loss_sum[0], dx[None], *[grads[n] for n in WEIGHTS], *[delta[n] for n in WEIGHTS],
            *[new_m[n] for n in WEIGHTS], *[new_v[n] for n in WEIGHTS])
```
